```python
import math
import jax, jax.numpy as jnp
from jax import lax
import numpy as np

D_MODEL = 1024
BATCH = 8
SEQ = 2048
DEPTH = 2
DEC_BATCH = 128
DEC_SEQ = 4
PAST_LEN = 16384
PAGE_SIZE = 128

MIX_WIDTH = D_MODEL
SSM_WIDTH = MIX_WIDTH // 2
GMLP_WIDTH = MIX_WIDTH - SSM_WIDTH
SSM_GROUP = 16
SSM_GROUPS = SSM_WIDTH // SSM_GROUP
SSM_STATE = 64
CHUNK = 128
GMLP_HEADS = 4
GMLP_HEAD_DIM = GMLP_WIDTH // GMLP_HEADS
PLE_DIM = 256
D_FF = 4 * D_MODEL
IN_WIDTH = SSM_WIDTH + 2 * GMLP_WIDTH
EPS = 1e-6
DT_MIN = 1e-3
DT_MAX = 1e-1
LAM_RE_MAX = -1e-4

kernel_name = "hymba_s5_gmlp_decoder_step"


def rmsnorm(x, g):
    xf = x.astype(jnp.float32)
    r = lax.rsqrt(jnp.mean(xf * xf, axis=-1, keepdims=True) + EPS)
    return (xf * r * g.astype(jnp.float32)).astype(x.dtype)


def layernorm(x, g, b):
    xf = x.astype(jnp.float32)
    mu = jnp.mean(xf, axis=-1, keepdims=True)
    xc = xf - mu
    r = lax.rsqrt(jnp.mean(xc * xc, axis=-1, keepdims=True) + EPS)
    return (xc * r * g.astype(jnp.float32) + b.astype(jnp.float32)).astype(x.dtype)


def _combine(e1, e2):
    a1r, a1i, b1r, b1i = e1
    a2r, a2i, b2r, b2i = e2
    return (a2r * a1r - a2i * a1i,
            a2r * a1i + a2i * a1r,
            a2r * b1r - a2i * b1i + b2r,
            a2r * b1i + a2i * b1r + b2i)


def s5_scan(u, h0_re, h0_im, lam_re, lam_im, log_dt, b_re, b_im, c_re, c_im, d_skip):
    f32 = jnp.float32
    lr = jnp.minimum(lam_re.astype(f32), LAM_RE_MAX)
    li = lam_im.astype(f32)
    dt = jnp.exp(log_dt.astype(f32))[:, None]
    mag = jnp.exp(lr * dt)
    abr = mag * jnp.cos(li * dt)
    abi = mag * jnp.sin(li * dt)
    den = lr * lr + li * li
    nr = abr - 1.0
    ni = abi
    qr = (nr * lr + ni * li) / den
    qi = (ni * lr - nr * li) / den
    br = b_re.astype(f32)
    bi = b_im.astype(f32)
    bbr = qr[..., None] * br - qi[..., None] * bi
    bbi = qr[..., None] * bi + qi[..., None] * br
    n, seq_len, _ = u.shape
    ug = u.astype(f32).reshape(n, seq_len, SSM_GROUPS, SSM_GROUP)
    bur = jnp.einsum('nlgh,gph->nlgp', ug, bbr)
    bui = jnp.einsum('nlgh,gph->nlgp', ug, bbi)
    ar = jnp.broadcast_to(abr, bur.shape)
    ai = jnp.broadcast_to(abi, bur.shape)
    acr, aci, hr, hi = lax.associative_scan(_combine, (ar, ai, bur, bui), axis=1)
    h0r = h0_re.astype(f32)[:, None]
    h0i = h0_im.astype(f32)[:, None]
    hr = hr + acr * h0r - aci * h0i
    hi = hi + acr * h0i + aci * h0r
    y = (jnp.einsum('nlgp,ghp->nlgh', hr, c_re.astype(f32))
         - jnp.einsum('nlgp,ghp->nlgh', hi, c_im.astype(f32)))
    y = y.reshape(n, seq_len, SSM_WIDTH) + d_skip.astype(f32) * u.astype(f32)
    return y.astype(u.dtype), hr[:, -1], hi[:, -1]


def chunk_mix(v, w_s, b_s):
    n, seq_len, _ = v.shape
    pad = (-seq_len) % CHUNK
    vp = jnp.pad(v, ((0, 0), (0, pad), (0, 0)))
    nc = (seq_len + pad) // CHUNK
    vr = vp.reshape(n, nc, CHUNK, GMLP_HEADS, GMLP_HEAD_DIM)
    mask = jnp.tril(jnp.ones((CHUNK, CHUNK), dtype=bool))
    w = jnp.where(mask[None], w_s, jnp.zeros_like(w_s))
    out = jnp.einsum('hts,ncshd->ncthd', w, vr) + b_s.T[None, None, :, :, None]
    return out.reshape(n, nc * CHUNK, GMLP_WIDTH)[:, :seq_len]


def block(h, p_i, h0r, h0i, w, i):
    a = rmsnorm(h, w['g_mix'][i])
    z = a @ w['w_in'][i]
    u_s = z[..., :SSM_WIDTH]
    zg = jax.nn.gelu(z[..., SSM_WIDTH:])
    u_g = zg[..., :GMLP_WIDTH]
    v_g = zg[..., GMLP_WIDTH:]
    y_s, hr, hi = s5_scan(u_s, h0r, h0i, w['lam_re'][i], w['lam_im'][i], w['log_dt'][i],
                          w['b_re'][i], w['b_im'][i], w['c_re'][i], w['c_im'][i], w['d_skip'][i])
    y_s = jax.nn.gelu(y_s)
    y_s = y_s * jax.nn.sigmoid(y_s @ w['w_glu'][i] + w['b_glu'][i])
    v_n = layernorm(v_g, w['g_v'][i], w['b_v'][i])
    y_g = u_g * chunk_mix(v_n, w['w_s'][i], w['b_s'][i])
    y = jnp.concatenate([rmsnorm(y_s, w['g_out_s'][i]), rmsnorm(y_g, w['g_out_g'][i])], axis=-1)
    h = h + y @ w['w_out'][i]
    f = rmsnorm(h, w['g_ffn'][i])
    h = h + jnp.square(jax.nn.relu(f @ w['w_up'][i])) @ w['w_down'][i]
    gate = jax.nn.sigmoid(rmsnorm(h, w['g_ple'][i]) @ w['w_ple_gate'][i])
    h = h + gate * (p_i @ w['w_ple'][i])
    return h, hr, hi, v_n


def setup_inputs(seed: int = 0) -> dict:
    key = jax.random.key(seed)
    ks = iter(jax.random.split(key, 40))
    f32 = jnp.float32

    def nrm(shape, scale):
        return jax.random.normal(next(ks), shape, f32) * scale

    n_idx = jnp.arange(SSM_STATE, dtype=f32)
    inp = {}
    inp['x_prompt'] = nrm((BATCH, SEQ, D_MODEL), 1.0)
    inp['x_sample'] = nrm((DEC_BATCH, DEC_SEQ, D_MODEL), 1.0)
    inp['state_ssm_re'] = nrm((DEPTH, DEC_BATCH, SSM_GROUPS, SSM_STATE), 0.3)
    inp['state_ssm_im'] = nrm((DEPTH, DEC_BATCH, SSM_GROUPS, SSM_STATE), 0.3)
    inp['p_prompt'] = nrm((DEPTH, BATCH, SEQ, PLE_DIM), 1.0)
    inp['p_sample'] = nrm((DEPTH, DEC_BATCH, DEC_SEQ, PLE_DIM), 1.0)
    inp['g_mix'] = 1.0 + nrm((DEPTH, D_MODEL), 0.02)
    inp['w_in'] = nrm((DEPTH, D_MODEL, IN_WIDTH), D_MODEL ** -0.5)
    inp['lam_re'] = -0.5 + nrm((DEPTH, SSM_GROUPS, SSM_STATE), 0.01)
    inp['lam_im'] = math.pi * n_idx + nrm((DEPTH, SSM_GROUPS, SSM_STATE), 0.01)
    inp['log_dt'] = jax.random.uniform(next(ks), (DEPTH, SSM_GROUPS), f32,
                                       math.log(DT_MIN), math.log(DT_MAX))
    inp['b_re'] = nrm((DEPTH, SSM_GROUPS, SSM_STATE, SSM_GROUP), (2.0 * SSM_GROUP) ** -0.5)
    inp['b_im'] = nrm((DEPTH, SSM_GROUPS, SSM_STATE, SSM_GROUP), (2.0 * SSM_GROUP) ** -0.5)
    inp['c_re'] = nrm((DEPTH, SSM_GROUPS, SSM_GROUP, SSM_STATE), (2.0 * SSM_STATE) ** -0.5)
    inp['c_im'] = nrm((DEPTH, SSM_GROUPS, SSM_GROUP, SSM_STATE), (2.0 * SSM_STATE) ** -0.5)
    inp['d_skip'] = nrm((DEPTH, SSM_WIDTH), 1.0)
    inp['w_glu'] = nrm((DEPTH, SSM_WIDTH, SSM_WIDTH), SSM_WIDTH ** -0.5)
    inp['b_glu'] = nrm((DEPTH, SSM_WIDTH), 0.02)
    inp['g_v'] = 1.0 + nrm((DEPTH, GMLP_WIDTH), 0.02)
    inp['b_v'] = nrm((DEPTH, GMLP_WIDTH), 0.02)
    inp['w_s'] = nrm((DEPTH, GMLP_HEADS, CHUNK, CHUNK), 0.5 * CHUNK ** -0.5)
    inp['b_s'] = 1.0 + nrm((DEPTH, GMLP_HEADS, CHUNK), 0.1)
    inp['g_out_s'] = 1.0 + nrm((DEPTH, SSM_WIDTH), 0.02)
    inp['g_out_g'] = 1.0 + nrm((DEPTH, GMLP_WIDTH), 0.02)
    inp['w_out'] = nrm((DEPTH, MIX_WIDTH, D_MODEL), MIX_WIDTH ** -0.5)
    inp['g_ffn'] = 1.0 + nrm((DEPTH, D_MODEL), 0.02)
    inp['w_up'] = nrm((DEPTH, D_MODEL, D_FF), D_MODEL ** -0.5)
    inp['w_down'] = nrm((DEPTH, D_FF, D_MODEL), D_FF ** -0.5)
    inp['g_ple'] = 1.0 + nrm((DEPTH, D_MODEL), 0.02)
    inp['w_ple_gate'] = nrm((DEPTH, D_MODEL, D_MODEL), D_MODEL ** -0.5)
    inp['w_ple'] = nrm((DEPTH, PLE_DIM, D_MODEL), PLE_DIM ** -0.5)
    inp['g_final'] = 1.0 + nrm((D_MODEL,), 0.02)
    return inp


def reference(x_prompt, x_sample, state_ssm_re, state_ssm_im, p_prompt, p_sample,
              g_mix, w_in, lam_re, lam_im, log_dt, b_re, b_im, c_re, c_im, d_skip,
              w_glu, b_glu, g_v, b_v, w_s, b_s, g_out_s, g_out_g, w_out,
              g_ffn, w_up, w_down, g_ple, w_ple_gate, w_ple, g_final):
    w = dict(g_mix=g_mix, w_in=w_in, lam_re=lam_re, lam_im=lam_im, log_dt=log_dt,
             b_re=b_re, b_im=b_im, c_re=c_re, c_im=c_im, d_skip=d_skip,
             w_glu=w_glu, b_glu=b_glu, g_v=g_v, b_v=b_v, w_s=w_s, b_s=b_s,
             g_out_s=g_out_s, g_out_g=g_out_g, w_out=w_out, g_ffn=g_ffn,
             w_up=w_up, w_down=w_down, g_ple=g_ple, w_ple_gate=w_ple_gate, w_ple=w_ple)
    h_p = x_prompt
    zeros_state = jnp.zeros((BATCH, SSM_GROUPS, SSM_STATE), jnp.float32)
    re_p, im_p = [], []
    for i in range(DEPTH):
        h_p, hr, hi, _ = block(h_p, p_prompt[i], zeros_state, zeros_state, w, i)
        re_p.append(hr)
        im_p.append(hi)
    y_prompt = rmsnorm(h_p, g_final)
    h_s = x_sample
    re_s, im_s, v_s = [], [], []
    for i in range(DEPTH):
        h_s, hr, hi, v_n = block(h_s, p_sample[i], state_ssm_re[i], state_ssm_im[i], w, i)
        re_s.append(hr)
        im_s.append(hi)
        v_s.append(v_n)
    y_sample = rmsnorm(h_s, g_final)
    ssm_re_prompt = jnp.stack(re_p)
    ssm_im_prompt = jnp.stack(im_p)
    ssm_re_sample = jnp.stack(re_s)
    ssm_im_sample = jnp.stack(im_s)
    v_sample = jnp.stack(v_s)
    return (y_prompt, y_sample, ssm_re_prompt, ssm_im_prompt, ssm_re_sample, ssm_im_sample, v_sample)
```

```python
import functools
import math

import jax
import jax.numpy as jnp
from jax import lax
from jax.experimental import pallas as pl
from jax.experimental.pallas import tpu as pltpu

D_MODEL = 1024
DEPTH = 2
SSM_WIDTH = 512
GMLP_WIDTH = 512
SSM_GROUP = 16
SSM_GROUPS = 32
SSM_STATE = 64
SSM_FLAT = SSM_GROUPS * SSM_STATE
CHUNK = 128
GMLP_HEADS = 4
GMLP_HEAD_DIM = 128
PLE_DIM = 256
D_FF = 4096
IN_WIDTH = SSM_WIDTH + 2 * GMLP_WIDTH
EPS = 1e-6
LAM_RE_MAX = -1e-4

LANES = 128
SUBLANES = 8
S5_SLICES = SSM_WIDTH // LANES
S5_SLICE_STATES = SSM_FLAT // S5_SLICES
S5_SLABS = S5_SLICE_STATES // LANES
SCAN_PAD_ROWS = 8
ROW_TILE = 256
FFN_ROWS = 512
FFN_COLS = 1024
MIXER_VMEM_BYTES = 56 * 1024 * 1024
FFN_VMEM_BYTES = 52 * 1024 * 1024

_F32 = jnp.float32
_BF16 = jnp.bfloat16


def _dot(a, b):
    return jnp.dot(a, b, preferred_element_type=_F32)


def _rmsnorm(x, g):
    r = lax.rsqrt(jnp.mean(x * x, axis=-1, keepdims=True) + EPS)
    return x * r * g


def _gelu(x):
    c = math.sqrt(2.0 / math.pi)
    return x * (0.5 * (1.0 + jnp.tanh(c * (x + 0.044715 * (x * x * x)))))


def _disc_kernel(lr_ref, li_ref, ldt_ref, br_ref, bi_ref, abr_ref, abi_ref, bbr_ref, bbi_ref):
    lr = jnp.minimum(lr_ref[...], LAM_RE_MAX)
    li = li_ref[...]
    dt = jnp.exp(ldt_ref[...])
    mag = jnp.exp(lr * dt)
    abr = mag * jnp.cos(li * dt)
    abi = mag * jnp.sin(li * dt)
    den = lr * lr + li * li
    nr = abr - 1.0
    ni = abi
    qr = (nr * lr + ni * li) / den
    qi = (ni * lr - nr * li) / den
    abr_ref[...] = abr
    abi_ref[...] = abi
    for h in range(SSM_GROUP):
        br = br_ref[h]
        bi = bi_ref[h]
        bbr_ref[h] = qr * br - qi * bi
        bbi_ref[h] = qr * bi + qi * br


def _discretise(lam_re, lam_im, log_dt, b_re, b_im):
    dg = DEPTH * SSM_GROUPS
    lr = lam_re.reshape(dg, SSM_STATE)
    li = lam_im.reshape(dg, SSM_STATE)
    ldt = log_dt.reshape(dg, 1)
    br = jnp.moveaxis(b_re, -1, 0).reshape(SSM_GROUP, dg, SSM_STATE)
    bi = jnp.moveaxis(b_im, -1, 0).reshape(SSM_GROUP, dg, SSM_STATE)
    vec = jax.ShapeDtypeStruct((dg, SSM_STATE), _F32)
    mat = jax.ShapeDtypeStruct((SSM_GROUP, dg, SSM_STATE), _F32)
    abr, abi, bbr, bbi = pl.pallas_call(
        _disc_kernel, out_shape=(vec, vec, mat, mat), name="s5_discretise")(lr, li, ldt, br, bi)
    shp = (SSM_GROUP, DEPTH, SSM_GROUPS, SSM_STATE)
    return (abr.reshape(DEPTH, 1, SSM_FLAT), abi.reshape(DEPTH, 1, SSM_FLAT),
            jnp.moveaxis(bbr.reshape(shp), 0, 2), jnp.moveaxis(bbi.reshape(shp), 0, 2))


def _block_diag(m):
    ns, ng, a, b = m.shape
    eye = jnp.eye(ng, dtype=m.dtype)
    return jnp.einsum('sgab,gk->sgakb', m, eye).reshape(ns, ng * a, ng * b)


def _s5_matrices(bbr, bbi, c_re, c_im):
    gps = SSM_GROUPS // S5_SLICES
    bbr = bbr.reshape(S5_SLICES, gps, SSM_GROUP, SSM_STATE)
    bbi = bbi.reshape(S5_SLICES, gps, SSM_GROUP, SSM_STATE)
    bblk = jnp.concatenate([_block_diag(bbr), _block_diag(bbi)], axis=-1)
    cr = jnp.swapaxes(c_re, -1, -2).reshape(S5_SLICES, gps, SSM_STATE, SSM_GROUP)
    ci = jnp.swapaxes(c_im, -1, -2).reshape(S5_SLICES, gps, SSM_STATE, SSM_GROUP)
    cblk = jnp.concatenate([_block_diag(cr), -_block_diag(ci)], axis=-2)
    return bblk.astype(_BF16), cblk.astype(_BF16)


def _tile_rows(ref, i, rows):
    nb, tb, c = ref.shape
    if nb == 1:
        return ref[0, pl.ds(pl.multiple_of(i * rows, rows), rows), :]
    per = rows // tb
    return ref[pl.ds(i * per, per)].reshape(rows, c)


def _store_tile_rows(ref, i, rows, val):
    nb, tb, c = ref.shape
    if nb == 1:
        ref[0, pl.ds(pl.multiple_of(i * rows, rows), rows), :] = val
    else:
        per = rows // tb
        ref[pl.ds(i * per, per)] = val.reshape(per, tb, c)


MIXER_INPUTS = 19


def _mixer_kernel(*refs, nseq, t_len, pitch, keep_v):
    (x_ref, h0r_ref, h0i_ref, g_mix_ref, w_in_ref, abr_ref, abi_ref, bblk_ref, cblk_ref,
     dskip_ref, w_glu_ref, b_glu_ref, g_v_ref, b_v_ref, mixw_ref, mixb_ref,
     g_os_ref, g_og_ref, w_out_ref) = refs[:MIXER_INPUTS]
    out_ref, sre_ref, sim_ref = refs[MIXER_INPUTS:MIXER_INPUTS + 3]
    if keep_v:
        vn_ref, us_scr, ug_scr, ys_scr, yg_scr, bu_scr = refs[MIXER_INPUTS + 3:]
    else:
        us_scr, ug_scr, ys_scr, yg_scr, bu_scr, vn_ref = refs[MIXER_INPUTS + 3:]
    rows = nseq * t_len
    n_tiles = rows // ROW_TILE
    if pitch == t_len:
        grp_rows, grp_pitch, n_grp = rows, rows, 1
    else:
        grp_rows, grp_pitch, n_grp = t_len, pitch, nseq

    @pl.when(pl.program_id(0) == 0)
    def _():
        sre_ref[...] = h0r_ref[...]
        sim_ref[...] = h0i_ref[...]

    def in_proj(i, carry):
        x = _tile_rows(x_ref, i, ROW_TILE)
        a = _rmsnorm(x, g_mix_ref[...]).astype(_BF16)
        z = _dot(a, w_in_ref[...])
        r0 = pl.multiple_of(i * ROW_TILE, ROW_TILE)
        us_scr[pl.ds(r0, ROW_TILE), :] = z[:, :SSM_WIDTH]
        zg = _gelu(z[:, SSM_WIDTH:])
        ug_scr[pl.ds(r0, ROW_TILE), :] = zg[:, :GMLP_WIDTH]
        v = zg[:, GMLP_WIDTH:]
        mu = jnp.mean(v, axis=-1, keepdims=True)
        vc = v - mu
        r = lax.rsqrt(jnp.mean(vc * vc, axis=-1, keepdims=True) + EPS)
        vn_ref[pl.ds(r0, ROW_TILE), :] = vc * r * g_v_ref[...] + b_v_ref[...]
        return carry
    lax.fori_loop(0, n_tiles, in_proj, 0)

    row = lax.broadcasted_iota(jnp.int32, (CHUNK, CHUNK), 0)
    col = lax.broadcasted_iota(jnp.int32, (CHUNK, CHUNK), 1)
    t_shift = t_len.bit_length() - 1
    causal = ((row >> t_shift) == (col >> t_shift)) & ((col & (t_len - 1)) <= (row & (t_len - 1)))
    mixw = [jnp.where(causal, mixw_ref[h], 0.0).astype(_BF16) for h in range(GMLP_HEADS)]

    def chunk_mix(b, carry):
        r0 = pl.multiple_of(b * CHUNK, CHUNK)
        vn = vn_ref[pl.ds(r0, CHUNK), :].astype(_BF16)
        mixed = [_dot(mixw[h], vn[:, h * GMLP_HEAD_DIM:(h + 1) * GMLP_HEAD_DIM])
                 for h in range(GMLP_HEADS)]
        m = jnp.concatenate(mixed, axis=-1) + mixb_ref[...]
        yg = ug_scr[pl.ds(r0, CHUNK), :] * m
        yg_scr[pl.ds(r0, CHUNK), :] = _rmsnorm(yg, g_og_ref[...]).astype(_BF16)
        return carry
    lax.fori_loop(0, rows // CHUNK, chunk_mix, 0)

    for s in range(S5_SLICES):
        c0 = s * LANES

        def input_map(g, carry):
            src = pl.multiple_of(g * grp_rows, SUBLANES)
            dst = pl.multiple_of(g * grp_pitch, SUBLANES)
            u = us_scr[pl.ds(src, grp_rows), c0:c0 + LANES].astype(_BF16)
            bu = _dot(u, bblk_ref[s])
            for c in range(2 * S5_SLABS):
                bu_scr[c, pl.ds(dst, grp_rows), :] = bu[:, c * LANES:(c + 1) * LANES]
            return carry
        lax.fori_loop(0, n_grp, input_map, 0)

        s0 = s * S5_SLICE_STATES
        a_re = [jnp.broadcast_to(abr_ref[:, s0 + c * LANES:s0 + (c + 1) * LANES], (nseq, LANES))
                for c in range(S5_SLABS)]
        a_im = [jnp.broadcast_to(abi_ref[:, s0 + c * LANES:s0 + (c + 1) * LANES], (nseq, LANES))
                for c in range(S5_SLABS)]
        h_init = tuple(
            (sre_ref[:, s0 + c * LANES:s0 + (c + 1) * LANES], sim_ref[:, s0 + c * LANES:s0 + (c + 1) * LANES])
            for c in range(S5_SLABS))

        def scan_step(t, h):
            new = []
            for c in range(S5_SLABS):
                rows_t = pl.ds(t, nseq, stride=pitch)
                hr, hi = h[c]
                nhr = a_re[c] * hr - a_im[c] * hi + bu_scr[c, rows_t, :]
                nhi = a_re[c] * hi + a_im[c] * hr + bu_scr[S5_SLABS + c, rows_t, :]
                bu_scr[c, rows_t, :] = nhr
                bu_scr[S5_SLABS + c, rows_t, :] = nhi
                new.append((nhr, nhi))
            return tuple(new)
        h_fin = lax.fori_loop(0, t_len, scan_step, h_init)
        for c in range(S5_SLABS):
            sre_ref[:, s0 + c * LANES:s0 + (c + 1) * LANES] = h_fin[c][0]
            sim_ref[:, s0 + c * LANES:s0 + (c + 1) * LANES] = h_fin[c][1]

        def output_map(g, carry):
            src = pl.multiple_of(g * grp_pitch, SUBLANES)
            dst = pl.multiple_of(g * grp_rows, SUBLANES)
            hs = jnp.concatenate([bu_scr[c, pl.ds(src, grp_rows), :] for c in range(2 * S5_SLABS)], axis=-1)
            ys_scr[pl.ds(dst, grp_rows), c0:c0 + LANES] = _dot(hs.astype(_BF16), cblk_ref[s])
            return carry
        lax.fori_loop(0, n_grp, output_map, 0)

    def out_proj(i, carry):
        r0 = pl.multiple_of(i * ROW_TILE, ROW_TILE)
        ys = ys_scr[pl.ds(r0, ROW_TILE), :] + dskip_ref[...] * us_scr[pl.ds(r0, ROW_TILE), :]
        ys = _gelu(ys)
        ys = ys * jax.nn.sigmoid(_dot(ys.astype(_BF16), w_glu_ref[...]) + b_glu_ref[...])
        ys = _rmsnorm(ys, g_os_ref[...]).astype(_BF16)
        y = _dot(ys, w_out_ref[:SSM_WIDTH, :]) + _dot(yg_scr[pl.ds(r0, ROW_TILE), :], w_out_ref[SSM_WIDTH:, :])
        _store_tile_rows(out_ref, i, ROW_TILE, _tile_rows(x_ref, i, ROW_TILE) + y)
        return carry
    lax.fori_loop(0, n_tiles, out_proj, 0)


def _const_spec(shape):
    nd = len(shape)
    return pl.BlockSpec(shape, lambda j: (0,) * nd, pipeline_mode=pl.Buffered(1))


def _mixer(x, h0r, h0i, wl, *, nseq, t_len, pitch, keep_v):
    nb, length, _ = x.shape
    t_blk = t_len if nb == nseq else nseq * t_len
    rows = nseq * t_len
    steps = length // t_blk
    n_buf_rows = nseq * pitch
    x_spec = pl.BlockSpec((nb, t_blk, D_MODEL), lambda j: (0, j, 0))
    st_spec = pl.BlockSpec((nseq, SSM_FLAT), lambda j: (0, 0))
    consts = [wl['g_mix'], wl['w_in'], wl['abr'], wl['abi'], wl['bblk'], wl['cblk'], wl['d_skip'],
              wl['w_glu'], wl['b_glu'], wl['g_v'], wl['b_v'], wl['mixw'], wl['mixb'],
              wl['g_out_s'], wl['g_out_g'], wl['w_out']]
    assert len(consts) + 3 == MIXER_INPUTS
    out_specs = [x_spec, st_spec, st_spec]
    out_shape = [jax.ShapeDtypeStruct(x.shape, _F32),
                 jax.ShapeDtypeStruct((nseq, SSM_FLAT), _F32),
                 jax.ShapeDtypeStruct((nseq, SSM_FLAT), _F32)]
    scratch = [pltpu.VMEM((rows, SSM_WIDTH), _F32),
               pltpu.VMEM((rows, GMLP_WIDTH), _F32),
               pltpu.VMEM((rows, SSM_WIDTH), _F32),
               pltpu.VMEM((rows, GMLP_WIDTH), _BF16),
               pltpu.VMEM((2 * S5_SLABS, n_buf_rows, LANES), _F32)]
    if keep_v:
        out_specs.append(pl.BlockSpec((rows, GMLP_WIDTH), lambda j: (j, 0)))
        out_shape.append(jax.ShapeDtypeStruct((steps * rows, GMLP_WIDTH), _F32))
    else:
        scratch.append(pltpu.VMEM((rows, GMLP_WIDTH), _F32))
    res = pl.pallas_call(
        functools.partial(_mixer_kernel, nseq=nseq, t_len=t_len, pitch=pitch, keep_v=keep_v),
        grid=(steps,),
        in_specs=[x_spec, st_spec, st_spec] + [_const_spec(c.shape) for c in consts],
        out_specs=tuple(out_specs),
        out_shape=tuple(out_shape),
        scratch_shapes=scratch,
        compiler_params=pltpu.CompilerParams(dimension_semantics=("arbitrary",),
                                             vmem_limit_bytes=MIXER_VMEM_BYTES),
        name="mixer_sample" if keep_v else "mixer_prompt",
    )(x, h0r, h0i, *consts)
    return res if keep_v else (*res, None)


def _ffn_kernel(h_ref, p_ref, g_ffn_ref, w_up_ref, w_down_ref, g_ple_ref, w_gate_ref, w_ple_ref, g_fin_ref,
                out_ref, *, final):
    h = h_ref[...]
    f = _rmsnorm(h, g_ffn_ref[...]).astype(_BF16)
    acc = h
    for c in range(D_FF // FFN_COLS):
        up = _dot(f, w_up_ref[:, c * FFN_COLS:(c + 1) * FFN_COLS])
        act = jnp.square(jnp.maximum(up, 0.0)).astype(_BF16)
        acc = acc + _dot(act, w_down_ref[c * FFN_COLS:(c + 1) * FFN_COLS, :])
    gate = jax.nn.sigmoid(_dot(_rmsnorm(acc, g_ple_ref[...]).astype(_BF16), w_gate_ref[...]))
    out = acc + gate * _dot(p_ref[...].astype(_BF16), w_ple_ref[...])
    if final:
        out = _rmsnorm(out, g_fin_ref[...])
    out_ref[...] = out


def _ffn(h, p, wl, g_final, *, final):
    n_rows = h.shape[0]
    consts = [wl['g_ffn'], wl['w_up'], wl['w_down'], wl['g_ple'], wl['w_ple_gate'], wl['w_ple'], g_final]
    return pl.pallas_call(
        functools.partial(_ffn_kernel, final=final),
        grid=(n_rows // FFN_ROWS,),
        in_specs=[pl.BlockSpec((FFN_ROWS, D_MODEL), lambda j: (j, 0)),
                  pl.BlockSpec((FFN_ROWS, PLE_DIM), lambda j: (j, 0))] + [_const_spec(c.shape) for c in consts],
        out_specs=pl.BlockSpec((FFN_ROWS, D_MODEL), lambda j: (j, 0)),
        out_shape=jax.ShapeDtypeStruct((n_rows, D_MODEL), _F32),
        compiler_params=pltpu.CompilerParams(dimension_semantics=("arbitrary",),
                                             vmem_limit_bytes=FFN_VMEM_BYTES),
        name="ffn",
    )(h, p, *consts)


def kernel(x_prompt, x_sample, state_ssm_re, state_ssm_im, p_prompt, p_sample, g_mix, w_in, lam_re, lam_im, log_dt, b_re, b_im, c_re, c_im, d_skip, w_glu, b_glu, g_v, b_v, w_s, b_s, g_out_s, g_out_g, w_out, g_ffn, w_up, w_down, g_ple, w_ple_gate, w_ple, g_final):
    batch, seq, _ = x_prompt.shape
    dec_batch, dec_seq, _ = x_sample.shape
    assert seq % CHUNK == 0 and batch == SUBLANES
    assert CHUNK % dec_seq == 0 and (dec_batch * dec_seq) % ROW_TILE == 0

    abr, abi, bbr, bbi = _discretise(lam_re, lam_im, log_dt, b_re, b_im)
    row = lambda v: v.reshape(1, -1)
    layers = []
    for i in range(DEPTH):
        bblk, cblk = _s5_matrices(bbr[i], bbi[i], c_re[i], c_im[i])
        bias = jnp.repeat(b_s[i].T, GMLP_HEAD_DIM, axis=1)
        reps = CHUNK // dec_seq
        layers.append(dict(
            g_mix=row(g_mix[i]), w_in=w_in[i].astype(_BF16), abr=abr[i], abi=abi[i], bblk=bblk, cblk=cblk,
            d_skip=row(d_skip[i]), w_glu=w_glu[i].astype(_BF16), b_glu=row(b_glu[i]),
            g_v=row(g_v[i]), b_v=row(b_v[i]), g_out_s=row(g_out_s[i]), g_out_g=row(g_out_g[i]),
            w_out=w_out[i].astype(_BF16), g_ffn=row(g_ffn[i]), w_up=w_up[i].astype(_BF16),
            w_down=w_down[i].astype(_BF16), g_ple=row(g_ple[i]), w_ple_gate=w_ple_gate[i].astype(_BF16),
            w_ple=w_ple[i].astype(_BF16),
            mixw_p=w_s[i], mixb_p=bias,
            mixw_s=jnp.tile(w_s[i][:, :dec_seq, :dec_seq], (1, reps, reps)),
            mixb_s=jnp.tile(bias[:dec_seq], (reps, 1))))
    g_fin = row(g_final)

    zeros = jnp.zeros((batch, SSM_FLAT), _F32)
    h_p = x_prompt
    h_s = x_sample.reshape(1, dec_batch * dec_seq, D_MODEL)
    re_p, im_p, re_s, im_s, v_s = [], [], [], [], []
    for i, wl in enumerate(layers):
        final = i == DEPTH - 1
        wl_p = dict(wl, mixw=wl['mixw_p'], mixb=wl['mixb_p'])
        h_p, sre, sim, _ = _mixer(h_p, zeros, zeros, wl_p, nseq=batch, t_len=CHUNK,
                                  pitch=CHUNK + SCAN_PAD_ROWS, keep_v=False)
        re_p.append(sre.reshape(batch, SSM_GROUPS, SSM_STATE))
        im_p.append(sim.reshape(batch, SSM_GROUPS, SSM_STATE))
        h_p = _ffn(h_p.reshape(batch * seq, D_MODEL), p_prompt[i].reshape(batch * seq, PLE_DIM),
                   wl, g_fin, final=final).reshape(batch, seq, D_MODEL)

        wl_s = dict(wl, mixw=wl['mixw_s'], mixb=wl['mixb_s'])
        h_s, sre, sim, vn = _mixer(h_s, state_ssm_re[i].reshape(dec_batch, SSM_FLAT),
                                   state_ssm_im[i].reshape(dec_batch, SSM_FLAT), wl_s,
                                   nseq=dec_batch, t_len=dec_seq, pitch=dec_seq, keep_v=True)
        re_s.append(sre.reshape(dec_batch, SSM_GROUPS, SSM_STATE))
        im_s.append(sim.reshape(dec_batch, SSM_GROUPS, SSM_STATE))
        v_s.append(vn.reshape(dec_batch, dec_seq, GMLP_WIDTH))
        h_s = _ffn(h_s.reshape(dec_batch * dec_seq, D_MODEL),
                   p_sample[i].reshape(dec_batch * dec_seq, PLE_DIM),
                   wl, g_fin, final=final).reshape(1, dec_batch * dec_seq, D_MODEL)

    y_sample = h_s.reshape(dec_batch, dec_seq, D_MODEL)
    return (h_p, y_sample, jnp.stack(re_p), jnp.stack(im_p), jnp.stack(re_s), jnp.stack(im_s), jnp.stack(v_s))
```

```python
import functools
import math

import jax
import jax.numpy as jnp
from jax import lax
from jax.experimental import pallas as pl
from jax.experimental.pallas import tpu as pltpu

D_MODEL = 1024
DEPTH = 2
SSM_WIDTH = 512
GMLP_WIDTH = 512
SSM_GROUP = 16
SSM_GROUPS = 32
SSM_STATE = 64
SSM_FLAT = SSM_GROUPS * SSM_STATE
CHUNK = 128
GMLP_HEADS = 4
GMLP_HEAD_DIM = 128
PLE_DIM = 256
D_FF = 4096
IN_WIDTH = SSM_WIDTH + 2 * GMLP_WIDTH
EPS = 1e-6
LAM_RE_MAX = -1e-4

LANES = 128
SUBLANES = 8
S5_SLICES = SSM_WIDTH // LANES
S5_SLICE_STATES = SSM_FLAT // S5_SLICES
S5_SLABS = S5_SLICE_STATES // LANES
SCAN_PAD_ROWS = 4
SCAN_UNROLL = 2
ROW_TILE = 256
FFN_ROWS = 512
FFN_COLS = 1024
MIXER_VMEM_BYTES = 56 * 1024 * 1024
FFN_VMEM_BYTES = 52 * 1024 * 1024

_F32 = jnp.float32
_BF16 = jnp.bfloat16


def _dot(a, b):
    return jnp.dot(a, b, preferred_element_type=_F32)


def _rmsnorm(x, g):
    r = lax.rsqrt(jnp.mean(x * x, axis=-1, keepdims=True) + EPS)
    return x * r * g


def _gelu(x):
    c = math.sqrt(2.0 / math.pi)
    return x * (0.5 * (1.0 + jnp.tanh(c * (x + 0.044715 * (x * x * x)))))


def _disc_kernel(lr_ref, li_ref, ldt_ref, br_ref, bi_ref, abr_ref, abi_ref, bbr_ref, bbi_ref):
    lr = jnp.minimum(lr_ref[...], LAM_RE_MAX)
    li = li_ref[...]
    dt = jnp.exp(ldt_ref[...])
    mag = jnp.exp(lr * dt)
    abr = mag * jnp.cos(li * dt)
    abi = mag * jnp.sin(li * dt)
    den = lr * lr + li * li
    nr = abr - 1.0
    ni = abi
    qr = (nr * lr + ni * li) / den
    qi = (ni * lr - nr * li) / den
    abr_ref[...] = abr
    abi_ref[...] = abi
    for h in range(SSM_GROUP):
        br = br_ref[h]
        bi = bi_ref[h]
        bbr_ref[h] = qr * br - qi * bi
        bbi_ref[h] = qr * bi + qi * br


def _discretise(lam_re, lam_im, log_dt, b_re, b_im):
    dg = DEPTH * SSM_GROUPS
    lr = lam_re.reshape(dg, SSM_STATE)
    li = lam_im.reshape(dg, SSM_STATE)
    ldt = log_dt.reshape(dg, 1)
    br = jnp.moveaxis(b_re, -1, 0).reshape(SSM_GROUP, dg, SSM_STATE)
    bi = jnp.moveaxis(b_im, -1, 0).reshape(SSM_GROUP, dg, SSM_STATE)
    vec = jax.ShapeDtypeStruct((dg, SSM_STATE), _F32)
    mat = jax.ShapeDtypeStruct((SSM_GROUP, dg, SSM_STATE), _F32)
    abr, abi, bbr, bbi = pl.pallas_call(
        _disc_kernel, out_shape=(vec, vec, mat, mat), name="s5_discretise")(lr, li, ldt, br, bi)
    shp = (SSM_GROUP, DEPTH, SSM_GROUPS, SSM_STATE)
    return (abr.reshape(DEPTH, 1, SSM_FLAT), abi.reshape(DEPTH, 1, SSM_FLAT),
            jnp.moveaxis(bbr.reshape(shp), 0, 2), jnp.moveaxis(bbi.reshape(shp), 0, 2))


def _block_diag(m):
    ns, ng, a, b = m.shape
    eye = jnp.eye(ng, dtype=m.dtype)
    return jnp.einsum('sgab,gk->sgakb', m, eye).reshape(ns, ng * a, ng * b)


def _s5_matrices(bbr, bbi, c_re, c_im):
    gps = SSM_GROUPS // S5_SLICES
    bbr = bbr.reshape(S5_SLICES, gps, SSM_GROUP, SSM_STATE)
    bbi = bbi.reshape(S5_SLICES, gps, SSM_GROUP, SSM_STATE)
    bblk = jnp.concatenate([_block_diag(bbr), _block_diag(bbi)], axis=-1)
    cr = jnp.swapaxes(c_re, -1, -2).reshape(S5_SLICES, gps, SSM_STATE, SSM_GROUP)
    ci = jnp.swapaxes(c_im, -1, -2).reshape(S5_SLICES, gps, SSM_STATE, SSM_GROUP)
    cblk = jnp.concatenate([_block_diag(cr), -_block_diag(ci)], axis=-2)
    return bblk.astype(_BF16), cblk.astype(_BF16)


def _tile_rows(ref, i, rows):
    nb, tb, c = ref.shape
    if nb == 1:
        return ref[0, i * rows:(i + 1) * rows, :]
    per = rows // tb
    return ref[i * per:(i + 1) * per].reshape(rows, c)


def _store_tile_rows(ref, i, rows, val):
    nb, tb, c = ref.shape
    if nb == 1:
        ref[0, i * rows:(i + 1) * rows, :] = val
    else:
        per = rows // tb
        ref[i * per:(i + 1) * per] = val.reshape(per, tb, c)


def _load_pitched(ref, i, rows, t_len, pitch):
    if pitch == t_len:
        return ref[i * rows:(i + 1) * rows, :]
    per = rows // t_len
    return jnp.concatenate([ref[n * pitch:n * pitch + t_len, :] for n in range(i * per, (i + 1) * per)], axis=0)


def _store_pitched(ref, i, rows, t_len, pitch, val):
    if pitch == t_len:
        ref[i * rows:(i + 1) * rows, :] = val
        return
    per = rows // t_len
    for k in range(per):
        n = i * per + k
        ref[n * pitch:n * pitch + t_len, :] = val[k * t_len:(k + 1) * t_len, :]


MIXER_INPUTS = 19


def _mixer_kernel(*refs, nseq, t_len, pitch, keep_v):
    (x_ref, h0r_ref, h0i_ref, g_mix_ref, w_in_ref, abr_ref, abi_ref, bblk_ref, cblk_ref,
     dskip_ref, w_glu_ref, b_glu_ref, g_v_ref, b_v_ref, mixw_ref, mixb_ref,
     g_os_ref, g_og_ref, w_out_ref) = refs[:MIXER_INPUTS]
    out_ref, sre_ref, sim_ref = refs[MIXER_INPUTS:MIXER_INPUTS + 3]
    if keep_v:
        vn_ref, us_scr, ug_scr, ys_scr, yg_scr, bu_scr = refs[MIXER_INPUTS + 3:]
    else:
        us_scr, ug_scr, ys_scr, yg_scr, bu_scr, vn_ref = refs[MIXER_INPUTS + 3:]
    rows = nseq * t_len
    n_tiles = rows // ROW_TILE

    @pl.when(pl.program_id(0) == 0)
    def _():
        sre_ref[...] = h0r_ref[...]
        sim_ref[...] = h0i_ref[...]
        us_scr[...] = jnp.zeros_like(us_scr)

    for i in range(n_tiles):
        x = _tile_rows(x_ref, i, ROW_TILE)
        a = _rmsnorm(x, g_mix_ref[...]).astype(_BF16)
        z = _dot(a, w_in_ref[...])
        _store_pitched(us_scr, i, ROW_TILE, t_len, pitch, z[:, :SSM_WIDTH])
        zg = _gelu(z[:, SSM_WIDTH:])
        ug_scr[i * ROW_TILE:(i + 1) * ROW_TILE, :] = zg[:, :GMLP_WIDTH]
        v = zg[:, GMLP_WIDTH:]
        mu = jnp.mean(v, axis=-1, keepdims=True)
        vc = v - mu
        r = lax.rsqrt(jnp.mean(vc * vc, axis=-1, keepdims=True) + EPS)
        vn_ref[i * ROW_TILE:(i + 1) * ROW_TILE, :] = vc * r * g_v_ref[...] + b_v_ref[...]

    row = lax.broadcasted_iota(jnp.int32, (CHUNK, CHUNK), 0)
    col = lax.broadcasted_iota(jnp.int32, (CHUNK, CHUNK), 1)
    t_shift = t_len.bit_length() - 1
    causal = ((row >> t_shift) == (col >> t_shift)) & ((col & (t_len - 1)) <= (row & (t_len - 1)))
    mixw = [jnp.where(causal, mixw_ref[h], 0.0).astype(_BF16) for h in range(GMLP_HEADS)]

    def chunk_mix(b, carry):
        r0 = pl.multiple_of(b * CHUNK, CHUNK)
        vn = vn_ref[pl.ds(r0, CHUNK), :].astype(_BF16)
        mixed = [_dot(mixw[h], vn[:, h * GMLP_HEAD_DIM:(h + 1) * GMLP_HEAD_DIM])
                 for h in range(GMLP_HEADS)]
        m = jnp.concatenate(mixed, axis=-1) + mixb_ref[...]
        yg = ug_scr[pl.ds(r0, CHUNK), :] * m
        yg_scr[pl.ds(r0, CHUNK), :] = _rmsnorm(yg, g_og_ref[...]).astype(_BF16)
        return carry
    lax.fori_loop(0, rows // CHUNK, chunk_mix, 0)

    for s in range(S5_SLICES):
        c0 = s * LANES
        bu = _dot(us_scr[:, c0:c0 + LANES].astype(_BF16), bblk_ref[s])
        for c in range(2 * S5_SLABS):
            bu_scr[c] = bu[:, c * LANES:(c + 1) * LANES]

        s0 = s * S5_SLICE_STATES
        a_re = [jnp.broadcast_to(abr_ref[:, s0 + c * LANES:s0 + (c + 1) * LANES], (nseq, LANES))
                for c in range(S5_SLABS)]
        a_im = [jnp.broadcast_to(abi_ref[:, s0 + c * LANES:s0 + (c + 1) * LANES], (nseq, LANES))
                for c in range(S5_SLABS)]
        h_init = tuple(
            (sre_ref[:, s0 + c * LANES:s0 + (c + 1) * LANES], sim_ref[:, s0 + c * LANES:s0 + (c + 1) * LANES])
            for c in range(S5_SLABS))

        def scan_step(t, h):
            new = []
            for c in range(S5_SLABS):
                rows_t = pl.ds(t, nseq, stride=pitch)
                hr, hi = h[c]
                nhr = a_re[c] * hr - a_im[c] * hi + bu_scr[c, rows_t, :]
                nhi = a_re[c] * hi + a_im[c] * hr + bu_scr[S5_SLABS + c, rows_t, :]
                bu_scr[c, rows_t, :] = nhr
                bu_scr[S5_SLABS + c, rows_t, :] = nhi
                new.append((nhr, nhi))
            return tuple(new)
        h_fin = lax.fori_loop(0, t_len, scan_step, h_init, unroll=SCAN_UNROLL)
        for c in range(S5_SLABS):
            sre_ref[:, s0 + c * LANES:s0 + (c + 1) * LANES] = h_fin[c][0]
            sim_ref[:, s0 + c * LANES:s0 + (c + 1) * LANES] = h_fin[c][1]

        hs = jnp.concatenate([bu_scr[c] for c in range(2 * S5_SLABS)], axis=-1).astype(_BF16)
        ys_scr[:, c0:c0 + LANES] = _dot(hs, cblk_ref[s])

    for i in range(n_tiles):
        ys = (_load_pitched(ys_scr, i, ROW_TILE, t_len, pitch)
              + dskip_ref[...] * _load_pitched(us_scr, i, ROW_TILE, t_len, pitch))
        ys = _gelu(ys)
        ys = ys * jax.nn.sigmoid(_dot(ys.astype(_BF16), w_glu_ref[...]) + b_glu_ref[...])
        ys = _rmsnorm(ys, g_os_ref[...]).astype(_BF16)
        y = (_dot(ys, w_out_ref[:SSM_WIDTH, :])
             + _dot(yg_scr[i * ROW_TILE:(i + 1) * ROW_TILE, :], w_out_ref[SSM_WIDTH:, :]))
        _store_tile_rows(out_ref, i, ROW_TILE, _tile_rows(x_ref, i, ROW_TILE) + y)


def _const_spec(shape):
    nd = len(shape)
    return pl.BlockSpec(shape, lambda j: (0,) * nd, pipeline_mode=pl.Buffered(1))


def _mixer(x, h0r, h0i, wl, *, nseq, t_len, pitch, keep_v):
    nb, length, _ = x.shape
    t_blk = t_len if nb == nseq else nseq * t_len
    rows = nseq * t_len
    steps = length // t_blk
    n_buf_rows = nseq * pitch
    x_spec = pl.BlockSpec((nb, t_blk, D_MODEL), lambda j: (0, j, 0))
    st_spec = pl.BlockSpec((nseq, SSM_FLAT), lambda j: (0, 0))
    consts = [wl['g_mix'], wl['w_in'], wl['abr'], wl['abi'], wl['bblk'], wl['cblk'], wl['d_skip'],
              wl['w_glu'], wl['b_glu'], wl['g_v'], wl['b_v'], wl['mixw'], wl['mixb'],
              wl['g_out_s'], wl['g_out_g'], wl['w_out']]
    assert len(consts) + 3 == MIXER_INPUTS
    out_specs = [x_spec, st_spec, st_spec]
    out_shape = [jax.ShapeDtypeStruct(x.shape, _F32),
                 jax.ShapeDtypeStruct((nseq, SSM_FLAT), _F32),
                 jax.ShapeDtypeStruct((nseq, SSM_FLAT), _F32)]
    scratch = [pltpu.VMEM((n_buf_rows, SSM_WIDTH), _F32),
               pltpu.VMEM((rows, GMLP_WIDTH), _F32),
               pltpu.VMEM((n_buf_rows, SSM_WIDTH), _F32),
               pltpu.VMEM((rows, GMLP_WIDTH), _BF16),
               pltpu.VMEM((2 * S5_SLABS, n_buf_rows, LANES), _F32)]
    if keep_v:
        out_specs.append(pl.BlockSpec((rows, GMLP_WIDTH), lambda j: (j, 0)))
        out_shape.append(jax.ShapeDtypeStruct((steps * rows, GMLP_WIDTH), _F32))
    else:
        scratch.append(pltpu.VMEM((rows, GMLP_WIDTH), _F32))
    res = pl.pallas_call(
        functools.partial(_mixer_kernel, nseq=nseq, t_len=t_len, pitch=pitch, keep_v=keep_v),
        grid=(steps,),
        in_specs=[x_spec, st_spec, st_spec] + [_const_spec(c.shape) for c in consts],
        out_specs=tuple(out_specs),
        out_shape=tuple(out_shape),
        scratch_shapes=scratch,
        compiler_params=pltpu.CompilerParams(dimension_semantics=("arbitrary",),
                                             vmem_limit_bytes=MIXER_VMEM_BYTES),
        name="mixer_sample" if keep_v else "mixer_prompt",
    )(x, h0r, h0i, *consts)
    return res if keep_v else (*res, None)


def _ffn_kernel(h_ref, p_ref, g_ffn_ref, w_up_ref, w_down_ref, g_ple_ref, w_gate_ref, w_ple_ref, g_fin_ref,
                out_ref, *, final):
    h = h_ref[...]
    f = _rmsnorm(h, g_ffn_ref[...]).astype(_BF16)
    acc = h
    for c in range(D_FF // FFN_COLS):
        up = _dot(f, w_up_ref[:, c * FFN_COLS:(c + 1) * FFN_COLS])
        act = jnp.square(jnp.maximum(up, 0.0)).astype(_BF16)
        acc = acc + _dot(act, w_down_ref[c * FFN_COLS:(c + 1) * FFN_COLS, :])
    gate = jax.nn.sigmoid(_dot(_rmsnorm(acc, g_ple_ref[...]).astype(_BF16), w_gate_ref[...]))
    out = acc + gate * _dot(p_ref[...].astype(_BF16), w_ple_ref[...])
    if final:
        out = _rmsnorm(out, g_fin_ref[...])
    out_ref[...] = out


def _ffn(h, p, wl, g_final, *, final):
    n_rows = h.shape[0]
    consts = [wl['g_ffn'], wl['w_up'], wl['w_down'], wl['g_ple'], wl['w_ple_gate'], wl['w_ple'], g_final]
    return pl.pallas_call(
        functools.partial(_ffn_kernel, final=final),
        grid=(n_rows // FFN_ROWS,),
        in_specs=[pl.BlockSpec((FFN_ROWS, D_MODEL), lambda j: (j, 0)),
                  pl.BlockSpec((FFN_ROWS, PLE_DIM), lambda j: (j, 0))] + [_const_spec(c.shape) for c in consts],
        out_specs=pl.BlockSpec((FFN_ROWS, D_MODEL), lambda j: (j, 0)),
        out_shape=jax.ShapeDtypeStruct((n_rows, D_MODEL), _F32),
        compiler_params=pltpu.CompilerParams(dimension_semantics=("arbitrary",),
                                             vmem_limit_bytes=FFN_VMEM_BYTES),
        name="ffn",
    )(h, p, *consts)


def kernel(x_prompt, x_sample, state_ssm_re, state_ssm_im, p_prompt, p_sample, g_mix, w_in, lam_re, lam_im, log_dt, b_re, b_im, c_re, c_im, d_skip, w_glu, b_glu, g_v, b_v, w_s, b_s, g_out_s, g_out_g, w_out, g_ffn, w_up, w_down, g_ple, w_ple_gate, w_ple, g_final):
    batch, seq, _ = x_prompt.shape
    dec_batch, dec_seq, _ = x_sample.shape
    assert seq % CHUNK == 0 and batch == SUBLANES
    assert CHUNK % dec_seq == 0 and (dec_batch * dec_seq) % ROW_TILE == 0

    abr, abi, bbr, bbi = _discretise(lam_re, lam_im, log_dt, b_re, b_im)
    row = lambda v: v.reshape(1, -1)
    layers = []
    for i in range(DEPTH):
        bblk, cblk = _s5_matrices(bbr[i], bbi[i], c_re[i], c_im[i])
        bias = jnp.repeat(b_s[i].T, GMLP_HEAD_DIM, axis=1)
        reps = CHUNK // dec_seq
        layers.append(dict(
            g_mix=row(g_mix[i]), w_in=w_in[i].astype(_BF16), abr=abr[i], abi=abi[i], bblk=bblk, cblk=cblk,
            d_skip=row(d_skip[i]), w_glu=w_glu[i].astype(_BF16), b_glu=row(b_glu[i]),
            g_v=row(g_v[i]), b_v=row(b_v[i]), g_out_s=row(g_out_s[i]), g_out_g=row(g_out_g[i]),
            w_out=w_out[i].astype(_BF16), g_ffn=row(g_ffn[i]), w_up=w_up[i].astype(_BF16),
            w_down=w_down[i].astype(_BF16), g_ple=row(g_ple[i]), w_ple_gate=w_ple_gate[i].astype(_BF16),
            w_ple=w_ple[i].astype(_BF16),
            mixw_p=w_s[i], mixb_p=bias,
            mixw_s=jnp.tile(w_s[i][:, :dec_seq, :dec_seq], (1, reps, reps)),
            mixb_s=jnp.tile(bias[:dec_seq], (reps, 1))))
    g_fin = row(g_final)

    zeros = jnp.zeros((batch, SSM_FLAT), _F32)
    h_p = x_prompt
    h_s = x_sample.reshape(1, dec_batch * dec_seq, D_MODEL)
    re_p, im_p, re_s, im_s, v_s = [], [], [], [], []
    for i, wl in enumerate(layers):
        final = i == DEPTH - 1
        wl_p = dict(wl, mixw=wl['mixw_p'], mixb=wl['mixb_p'])
        h_p, sre, sim, _ = _mixer(h_p, zeros, zeros, wl_p, nseq=batch, t_len=CHUNK,
                                  pitch=CHUNK + SCAN_PAD_ROWS, keep_v=False)
        re_p.append(sre.reshape(batch, SSM_GROUPS, SSM_STATE))
        im_p.append(sim.reshape(batch, SSM_GROUPS, SSM_STATE))
        h_p = _ffn(h_p.reshape(batch * seq, D_MODEL), p_prompt[i].reshape(batch * seq, PLE_DIM),
                   wl, g_fin, final=final).reshape(batch, seq, D_MODEL)

        wl_s = dict(wl, mixw=wl['mixw_s'], mixb=wl['mixb_s'])
        h_s, sre, sim, vn = _mixer(h_s, state_ssm_re[i].reshape(dec_batch, SSM_FLAT),
                                   state_ssm_im[i].reshape(dec_batch, SSM_FLAT), wl_s,
                                   nseq=dec_batch, t_len=dec_seq, pitch=dec_seq, keep_v=True)
        re_s.append(sre.reshape(dec_batch, SSM_GROUPS, SSM_STATE))
        im_s.append(sim.reshape(dec_batch, SSM_GROUPS, SSM_STATE))
        v_s.append(vn.reshape(dec_batch, dec_seq, GMLP_WIDTH))
        h_s = _ffn(h_s.reshape(dec_batch * dec_seq, D_MODEL),
                   p_sample[i].reshape(dec_batch * dec_seq, PLE_DIM),
                   wl, g_fin, final=final).reshape(1, dec_batch * dec_seq, D_MODEL)

    y_sample = h_s.reshape(dec_batch, dec_seq, D_MODEL)
    return (h_p, y_sample, jnp.stack(re_p), jnp.stack(im_p), jnp.stack(re_s), jnp.stack(im_s), jnp.stack(v_s))
```

```python
import functools
import math

import jax
import jax.numpy as jnp
from jax import lax
from jax.experimental import pallas as pl
from jax.experimental.pallas import tpu as pltpu

D_MODEL = 1024
DEPTH = 2
SSM_WIDTH = 512
GMLP_WIDTH = 512
SSM_GROUP = 16
SSM_GROUPS = 32
SSM_STATE = 64
SSM_FLAT = SSM_GROUPS * SSM_STATE
CHUNK = 128
GMLP_HEADS = 4
GMLP_HEAD_DIM = 128
PLE_DIM = 256
D_FF = 4096
IN_WIDTH = SSM_WIDTH + 2 * GMLP_WIDTH
EPS = 1e-6
LAM_RE_MAX = -1e-4

LANES = 128
SUBLANES = 8
S5_SLICES = SSM_WIDTH // LANES
S5_SLICE_STATES = SSM_FLAT // S5_SLICES
S5_SLABS = S5_SLICE_STATES // LANES
SCAN_PAD_ROWS = 4
SCAN_UNROLL = 2
ROW_TILE = 256
FFN_ROWS = 512
FFN_COLS = 1024
MIXER_VMEM_BYTES = 56 * 1024 * 1024
FFN_VMEM_BYTES = 52 * 1024 * 1024

_F32 = jnp.float32
_BF16 = jnp.bfloat16


def _dot(a, b):
    return jnp.dot(a, b, preferred_element_type=_F32)


def _rmsnorm(x, g):
    r = lax.rsqrt(jnp.mean(x * x, axis=-1, keepdims=True) + EPS)
    return x * r * g


def _gelu(x):
    c = math.sqrt(2.0 / math.pi)
    return x * (0.5 * (1.0 + jnp.tanh(c * (x + 0.044715 * (x * x * x)))))


def _disc_kernel(lr_ref, li_ref, ldt_ref, br_ref, bi_ref, abr_ref, abi_ref, bbr_ref, bbi_ref):
    lr = jnp.minimum(lr_ref[...], LAM_RE_MAX)
    li = li_ref[...]
    dt = jnp.exp(ldt_ref[...])
    mag = jnp.exp(lr * dt)
    abr = mag * jnp.cos(li * dt)
    abi = mag * jnp.sin(li * dt)
    den = lr * lr + li * li
    nr = abr - 1.0
    ni = abi
    qr = (nr * lr + ni * li) / den
    qi = (ni * lr - nr * li) / den
    abr_ref[...] = abr
    abi_ref[...] = abi
    for h in range(SSM_GROUP):
        br = br_ref[h]
        bi = bi_ref[h]
        bbr_ref[h] = qr * br - qi * bi
        bbi_ref[h] = qr * bi + qi * br


def _discretise(lam_re, lam_im, log_dt, b_re, b_im):
    dg = DEPTH * SSM_GROUPS
    lr = lam_re.reshape(dg, SSM_STATE)
    li = lam_im.reshape(dg, SSM_STATE)
    ldt = log_dt.reshape(dg, 1)
    br = jnp.moveaxis(b_re, -1, 0).reshape(SSM_GROUP, dg, SSM_STATE)
    bi = jnp.moveaxis(b_im, -1, 0).reshape(SSM_GROUP, dg, SSM_STATE)
    vec = jax.ShapeDtypeStruct((dg, SSM_STATE), _F32)
    mat = jax.ShapeDtypeStruct((SSM_GROUP, dg, SSM_STATE), _F32)
    abr, abi, bbr, bbi = pl.pallas_call(
        _disc_kernel, out_shape=(vec, vec, mat, mat), name="s5_discretise")(lr, li, ldt, br, bi)
    shp = (SSM_GROUP, DEPTH, SSM_GROUPS, SSM_STATE)
    return (abr.reshape(DEPTH, 1, SSM_FLAT), abi.reshape(DEPTH, 1, SSM_FLAT),
            jnp.moveaxis(bbr.reshape(shp), 0, 2), jnp.moveaxis(bbi.reshape(shp), 0, 2))


def _block_diag(m):
    *lead, ng, a, b = m.shape
    eye = jnp.eye(ng, dtype=m.dtype)
    return jnp.einsum('...gab,gk->...gakb', m, eye).reshape(*lead, ng * a, ng * b)


def _s5_matrices(bbr, bbi, c_re, c_im):
    gps = SSM_GROUPS // S5_SLICES
    shp = (DEPTH, S5_SLICES, gps, SSM_GROUP, SSM_STATE)
    bblk = jnp.concatenate([_block_diag(bbr.reshape(shp)), _block_diag(bbi.reshape(shp))], axis=-1)
    cr = jnp.swapaxes(c_re.reshape(shp), -1, -2)
    ci = jnp.swapaxes(c_im.reshape(shp), -1, -2)
    cblk = jnp.concatenate([_block_diag(cr), -_block_diag(ci)], axis=-2)
    return bblk.astype(_BF16), cblk.astype(_BF16)


def _tile_rows(ref, i, rows):
    nb, tb, c = ref.shape
    if nb == 1:
        return ref[0, i * rows:(i + 1) * rows, :]
    per = rows // tb
    return ref[i * per:(i + 1) * per].reshape(rows, c)


def _store_tile_rows(ref, i, rows, val):
    nb, tb, c = ref.shape
    if nb == 1:
        ref[0, i * rows:(i + 1) * rows, :] = val
    else:
        per = rows // tb
        ref[i * per:(i + 1) * per] = val.reshape(per, tb, c)


def _load_pitched(ref, i, rows, t_len, pitch):
    if pitch == t_len:
        return ref[i * rows:(i + 1) * rows, :]
    per = rows // t_len
    return jnp.concatenate([ref[n * pitch:n * pitch + t_len, :] for n in range(i * per, (i + 1) * per)], axis=0)


def _store_pitched(ref, i, rows, t_len, pitch, val):
    if pitch == t_len:
        ref[i * rows:(i + 1) * rows, :] = val
        return
    per = rows // t_len
    for k in range(per):
        n = i * per + k
        ref[n * pitch:n * pitch + t_len, :] = val[k * t_len:(k + 1) * t_len, :]


MIXER_INPUTS = 19


def _mixer_kernel(*refs, nseq, t_len, pitch, keep_v):
    (x_ref, h0r_ref, h0i_ref, g_mix_ref, w_in_ref, abr_ref, abi_ref, bblk_ref, cblk_ref,
     dskip_ref, w_glu_ref, b_glu_ref, g_v_ref, b_v_ref, mixw_ref, mixb_ref,
     g_os_ref, g_og_ref, w_out_ref) = refs[:MIXER_INPUTS]
    out_ref, sre_ref, sim_ref = refs[MIXER_INPUTS:MIXER_INPUTS + 3]
    if keep_v:
        vn_ref, us_scr, ys_scr, yg_scr, bu_scr = refs[MIXER_INPUTS + 3:]
    else:
        us_scr, ys_scr, yg_scr, bu_scr = refs[MIXER_INPUTS + 3:]
    rows = nseq * t_len
    n_tiles = rows // ROW_TILE

    @pl.when(pl.program_id(0) == 0)
    def _():
        sre_ref[...] = h0r_ref[...]
        sim_ref[...] = h0i_ref[...]
        us_scr[...] = jnp.zeros_like(us_scr)

    row = lax.broadcasted_iota(jnp.int32, (CHUNK, CHUNK), 0)
    col = lax.broadcasted_iota(jnp.int32, (CHUNK, CHUNK), 1)
    t_shift = t_len.bit_length() - 1
    causal = ((row >> t_shift) == (col >> t_shift)) & ((col & (t_len - 1)) <= (row & (t_len - 1)))
    mixw = [jnp.where(causal, mixw_ref[h], 0.0).astype(_BF16) for h in range(GMLP_HEADS)]

    for i in range(n_tiles):
        x = _tile_rows(x_ref, i, ROW_TILE)
        a = _rmsnorm(x, g_mix_ref[...]).astype(_BF16)
        z = _dot(a, w_in_ref[...])
        _store_pitched(us_scr, i, ROW_TILE, t_len, pitch, z[:, :SSM_WIDTH])
        zg = _gelu(z[:, SSM_WIDTH:])
        ug = zg[:, :GMLP_WIDTH]
        v = zg[:, GMLP_WIDTH:]
        mu = jnp.mean(v, axis=-1, keepdims=True)
        vc = v - mu
        r = lax.rsqrt(jnp.mean(vc * vc, axis=-1, keepdims=True) + EPS)
        vn = vc * r * g_v_ref[...] + b_v_ref[...]
        if keep_v:
            vn_ref[i * ROW_TILE:(i + 1) * ROW_TILE, :] = vn
        vn = vn.astype(_BF16)
        for b in range(ROW_TILE // CHUNK):
            rb = slice(b * CHUNK, (b + 1) * CHUNK)
            mixed = [_dot(mixw[h], vn[rb, h * GMLP_HEAD_DIM:(h + 1) * GMLP_HEAD_DIM])
                     for h in range(GMLP_HEADS)]
            m = jnp.concatenate(mixed, axis=-1) + mixb_ref[...]
            yg = _rmsnorm(ug[rb, :] * m, g_og_ref[...]).astype(_BF16)
            yg_scr[i * ROW_TILE + b * CHUNK:i * ROW_TILE + (b + 1) * CHUNK, :] = yg

    for s in range(S5_SLICES):
        c0 = s * LANES
        bu = _dot(us_scr[:, c0:c0 + LANES].astype(_BF16), bblk_ref[s])
        for c in range(2 * S5_SLABS):
            bu_scr[c] = bu[:, c * LANES:(c + 1) * LANES]

        s0 = s * S5_SLICE_STATES
        a_re = [jnp.broadcast_to(abr_ref[:, s0 + c * LANES:s0 + (c + 1) * LANES], (nseq, LANES))
                for c in range(S5_SLABS)]
        a_im = [jnp.broadcast_to(abi_ref[:, s0 + c * LANES:s0 + (c + 1) * LANES], (nseq, LANES))
                for c in range(S5_SLABS)]
        h_init = tuple(
            (sre_ref[:, s0 + c * LANES:s0 + (c + 1) * LANES], sim_ref[:, s0 + c * LANES:s0 + (c + 1) * LANES])
            for c in range(S5_SLABS))

        def scan_step(t, h):
            new = []
            for c in range(S5_SLABS):
                rows_t = pl.ds(t, nseq, stride=pitch)
                hr, hi = h[c]
                nhr = a_re[c] * hr - a_im[c] * hi + bu_scr[c, rows_t, :]
                nhi = a_re[c] * hi + a_im[c] * hr + bu_scr[S5_SLABS + c, rows_t, :]
                bu_scr[c, rows_t, :] = nhr
                bu_scr[S5_SLABS + c, rows_t, :] = nhi
                new.append((nhr, nhi))
            return tuple(new)
        h_fin = lax.fori_loop(0, t_len, scan_step, h_init, unroll=SCAN_UNROLL)
        for c in range(S5_SLABS):
            sre_ref[:, s0 + c * LANES:s0 + (c + 1) * LANES] = h_fin[c][0]
            sim_ref[:, s0 + c * LANES:s0 + (c + 1) * LANES] = h_fin[c][1]

        hs = jnp.concatenate([bu_scr[c] for c in range(2 * S5_SLABS)], axis=-1).astype(_BF16)
        ys_scr[:, c0:c0 + LANES] = _dot(hs, cblk_ref[s])

    for i in range(n_tiles):
        ys = (_load_pitched(ys_scr, i, ROW_TILE, t_len, pitch)
              + dskip_ref[...] * _load_pitched(us_scr, i, ROW_TILE, t_len, pitch))
        ys = _gelu(ys)
        ys = ys * jax.nn.sigmoid(_dot(ys.astype(_BF16), w_glu_ref[...]) + b_glu_ref[...])
        ys = _rmsnorm(ys, g_os_ref[...]).astype(_BF16)
        y = (_dot(ys, w_out_ref[:SSM_WIDTH, :])
             + _dot(yg_scr[i * ROW_TILE:(i + 1) * ROW_TILE, :], w_out_ref[SSM_WIDTH:, :]))
        _store_tile_rows(out_ref, i, ROW_TILE, _tile_rows(x_ref, i, ROW_TILE) + y)


def _layer_spec(arr, layer):
    nd = arr.ndim - 1
    return pl.BlockSpec((None,) + arr.shape[1:], lambda j: (layer,) + (0,) * nd,
                        pipeline_mode=pl.Buffered(1))


MIXER_WEIGHTS = ('g_mix', 'w_in', 'abr', 'abi', 'bblk', 'cblk', 'd_skip', 'w_glu', 'b_glu', 'g_v', 'b_v',
                 'mixw', 'mixb', 'g_out_s', 'g_out_g', 'w_out')
FFN_WEIGHTS = ('g_ffn', 'w_up', 'w_down', 'g_ple', 'w_ple_gate', 'w_ple')


def _mixer(x, h0r, h0i, w, layer, *, h0_layer, nseq, t_len, pitch, keep_v):
    nb, length, _ = x.shape
    t_blk = t_len if nb == nseq else nseq * t_len
    rows = nseq * t_len
    steps = length // t_blk
    n_buf_rows = nseq * pitch
    x_spec = pl.BlockSpec((nb, t_blk, D_MODEL), lambda j: (0, j, 0))
    st_spec = pl.BlockSpec((nseq, SSM_FLAT), lambda j: (0, 0))
    consts = [w[k + ('_s' if keep_v else '_p')] if k in ('mixw', 'mixb') else w[k] for k in MIXER_WEIGHTS]
    assert len(consts) + 3 == MIXER_INPUTS
    out_specs = [x_spec, st_spec, st_spec]
    out_shape = [jax.ShapeDtypeStruct(x.shape, _F32),
                 jax.ShapeDtypeStruct((nseq, SSM_FLAT), _F32),
                 jax.ShapeDtypeStruct((nseq, SSM_FLAT), _F32)]
    scratch = [pltpu.VMEM((n_buf_rows, SSM_WIDTH), _F32),
               pltpu.VMEM((n_buf_rows, SSM_WIDTH), _F32),
               pltpu.VMEM((rows, GMLP_WIDTH), _BF16),
               pltpu.VMEM((2 * S5_SLABS, n_buf_rows, LANES), _F32)]
    if keep_v:
        out_specs.append(pl.BlockSpec((rows, GMLP_WIDTH), lambda j: (j, 0)))
        out_shape.append(jax.ShapeDtypeStruct((steps * rows, GMLP_WIDTH), _F32))
    res = pl.pallas_call(
        functools.partial(_mixer_kernel, nseq=nseq, t_len=t_len, pitch=pitch, keep_v=keep_v),
        grid=(steps,),
        in_specs=([x_spec, _layer_spec(h0r, h0_layer), _layer_spec(h0i, h0_layer)]
                  + [_layer_spec(c, layer) for c in consts]),
        out_specs=tuple(out_specs),
        out_shape=tuple(out_shape),
        scratch_shapes=scratch,
        compiler_params=pltpu.CompilerParams(dimension_semantics=("arbitrary",),
                                             vmem_limit_bytes=MIXER_VMEM_BYTES),
        name="mixer_sample" if keep_v else "mixer_prompt",
    )(x, h0r, h0i, *consts)
    return res if keep_v else (*res, None)


def _ffn_kernel(h_ref, p_ref, g_ffn_ref, w_up_ref, w_down_ref, g_ple_ref, w_gate_ref, w_ple_ref, g_fin_ref,
                out_ref, *, final):
    h = h_ref[...]
    f = _rmsnorm(h, g_ffn_ref[...]).astype(_BF16)
    acc = h
    for c in range(D_FF // FFN_COLS):
        up = _dot(f, w_up_ref[:, c * FFN_COLS:(c + 1) * FFN_COLS])
        act = jnp.square(jnp.maximum(up, 0.0)).astype(_BF16)
        acc = acc + _dot(act, w_down_ref[c * FFN_COLS:(c + 1) * FFN_COLS, :])
    gate = jax.nn.sigmoid(_dot(_rmsnorm(acc, g_ple_ref[...]).astype(_BF16), w_gate_ref[...]))
    out = acc + gate * _dot(p_ref[...].astype(_BF16), w_ple_ref[...])
    if final:
        out = _rmsnorm(out, g_fin_ref[...])
    out_ref[...] = out


def _ffn(h, p, w, g_final, layer, *, final):
    n_rows = h.shape[0]
    consts = [w[k] for k in FFN_WEIGHTS]
    return pl.pallas_call(
        functools.partial(_ffn_kernel, final=final),
        grid=(n_rows // FFN_ROWS,),
        in_specs=([pl.BlockSpec((FFN_ROWS, D_MODEL), lambda j: (j, 0)),
                   pl.BlockSpec((None, FFN_ROWS, PLE_DIM), lambda j: (layer, j, 0))]
                  + [_layer_spec(c, layer) for c in consts] + [_layer_spec(g_final, 0)]),
        out_specs=pl.BlockSpec((FFN_ROWS, D_MODEL), lambda j: (j, 0)),
        out_shape=jax.ShapeDtypeStruct((n_rows, D_MODEL), _F32),
        compiler_params=pltpu.CompilerParams(dimension_semantics=("arbitrary",),
                                             vmem_limit_bytes=FFN_VMEM_BYTES),
        name="ffn",
    )(h, p, *consts, g_final)


def kernel(x_prompt, x_sample, state_ssm_re, state_ssm_im, p_prompt, p_sample, g_mix, w_in, lam_re, lam_im, log_dt, b_re, b_im, c_re, c_im, d_skip, w_glu, b_glu, g_v, b_v, w_s, b_s, g_out_s, g_out_g, w_out, g_ffn, w_up, w_down, g_ple, w_ple_gate, w_ple, g_final):
    batch, seq, _ = x_prompt.shape
    dec_batch, dec_seq, _ = x_sample.shape
    n_p, n_s = batch * seq, dec_batch * dec_seq
    assert seq % CHUNK == 0 and batch == SUBLANES
    assert CHUNK % dec_seq == 0 and n_s % ROW_TILE == 0

    abr, abi, bbr, bbi = _discretise(lam_re, lam_im, log_dt, b_re, b_im)
    bblk, cblk = _s5_matrices(bbr, bbi, c_re, c_im)
    rowv = lambda v: v.reshape(DEPTH, 1, -1)
    bias = jnp.repeat(jnp.swapaxes(b_s, -1, -2), GMLP_HEAD_DIM, axis=-1)
    reps = CHUNK // dec_seq
    w = dict(
        g_mix=rowv(g_mix), w_in=w_in.astype(_BF16), abr=abr, abi=abi, bblk=bblk, cblk=cblk,
        d_skip=rowv(d_skip), w_glu=w_glu.astype(_BF16), b_glu=rowv(b_glu), g_v=rowv(g_v), b_v=rowv(b_v),
        g_out_s=rowv(g_out_s), g_out_g=rowv(g_out_g), w_out=w_out.astype(_BF16),
        g_ffn=rowv(g_ffn), w_up=w_up.astype(_BF16), w_down=w_down.astype(_BF16), g_ple=rowv(g_ple),
        w_ple_gate=w_ple_gate.astype(_BF16), w_ple=w_ple.astype(_BF16),
        mixw_p=w_s, mixb_p=bias,
        mixw_s=jnp.tile(w_s[:, :, :dec_seq, :dec_seq], (1, 1, reps, reps)),
        mixb_s=jnp.tile(bias[:, :dec_seq], (1, reps, 1)))
    g_fin = g_final.reshape(1, 1, D_MODEL)

    zeros = jnp.zeros((1, batch, SSM_FLAT), _F32)
    s0_re = state_ssm_re.reshape(DEPTH, dec_batch, SSM_FLAT)
    s0_im = state_ssm_im.reshape(DEPTH, dec_batch, SSM_FLAT)
    pp = p_prompt.reshape(DEPTH, n_p, PLE_DIM)
    ps = p_sample.reshape(DEPTH, n_s, PLE_DIM)
    h_p = x_prompt
    h_s = x_sample.reshape(1, n_s, D_MODEL)
    re_p, im_p, re_s, im_s, v_s = [], [], [], [], []
    for i in range(DEPTH):
        final = i == DEPTH - 1
        h_p, sre, sim, _ = _mixer(h_p, zeros, zeros, w, i, h0_layer=0, nseq=batch, t_len=CHUNK,
                                  pitch=CHUNK + SCAN_PAD_ROWS, keep_v=False)
        re_p.append(sre)
        im_p.append(sim)
        h_p = _ffn(h_p.reshape(n_p, D_MODEL), pp, w, g_fin, i, final=final).reshape(batch, seq, D_MODEL)

        h_s, sre, sim, vn = _mixer(h_s, s0_re, s0_im, w, i, h0_layer=i, nseq=dec_batch, t_len=dec_seq,
                                   pitch=dec_seq, keep_v=True)
        re_s.append(sre)
        im_s.append(sim)
        v_s.append(vn)
        h_s = _ffn(h_s.reshape(n_s, D_MODEL), ps, w, g_fin, i, final=final).reshape(1, n_s, D_MODEL)

    st_p = (DEPTH, batch, SSM_GROUPS, SSM_STATE)
    st_s = (DEPTH, dec_batch, SSM_GROUPS, SSM_STATE)
    return (h_p, h_s.reshape(dec_batch, dec_seq, D_MODEL),
            jnp.stack(re_p).reshape(st_p), jnp.stack(im_p).reshape(st_p),
            jnp.stack(re_s).reshape(st_s), jnp.stack(im_s).reshape(st_s),
            jnp.stack(v_s).reshape(DEPTH, dec_batch, dec_seq, GMLP_WIDTH))
```

```python
import functools
import math

import jax
import jax.numpy as jnp
from jax import lax
from jax.experimental import pallas as pl
from jax.experimental.pallas import tpu as pltpu

D_MODEL = 1024
DEPTH = 2
SSM_WIDTH = 512
GMLP_WIDTH = 512
SSM_GROUP = 16
SSM_GROUPS = 32
SSM_STATE = 64
SSM_FLAT = SSM_GROUPS * SSM_STATE
CHUNK = 128
GMLP_HEADS = 4
GMLP_HEAD_DIM = 128
PLE_DIM = 256
D_FF = 4096
IN_WIDTH = SSM_WIDTH + 2 * GMLP_WIDTH
EPS = 1e-6
LAM_RE_MAX = -1e-4

LANES = 128
SUBLANES = 8
S5_SLICES = SSM_WIDTH // LANES
S5_SLICE_STATES = SSM_FLAT // S5_SLICES
S5_SLABS = S5_SLICE_STATES // LANES
S5_BLOCK = 4
SCAN_PAD_BLOCKS = 4
SCAN_UNROLL = 2
ROW_TILE = 256
FFN_ROWS = 512
FFN_COLS = 1024
MIXER_VMEM_BYTES = 56 * 1024 * 1024
FFN_VMEM_BYTES = 52 * 1024 * 1024

_F32 = jnp.float32
_BF16 = jnp.bfloat16
_EXACT = lax.Precision.HIGHEST


def _dot(a, b):
    return jnp.dot(a, b, preferred_element_type=_F32)


def _rmsnorm(x, g):
    r = lax.rsqrt(jnp.mean(x * x, axis=-1, keepdims=True) + EPS)
    return x * r * g


def _gelu(x):
    c = math.sqrt(2.0 / math.pi)
    return x * (0.5 * (1.0 + jnp.tanh(c * (x + 0.044715 * (x * x * x)))))


def _disc_kernel(lr_ref, li_ref, ldt_ref, br_ref, bi_ref, abr_ref, abi_ref, bbr_ref, bbi_ref):
    lr = jnp.minimum(lr_ref[...], LAM_RE_MAX)
    li = li_ref[...]
    dt = jnp.exp(ldt_ref[...])
    mag = jnp.exp(lr * dt)
    abr = mag * jnp.cos(li * dt)
    abi = mag * jnp.sin(li * dt)
    den = lr * lr + li * li
    nr = abr - 1.0
    ni = abi
    qr = (nr * lr + ni * li) / den
    qi = (ni * lr - nr * li) / den
    abr_ref[...] = abr
    abi_ref[...] = abi
    for h in range(SSM_GROUP):
        br = br_ref[h]
        bi = bi_ref[h]
        bbr_ref[h] = qr * br - qi * bi
        bbi_ref[h] = qr * bi + qi * br


def _discretise(lam_re, lam_im, log_dt, b_re, b_im):
    dg = DEPTH * SSM_GROUPS
    lr = lam_re.reshape(dg, SSM_STATE)
    li = lam_im.reshape(dg, SSM_STATE)
    ldt = log_dt.reshape(dg, 1)
    br = jnp.moveaxis(b_re, -1, 0).reshape(SSM_GROUP, dg, SSM_STATE)
    bi = jnp.moveaxis(b_im, -1, 0).reshape(SSM_GROUP, dg, SSM_STATE)
    vec = jax.ShapeDtypeStruct((dg, SSM_STATE), _F32)
    mat = jax.ShapeDtypeStruct((SSM_GROUP, dg, SSM_STATE), _F32)
    abr, abi, bbr, bbi = pl.pallas_call(
        _disc_kernel, out_shape=(vec, vec, mat, mat), name="s5_discretise")(lr, li, ldt, br, bi)
    shp = (SSM_GROUP, DEPTH, SSM_GROUPS, SSM_STATE)
    vshp = (DEPTH, SSM_GROUPS, SSM_STATE)
    return (abr.reshape(vshp), abi.reshape(vshp),
            jnp.moveaxis(bbr.reshape(shp), 0, 2), jnp.moveaxis(bbi.reshape(shp), 0, 2))


def _block_diag(m):
    *lead, ng, a, b = m.shape
    eye = jnp.eye(ng, dtype=m.dtype)
    return jnp.einsum('...gab,gk->...gakb', m, eye).reshape(*lead, ng * a, ng * b)


def _cmul(a, b):
    return a[0] * b[0] - a[1] * b[1], a[0] * b[1] + a[1] * b[0]


def _s5_matrices(abr, abi, bbr, bbi, c_re, c_im):
    r = S5_BLOCK
    gps = SSM_GROUPS // S5_SLICES
    a1 = (abr[:, :, None, :], abi[:, :, None, :])
    apow = [(jnp.ones_like(a1[0]), jnp.zeros_like(a1[1]))]
    for _ in range(r):
        apow.append(_cmul(apow[-1], a1))
    ab = [_cmul(apow[k], (bbr, bbi)) for k in range(r)]
    ca = [_cmul((c_re, c_im), apow[k]) for k in range(r + 1)]
    cab = [jnp.einsum('dghp,dgop->dgho', ab[k][0], c_re, precision=_EXACT)
           - jnp.einsum('dghp,dgop->dgho', ab[k][1], c_im, precision=_EXACT) for k in range(r)]

    def bd(m):
        return _block_diag(m.reshape(DEPTH, S5_SLICES, gps, *m.shape[2:]))

    def tr(m):
        return jnp.swapaxes(m, -1, -2)

    w_v = jnp.concatenate(
        [jnp.concatenate([bd(ab[r - 1 - m][0]), bd(ab[r - 1 - m][1])], axis=-1) for m in range(r)], axis=-2)
    y_re = jnp.concatenate([bd(tr(ca[i + 1][0])) for i in range(r)], axis=-1)
    y_im = jnp.concatenate([-bd(tr(ca[i + 1][1])) for i in range(r)], axis=-1)
    zero = jnp.zeros((DEPTH, S5_SLICES, LANES, LANES), _F32)
    y_u = jnp.concatenate(
        [jnp.concatenate([bd(cab[i - m]) if m <= i else zero for i in range(r)], axis=-1) for m in range(r)],
        axis=-2)
    w_y = jnp.concatenate([y_re, y_im, y_u], axis=-2)
    a_blk = (apow[r][0].reshape(DEPTH, 1, SSM_FLAT), apow[r][1].reshape(DEPTH, 1, SSM_FLAT))
    return w_v.astype(_BF16), w_y.astype(_BF16), a_blk


def _tile_rows(ref, i, rows):
    nb, tb, c = ref.shape
    if nb == 1:
        return ref[0, i * rows:(i + 1) * rows, :]
    per = rows // tb
    return ref[i * per:(i + 1) * per].reshape(rows, c)


def _store_tile_rows(ref, i, rows, val):
    nb, tb, c = ref.shape
    if nb == 1:
        ref[0, i * rows:(i + 1) * rows, :] = val
    else:
        per = rows // tb
        ref[i * per:(i + 1) * per] = val.reshape(per, tb, c)


def _load_slabs(ref, i, rows, t_len, seq_rows):
    if seq_rows == t_len:
        return jnp.concatenate([ref[s, i * rows:(i + 1) * rows, :] for s in range(S5_SLICES)], axis=-1)
    per = rows // t_len
    return jnp.concatenate(
        [jnp.concatenate([ref[s, n * seq_rows:n * seq_rows + t_len, :] for s in range(S5_SLICES)], axis=-1)
         for n in range(i * per, (i + 1) * per)], axis=0)


def _store_slabs(ref, i, rows, t_len, seq_rows, val):
    for s in range(S5_SLICES):
        cols = slice(s * LANES, (s + 1) * LANES)
        if seq_rows == t_len:
            ref[s, i * rows:(i + 1) * rows, :] = val[:, cols]
        else:
            per = rows // t_len
            for k in range(per):
                n = i * per + k
                ref[s, n * seq_rows:n * seq_rows + t_len, :] = val[k * t_len:(k + 1) * t_len, cols]


MIXER_INPUTS = 19


def _mixer_kernel(*refs, nseq, t_len, blk_pitch, keep_v):
    (x_ref, h0r_ref, h0i_ref, g_mix_ref, w_in_ref, abr_ref, abi_ref, wv_ref, wy_ref,
     dskip_ref, w_glu_ref, b_glu_ref, g_v_ref, b_v_ref, mixw_ref, mixb_ref,
     g_os_ref, g_og_ref, w_out_ref) = refs[:MIXER_INPUTS]
    out_ref, sre_ref, sim_ref = refs[MIXER_INPUTS:MIXER_INPUTS + 3]
    if keep_v:
        vn_ref, us_scr, ys_scr, yg_scr, hb_scr = refs[MIXER_INPUTS + 3:]
    else:
        us_scr, ys_scr, yg_scr, hb_scr = refs[MIXER_INPUTS + 3:]
    rows = nseq * t_len
    n_tiles = rows // ROW_TILE
    n_blk = t_len // S5_BLOCK
    seq_rows = S5_BLOCK * blk_pitch
    n_buf = nseq * blk_pitch

    @pl.when(pl.program_id(0) == 0)
    def _():
        sre_ref[...] = h0r_ref[...]
        sim_ref[...] = h0i_ref[...]
        us_scr[...] = jnp.zeros_like(us_scr)

    row = lax.broadcasted_iota(jnp.int32, (CHUNK, CHUNK), 0)
    col = lax.broadcasted_iota(jnp.int32, (CHUNK, CHUNK), 1)
    t_shift = t_len.bit_length() - 1
    causal = ((row >> t_shift) == (col >> t_shift)) & ((col & (t_len - 1)) <= (row & (t_len - 1)))
    mixw = [jnp.where(causal, mixw_ref[h], 0.0).astype(_BF16) for h in range(GMLP_HEADS)]

    def in_proj(i):
        x = _tile_rows(x_ref, i, ROW_TILE)
        a = _rmsnorm(x, g_mix_ref[...]).astype(_BF16)
        return _dot(a, w_in_ref[...])

    z_next = in_proj(0)
    for i in range(n_tiles):
        z = z_next
        if i + 1 < n_tiles:
            z_next = in_proj(i + 1)
        _store_slabs(us_scr, i, ROW_TILE, t_len, seq_rows, z[:, :SSM_WIDTH])
        zg = _gelu(z[:, SSM_WIDTH:])
        ug = zg[:, :GMLP_WIDTH]
        v = zg[:, GMLP_WIDTH:]
        mu = jnp.mean(v, axis=-1, keepdims=True)
        vc = v - mu
        r = lax.rsqrt(jnp.mean(vc * vc, axis=-1, keepdims=True) + EPS)
        vn = vc * r * g_v_ref[...] + b_v_ref[...]
        if keep_v:
            vn_ref[i * ROW_TILE:(i + 1) * ROW_TILE, :] = vn
        vn = vn.astype(_BF16)
        for b in range(ROW_TILE // CHUNK):
            rb = slice(b * CHUNK, (b + 1) * CHUNK)
            mixed = [_dot(mixw[h], vn[rb, h * GMLP_HEAD_DIM:(h + 1) * GMLP_HEAD_DIM])
                     for h in range(GMLP_HEADS)]
            m = jnp.concatenate(mixed, axis=-1) + mixb_ref[...]
            yg = _rmsnorm(ug[rb, :] * m, g_og_ref[...]).astype(_BF16)
            yg_scr[i * ROW_TILE + b * CHUNK:i * ROW_TILE + (b + 1) * CHUNK, :] = yg

    for s in range(S5_SLICES):
        u_blk = jnp.concatenate([us_scr[s, pl.ds(m, n_buf, stride=S5_BLOCK), :] for m in range(S5_BLOCK)],
                                axis=-1).astype(_BF16)
        v_in = _dot(u_blk, wv_ref[s])
        for c in range(2 * S5_SLABS):
            hb_scr[c] = v_in[:, c * LANES:(c + 1) * LANES]

        s0 = s * S5_SLICE_STATES
        a_re = [jnp.broadcast_to(abr_ref[:, s0 + c * LANES:s0 + (c + 1) * LANES], (nseq, LANES))
                for c in range(S5_SLABS)]
        a_im = [jnp.broadcast_to(abi_ref[:, s0 + c * LANES:s0 + (c + 1) * LANES], (nseq, LANES))
                for c in range(S5_SLABS)]
        h_init = tuple(
            (sre_ref[:, s0 + c * LANES:s0 + (c + 1) * LANES], sim_ref[:, s0 + c * LANES:s0 + (c + 1) * LANES])
            for c in range(S5_SLABS))

        def scan_step(j, h):
            new = []
            for c in range(S5_SLABS):
                rows_j = pl.ds(j, nseq, stride=blk_pitch)
                hr, hi = h[c]
                v_re = hb_scr[c, rows_j, :]
                v_im = hb_scr[S5_SLABS + c, rows_j, :]
                hb_scr[c, rows_j, :] = hr
                hb_scr[S5_SLABS + c, rows_j, :] = hi
                new.append((a_re[c] * hr - a_im[c] * hi + v_re, a_re[c] * hi + a_im[c] * hr + v_im))
            return tuple(new)
        if n_blk == 1:
            h_fin = scan_step(0, h_init)
        else:
            h_fin = lax.fori_loop(0, n_blk, scan_step, h_init, unroll=SCAN_UNROLL)
        for c in range(S5_SLABS):
            sre_ref[:, s0 + c * LANES:s0 + (c + 1) * LANES] = h_fin[c][0]
            sim_ref[:, s0 + c * LANES:s0 + (c + 1) * LANES] = h_fin[c][1]

        lhs = jnp.concatenate([hb_scr[c].astype(_BF16) for c in range(2 * S5_SLABS)] + [u_blk], axis=-1)
        y_blk = _dot(lhs, wy_ref[s])
        for i in range(S5_BLOCK):
            ys_scr[s, pl.ds(i, n_buf, stride=S5_BLOCK), :] = y_blk[:, i * LANES:(i + 1) * LANES]

    def glu_in(i):
        ys = (_load_slabs(ys_scr, i, ROW_TILE, t_len, seq_rows)
              + dskip_ref[...] * _load_slabs(us_scr, i, ROW_TILE, t_len, seq_rows))
        ys = _gelu(ys)
        return ys, _dot(ys.astype(_BF16), w_glu_ref[...])

    nxt = glu_in(0)
    for i in range(n_tiles):
        ys, gate = nxt
        if i + 1 < n_tiles:
            nxt = glu_in(i + 1)
        ys = ys * jax.nn.sigmoid(gate + b_glu_ref[...])
        ys = _rmsnorm(ys, g_os_ref[...]).astype(_BF16)
        y = (_dot(ys, w_out_ref[:SSM_WIDTH, :])
             + _dot(yg_scr[i * ROW_TILE:(i + 1) * ROW_TILE, :], w_out_ref[SSM_WIDTH:, :]))
        _store_tile_rows(out_ref, i, ROW_TILE, _tile_rows(x_ref, i, ROW_TILE) + y)


def _layer_spec(arr, layer):
    nd = arr.ndim - 1
    return pl.BlockSpec((None,) + arr.shape[1:], lambda j: (layer,) + (0,) * nd,
                        pipeline_mode=pl.Buffered(1))


MIXER_WEIGHTS = ('g_mix', 'w_in', 'a_blk_re', 'a_blk_im', 'w_v', 'w_y', 'd_skip', 'w_glu', 'b_glu', 'g_v', 'b_v',
                 'mixw', 'mixb', 'g_out_s', 'g_out_g', 'w_out')
FFN_WEIGHTS = ('g_ffn', 'w_up', 'w_down', 'g_ple', 'w_ple_gate', 'w_ple')


def _mixer(x, h0r, h0i, w, layer, *, h0_layer, nseq, t_len, blk_pitch, keep_v):
    nb, length, _ = x.shape
    t_blk = t_len if nb == nseq else nseq * t_len
    rows = nseq * t_len
    steps = length // t_blk
    slab_rows = nseq * S5_BLOCK * blk_pitch
    x_spec = pl.BlockSpec((nb, t_blk, D_MODEL), lambda j: (0, j, 0))
    st_spec = pl.BlockSpec((nseq, SSM_FLAT), lambda j: (0, 0))
    consts = [w[k + ('_s' if keep_v else '_p')] if k in ('mixw', 'mixb') else w[k] for k in MIXER_WEIGHTS]
    assert len(consts) + 3 == MIXER_INPUTS
    out_specs = [x_spec, st_spec, st_spec]
    out_shape = [jax.ShapeDtypeStruct(x.shape, _F32),
                 jax.ShapeDtypeStruct((nseq, SSM_FLAT), _F32),
                 jax.ShapeDtypeStruct((nseq, SSM_FLAT), _F32)]
    scratch = [pltpu.VMEM((S5_SLICES, slab_rows, LANES), _F32),
               pltpu.VMEM((S5_SLICES, slab_rows, LANES), _F32),
               pltpu.VMEM((rows, GMLP_WIDTH), _BF16),
               pltpu.VMEM((2 * S5_SLABS, nseq * blk_pitch, LANES), _F32)]
    if keep_v:
        out_specs.append(pl.BlockSpec((rows, GMLP_WIDTH), lambda j: (j, 0)))
        out_shape.append(jax.ShapeDtypeStruct((steps * rows, GMLP_WIDTH), _F32))
    res = pl.pallas_call(
        functools.partial(_mixer_kernel, nseq=nseq, t_len=t_len, blk_pitch=blk_pitch, keep_v=keep_v),
        grid=(steps,),
        in_specs=([x_spec, _layer_spec(h0r, h0_layer), _layer_spec(h0i, h0_layer)]
                  + [_layer_spec(c, layer) for c in consts]),
        out_specs=tuple(out_specs),
        out_shape=tuple(out_shape),
        scratch_shapes=scratch,
        compiler_params=pltpu.CompilerParams(dimension_semantics=("arbitrary",),
                                             vmem_limit_bytes=MIXER_VMEM_BYTES),
        name="mixer_sample" if keep_v else "mixer_prompt",
    )(x, h0r, h0i, *consts)
    return res if keep_v else (*res, None)


def _ffn_kernel(h_ref, p_ref, g_ffn_ref, w_up_ref, w_down_ref, g_ple_ref, w_gate_ref, w_ple_ref, g_fin_ref,
                out_ref, *, final):
    h = h_ref[...]
    f = _rmsnorm(h, g_ffn_ref[...]).astype(_BF16)
    acc = h
    for c in range(D_FF // FFN_COLS):
        up = _dot(f, w_up_ref[:, c * FFN_COLS:(c + 1) * FFN_COLS])
        act = jnp.square(jnp.maximum(up, 0.0)).astype(_BF16)
        acc = acc + _dot(act, w_down_ref[c * FFN_COLS:(c + 1) * FFN_COLS, :])
    gate = jax.nn.sigmoid(_dot(_rmsnorm(acc, g_ple_ref[...]).astype(_BF16), w_gate_ref[...]))
    out = acc + gate * _dot(p_ref[...].astype(_BF16), w_ple_ref[...])
    if final:
        out = _rmsnorm(out, g_fin_ref[...])
    out_ref[...] = out


def _ffn(h, p, w, g_final, layer, *, final):
    n_rows = h.shape[0]
    consts = [w[k] for k in FFN_WEIGHTS]
    return pl.pallas_call(
        functools.partial(_ffn_kernel, final=final),
        grid=(n_rows // FFN_ROWS,),
        in_specs=([pl.BlockSpec((FFN_ROWS, D_MODEL), lambda j: (j, 0)),
                   pl.BlockSpec((None, FFN_ROWS, PLE_DIM), lambda j: (layer, j, 0))]
                  + [_layer_spec(c, layer) for c in consts] + [_layer_spec(g_final, 0)]),
        out_specs=pl.BlockSpec((FFN_ROWS, D_MODEL), lambda j: (j, 0)),
        out_shape=jax.ShapeDtypeStruct((n_rows, D_MODEL), _F32),
        compiler_params=pltpu.CompilerParams(dimension_semantics=("arbitrary",),
                                             vmem_limit_bytes=FFN_VMEM_BYTES),
        name="ffn",
    )(h, p, *consts, g_final)


def kernel(x_prompt, x_sample, state_ssm_re, state_ssm_im, p_prompt, p_sample, g_mix, w_in, lam_re, lam_im, log_dt, b_re, b_im, c_re, c_im, d_skip, w_glu, b_glu, g_v, b_v, w_s, b_s, g_out_s, g_out_g, w_out, g_ffn, w_up, w_down, g_ple, w_ple_gate, w_ple, g_final):
    batch, seq, _ = x_prompt.shape
    dec_batch, dec_seq, _ = x_sample.shape
    n_p, n_s = batch * seq, dec_batch * dec_seq
    assert seq % CHUNK == 0 and batch == SUBLANES
    assert dec_seq == S5_BLOCK and CHUNK % dec_seq == 0 and n_s % ROW_TILE == 0

    abr, abi, bbr, bbi = _discretise(lam_re, lam_im, log_dt, b_re, b_im)
    w_v, w_y, a_blk = _s5_matrices(abr, abi, bbr, bbi, c_re, c_im)
    rowv = lambda v: v.reshape(DEPTH, 1, -1)
    head_of = (jnp.arange(GMLP_WIDTH)[None, :] // GMLP_HEAD_DIM == jnp.arange(GMLP_HEADS)[:, None]).astype(_F32)
    pos_of = (jnp.arange(CHUNK)[:, None] % dec_seq == jnp.arange(dec_seq)[None, :]).astype(_F32)
    w_s4 = w_s[:, :, :dec_seq, :dec_seq]
    w = dict(
        g_mix=rowv(g_mix), w_in=w_in.astype(_BF16), a_blk_re=a_blk[0], a_blk_im=a_blk[1], w_v=w_v, w_y=w_y,
        d_skip=rowv(d_skip), w_glu=w_glu.astype(_BF16), b_glu=rowv(b_glu), g_v=rowv(g_v), b_v=rowv(b_v),
        g_out_s=rowv(g_out_s), g_out_g=rowv(g_out_g), w_out=w_out.astype(_BF16),
        g_ffn=rowv(g_ffn), w_up=w_up.astype(_BF16), w_down=w_down.astype(_BF16), g_ple=rowv(g_ple),
        w_ple_gate=w_ple_gate.astype(_BF16), w_ple=w_ple.astype(_BF16),
        mixw_p=w_s,
        mixb_p=jnp.einsum('dht,hc->dtc', b_s, head_of, precision=_EXACT),
        mixw_s=jnp.einsum('rt,dhts,cs->dhrc', pos_of, w_s4, pos_of, precision=_EXACT),
        mixb_s=jnp.einsum('dht,rt,hc->drc', b_s[:, :, :dec_seq], pos_of, head_of, precision=_EXACT))
    g_fin = g_final.reshape(1, 1, D_MODEL)

    zeros = jnp.zeros((1, batch, SSM_FLAT), _F32)
    s0_re = state_ssm_re.reshape(DEPTH, dec_batch, SSM_FLAT)
    s0_im = state_ssm_im.reshape(DEPTH, dec_batch, SSM_FLAT)
    pp = p_prompt.reshape(DEPTH, n_p, PLE_DIM)
    ps = p_sample.reshape(DEPTH, n_s, PLE_DIM)
    h_p = x_prompt
    h_s = x_sample.reshape(1, n_s, D_MODEL)
    re_p, im_p, re_s, im_s, v_s = [], [], [], [], []
    for i in range(DEPTH):
        final = i == DEPTH - 1
        h_p, sre, sim, _ = _mixer(h_p, zeros, zeros, w, i, h0_layer=0, nseq=batch, t_len=CHUNK,
                                  blk_pitch=CHUNK // S5_BLOCK + SCAN_PAD_BLOCKS, keep_v=False)
        re_p.append(sre)
        im_p.append(sim)
        h_p = _ffn(h_p.reshape(n_p, D_MODEL), pp, w, g_fin, i, final=final).reshape(batch, seq, D_MODEL)

        h_s, sre, sim, vn = _mixer(h_s, s0_re, s0_im, w, i, h0_layer=i, nseq=dec_batch, t_len=dec_seq,
                                   blk_pitch=1, keep_v=True)
        re_s.append(sre)
        im_s.append(sim)
        v_s.append(vn)
        h_s = _ffn(h_s.reshape(n_s, D_MODEL), ps, w, g_fin, i, final=final).reshape(1, n_s, D_MODEL)

    st_p = (DEPTH, batch, SSM_GROUPS, SSM_STATE)
    st_s = (DEPTH, dec_batch, SSM_GROUPS, SSM_STATE)
    return (h_p, h_s.reshape(dec_batch, dec_seq, D_MODEL),
            jnp.stack(re_p).reshape(st_p), jnp.stack(im_p).reshape(st_p),
            jnp.stack(re_s).reshape(st_s), jnp.stack(im_s).reshape(st_s),
            jnp.stack(v_s).reshape(DEPTH, dec_batch, dec_seq, GMLP_WIDTH))
```

```python
import functools
import math

import jax
import jax.numpy as jnp
from jax import lax
from jax.experimental import pallas as pl
from jax.experimental.pallas import tpu as pltpu

D_MODEL = 1024
DEPTH = 2
SSM_WIDTH = 512
GMLP_WIDTH = 512
SSM_GROUP = 16
SSM_GROUPS = 32
SSM_STATE = 64
SSM_FLAT = SSM_GROUPS * SSM_STATE
CHUNK = 128
GMLP_HEADS = 4
GMLP_HEAD_DIM = 128
PLE_DIM = 256
D_FF = 4096
IN_WIDTH = SSM_WIDTH + 2 * GMLP_WIDTH
EPS = 1e-6
LAM_RE_MAX = -1e-4

LANES = 128
SUBLANES = 8
S5_SLICES = SSM_WIDTH // LANES
S5_SLICE_STATES = SSM_FLAT // S5_SLICES
S5_SLABS = S5_SLICE_STATES // LANES
S5_BLOCK = 4
SCAN_PAD_BLOCKS = 4
SCAN_UNROLL = 2
ROW_TILE = 256
FFN_ROWS = 512
FFN_COLS = 1024
MIXER_VMEM_BYTES = 56 * 1024 * 1024
FFN_VMEM_BYTES = 52 * 1024 * 1024

_F32 = jnp.float32
_BF16 = jnp.bfloat16
_EXACT = lax.Precision.HIGHEST


def _dot(a, b):
    return jnp.dot(a, b, preferred_element_type=_F32)


def _rmsnorm(x, g):
    r = lax.rsqrt(jnp.mean(x * x, axis=-1, keepdims=True) + EPS)
    return x * r * g


def _gelu(x):
    c = math.sqrt(2.0 / math.pi)
    return x * (0.5 * (1.0 + jnp.tanh(c * (x + 0.044715 * (x * x * x)))))


def _disc_kernel(lr_ref, li_ref, ldt_ref, br_ref, bi_ref, abr_ref, abi_ref, bbr_ref, bbi_ref):
    lr = jnp.minimum(lr_ref[...], LAM_RE_MAX)
    li = li_ref[...]
    dt = jnp.exp(ldt_ref[...])
    mag = jnp.exp(lr * dt)
    abr = mag * jnp.cos(li * dt)
    abi = mag * jnp.sin(li * dt)
    den = lr * lr + li * li
    nr = abr - 1.0
    ni = abi
    qr = (nr * lr + ni * li) / den
    qi = (ni * lr - nr * li) / den
    abr_ref[...] = abr
    abi_ref[...] = abi
    for h in range(SSM_GROUP):
        br = br_ref[h]
        bi = bi_ref[h]
        bbr_ref[h] = qr * br - qi * bi
        bbi_ref[h] = qr * bi + qi * br


def _discretise(lam_re, lam_im, log_dt, b_re, b_im):
    dg = DEPTH * SSM_GROUPS
    lr = lam_re.reshape(dg, SSM_STATE)
    li = lam_im.reshape(dg, SSM_STATE)
    ldt = log_dt.reshape(dg, 1)
    br = jnp.moveaxis(b_re, -1, 0).reshape(SSM_GROUP, dg, SSM_STATE)
    bi = jnp.moveaxis(b_im, -1, 0).reshape(SSM_GROUP, dg, SSM_STATE)
    vec = jax.ShapeDtypeStruct((dg, SSM_STATE), _F32)
    mat = jax.ShapeDtypeStruct((SSM_GROUP, dg, SSM_STATE), _F32)
    abr, abi, bbr, bbi = pl.pallas_call(
        _disc_kernel, out_shape=(vec, vec, mat, mat), name="s5_discretise")(lr, li, ldt, br, bi)
    shp = (SSM_GROUP, DEPTH, SSM_GROUPS, SSM_STATE)
    vshp = (DEPTH, SSM_GROUPS, SSM_STATE)
    return (abr.reshape(vshp), abi.reshape(vshp),
            jnp.moveaxis(bbr.reshape(shp), 0, 2), jnp.moveaxis(bbi.reshape(shp), 0, 2))


def _cmul(a, b):
    return a[0] * b[0] - a[1] * b[1], a[0] * b[1] + a[1] * b[0]


S5_SLICE_GROUPS = SSM_GROUPS // S5_SLICES


def _same_group(shape, row_group, col_group):
    rg = lax.broadcasted_iota(jnp.int32, shape, 0) >> (row_group.bit_length() - 1)
    cg = lax.broadcasted_iota(jnp.int32, shape, 1) >> (col_group.bit_length() - 1)
    return rg == cg


def _expand_kernel(ab_ref, cat_ref, cab_ref, t_state_ref, t_chan_ref, wv_ref, wy_ref):
    r = S5_BLOCK
    n_st = S5_SLICE_STATES
    mask_v = _same_group((LANES, n_st), SSM_GROUP, SSM_STATE)
    mask_s = _same_group((n_st, LANES), SSM_STATE, SSM_GROUP)
    mask_u = _same_group((LANES, LANES), SSM_GROUP, SSM_GROUP)
    for m in range(r):
        for part in range(2):
            e = _dot(ab_ref[r - 1 - m, part].astype(_BF16), t_state_ref[...])
            wv_ref[m * LANES:(m + 1) * LANES, part * n_st:(part + 1) * n_st] = (
                jnp.where(mask_v, e, 0.0).astype(_BF16))
    for i in range(r):
        cols = slice(i * LANES, (i + 1) * LANES)
        for part in range(2):
            e = jnp.where(mask_s, _dot(cat_ref[i, part].astype(_BF16), t_chan_ref[...]), 0.0)
            wy_ref[part * n_st:(part + 1) * n_st, cols] = (e if part == 0 else -e).astype(_BF16)
        for m in range(r):
            if m <= i:
                e = jnp.where(mask_u, _dot(cab_ref[i - m].astype(_BF16), t_chan_ref[...]), 0.0)
            else:
                e = jnp.zeros((LANES, LANES), _F32)
            wy_ref[2 * n_st + m * LANES:2 * n_st + (m + 1) * LANES, cols] = e.astype(_BF16)


def _s5_matrices(abr, abi, bbr, bbi, c_re, c_im):
    r = S5_BLOCK
    gps = S5_SLICE_GROUPS
    a1 = (abr[:, :, None, :], abi[:, :, None, :])
    apow = [(jnp.ones_like(a1[0]), jnp.zeros_like(a1[1]))]
    for _ in range(r):
        apow.append(_cmul(apow[-1], a1))
    ab = [_cmul(apow[k], (bbr, bbi)) for k in range(r)]
    ca = [_cmul((c_re, c_im), apow[k]) for k in range(1, r + 1)]
    cab = [jnp.einsum('dghp,dgop->dgho', ab[k][0], c_re, precision=_EXACT)
           - jnp.einsum('dghp,dgop->dgho', ab[k][1], c_im, precision=_EXACT) for k in range(r)]

    def per_slice(m):
        return m.reshape(DEPTH, S5_SLICES, gps * m.shape[2], m.shape[3])

    ab_c = jnp.stack([jnp.stack([per_slice(p) for p in ab[k]], axis=2) for k in range(r)], axis=2)
    cat_c = jnp.stack([jnp.stack([per_slice(jnp.swapaxes(p, -1, -2)) for p in ca[k]], axis=2)
                       for k in range(r)], axis=2)
    cab_c = jnp.stack([per_slice(cab[k]) for k in range(r)], axis=2)
    t_state = jnp.tile(jnp.eye(SSM_STATE, dtype=_BF16), (1, gps))
    t_chan = jnp.tile(jnp.eye(SSM_GROUP, dtype=_BF16), (1, gps))

    def blk(a):
        nd = a.ndim - 2
        return pl.BlockSpec((None, None) + a.shape[2:], lambda d, s: (d, s) + (0,) * nd)

    def whole(a):
        return pl.BlockSpec(a.shape, lambda d, s: (0, 0))

    v_shape = (DEPTH, S5_SLICES, r * LANES, 2 * S5_SLICE_STATES)
    y_shape = (DEPTH, S5_SLICES, 2 * S5_SLICE_STATES + r * LANES, r * LANES)
    w_v, w_y = pl.pallas_call(
        _expand_kernel,
        grid=(DEPTH, S5_SLICES),
        in_specs=[blk(ab_c), blk(cat_c), blk(cab_c), whole(t_state), whole(t_chan)],
        out_specs=(pl.BlockSpec((None, None) + v_shape[2:], lambda d, s: (d, s, 0, 0)),
                   pl.BlockSpec((None, None) + y_shape[2:], lambda d, s: (d, s, 0, 0))),
        out_shape=(jax.ShapeDtypeStruct(v_shape, _BF16), jax.ShapeDtypeStruct(y_shape, _BF16)),
        name="s5_expand",
    )(ab_c, cat_c, cab_c, t_state, t_chan)
    a_blk = (apow[r][0].reshape(DEPTH, 1, SSM_FLAT), apow[r][1].reshape(DEPTH, 1, SSM_FLAT))
    return w_v, w_y, a_blk


def _tile_rows(ref, i, rows):
    nb, tb, c = ref.shape
    if nb == 1:
        return ref[0, i * rows:(i + 1) * rows, :]
    per = rows // tb
    return ref[i * per:(i + 1) * per].reshape(rows, c)


def _store_tile_rows(ref, i, rows, val):
    nb, tb, c = ref.shape
    if nb == 1:
        ref[0, i * rows:(i + 1) * rows, :] = val
    else:
        per = rows // tb
        ref[i * per:(i + 1) * per] = val.reshape(per, tb, c)


def _load_slabs(ref, i, rows, t_len, seq_rows):
    if seq_rows == t_len:
        return jnp.concatenate([ref[s, i * rows:(i + 1) * rows, :] for s in range(S5_SLICES)], axis=-1)
    per = rows // t_len
    return jnp.concatenate(
        [jnp.concatenate([ref[s, n * seq_rows:n * seq_rows + t_len, :] for s in range(S5_SLICES)], axis=-1)
         for n in range(i * per, (i + 1) * per)], axis=0)


def _store_slabs(ref, i, rows, t_len, seq_rows, val):
    for s in range(S5_SLICES):
        cols = slice(s * LANES, (s + 1) * LANES)
        if seq_rows == t_len:
            ref[s, i * rows:(i + 1) * rows, :] = val[:, cols]
        else:
            per = rows // t_len
            for k in range(per):
                n = i * per + k
                ref[s, n * seq_rows:n * seq_rows + t_len, :] = val[k * t_len:(k + 1) * t_len, cols]


MIXER_INPUTS = 19


def _mixer_kernel(*refs, nseq, t_len, blk_pitch, keep_v):
    (x_ref, h0r_ref, h0i_ref, g_mix_ref, w_in_ref, abr_ref, abi_ref, wv_ref, wy_ref,
     dskip_ref, w_glu_ref, b_glu_ref, g_v_ref, b_v_ref, mixw_ref, mixb_ref,
     g_os_ref, g_og_ref, w_out_ref) = refs[:MIXER_INPUTS]
    out_ref, sre_ref, sim_ref = refs[MIXER_INPUTS:MIXER_INPUTS + 3]
    if keep_v:
        vn_ref, us_scr, ys_scr, yg_scr, hb_scr = refs[MIXER_INPUTS + 3:]
    else:
        us_scr, ys_scr, yg_scr, hb_scr = refs[MIXER_INPUTS + 3:]
    rows = nseq * t_len
    n_tiles = rows // ROW_TILE
    n_blk = t_len // S5_BLOCK
    seq_rows = S5_BLOCK * blk_pitch
    n_buf = nseq * blk_pitch

    @pl.when(pl.program_id(0) == 0)
    def _():
        sre_ref[...] = h0r_ref[...]
        sim_ref[...] = h0i_ref[...]
        us_scr[...] = jnp.zeros_like(us_scr)

    row = lax.broadcasted_iota(jnp.int32, (CHUNK, CHUNK), 0)
    col = lax.broadcasted_iota(jnp.int32, (CHUNK, CHUNK), 1)
    t_shift = t_len.bit_length() - 1
    causal = ((row >> t_shift) == (col >> t_shift)) & ((col & (t_len - 1)) <= (row & (t_len - 1)))
    mixw = [jnp.where(causal, mixw_ref[h], 0.0).astype(_BF16) for h in range(GMLP_HEADS)]

    def in_proj(i):
        x = _tile_rows(x_ref, i, ROW_TILE)
        a = _rmsnorm(x, g_mix_ref[...]).astype(_BF16)
        return _dot(a, w_in_ref[...])

    z_next = in_proj(0)
    for i in range(n_tiles):
        z = z_next
        if i + 1 < n_tiles:
            z_next = in_proj(i + 1)
        _store_slabs(us_scr, i, ROW_TILE, t_len, seq_rows, z[:, :SSM_WIDTH])
        zg = _gelu(z[:, SSM_WIDTH:])
        ug = zg[:, :GMLP_WIDTH]
        v = zg[:, GMLP_WIDTH:]
        mu = jnp.mean(v, axis=-1, keepdims=True)
        vc = v - mu
        r = lax.rsqrt(jnp.mean(vc * vc, axis=-1, keepdims=True) + EPS)
        vn = vc * r * g_v_ref[...] + b_v_ref[...]
        if keep_v:
            vn_ref[i * ROW_TILE:(i + 1) * ROW_TILE, :] = vn
        vn = vn.astype(_BF16)
        for b in range(ROW_TILE // CHUNK):
            rb = slice(b * CHUNK, (b + 1) * CHUNK)
            mixed = [_dot(mixw[h], vn[rb, h * GMLP_HEAD_DIM:(h + 1) * GMLP_HEAD_DIM])
                     for h in range(GMLP_HEADS)]
            m = jnp.concatenate(mixed, axis=-1) + mixb_ref[...]
            yg = _rmsnorm(ug[rb, :] * m, g_og_ref[...]).astype(_BF16)
            yg_scr[i * ROW_TILE + b * CHUNK:i * ROW_TILE + (b + 1) * CHUNK, :] = yg

    for s in range(S5_SLICES):
        u_blk = jnp.concatenate([us_scr[s, pl.ds(m, n_buf, stride=S5_BLOCK), :] for m in range(S5_BLOCK)],
                                axis=-1).astype(_BF16)
        v_in = _dot(u_blk, wv_ref[s])
        for c in range(2 * S5_SLABS):
            hb_scr[c] = v_in[:, c * LANES:(c + 1) * LANES]

        s0 = s * S5_SLICE_STATES
        a_re = [jnp.broadcast_to(abr_ref[:, s0 + c * LANES:s0 + (c + 1) * LANES], (nseq, LANES))
                for c in range(S5_SLABS)]
        a_im = [jnp.broadcast_to(abi_ref[:, s0 + c * LANES:s0 + (c + 1) * LANES], (nseq, LANES))
                for c in range(S5_SLABS)]
        h_init = tuple(
            (sre_ref[:, s0 + c * LANES:s0 + (c + 1) * LANES], sim_ref[:, s0 + c * LANES:s0 + (c + 1) * LANES])
            for c in range(S5_SLABS))

        def scan_step(j, h):
            new = []
            for c in range(S5_SLABS):
                rows_j = pl.ds(j, nseq, stride=blk_pitch)
                hr, hi = h[c]
                v_re = hb_scr[c, rows_j, :]
                v_im = hb_scr[S5_SLABS + c, rows_j, :]
                hb_scr[c, rows_j, :] = hr
                hb_scr[S5_SLABS + c, rows_j, :] = hi
                new.append((a_re[c] * hr - a_im[c] * hi + v_re, a_re[c] * hi + a_im[c] * hr + v_im))
            return tuple(new)
        if n_blk == 1:
            h_fin = scan_step(0, h_init)
        else:
            h_fin = lax.fori_loop(0, n_blk, scan_step, h_init, unroll=SCAN_UNROLL)
        for c in range(S5_SLABS):
            sre_ref[:, s0 + c * LANES:s0 + (c + 1) * LANES] = h_fin[c][0]
            sim_ref[:, s0 + c * LANES:s0 + (c + 1) * LANES] = h_fin[c][1]

        lhs = jnp.concatenate([hb_scr[c].astype(_BF16) for c in range(2 * S5_SLABS)] + [u_blk], axis=-1)
        y_blk = _dot(lhs, wy_ref[s])
        for i in range(S5_BLOCK):
            ys_scr[s, pl.ds(i, n_buf, stride=S5_BLOCK), :] = y_blk[:, i * LANES:(i + 1) * LANES]

    def glu_in(i):
        ys = (_load_slabs(ys_scr, i, ROW_TILE, t_len, seq_rows)
              + dskip_ref[...] * _load_slabs(us_scr, i, ROW_TILE, t_len, seq_rows))
        ys = _gelu(ys)
        return ys, _dot(ys.astype(_BF16), w_glu_ref[...])

    nxt = glu_in(0)
    for i in range(n_tiles):
        ys, gate = nxt
        if i + 1 < n_tiles:
            nxt = glu_in(i + 1)
        ys = ys * jax.nn.sigmoid(gate + b_glu_ref[...])
        ys = _rmsnorm(ys, g_os_ref[...]).astype(_BF16)
        y = (_dot(ys, w_out_ref[:SSM_WIDTH, :])
             + _dot(yg_scr[i * ROW_TILE:(i + 1) * ROW_TILE, :], w_out_ref[SSM_WIDTH:, :]))
        _store_tile_rows(out_ref, i, ROW_TILE, _tile_rows(x_ref, i, ROW_TILE) + y)


def _layer_spec(arr, layer):
    nd = arr.ndim - 1
    return pl.BlockSpec((None,) + arr.shape[1:], lambda j: (layer,) + (0,) * nd,
                        pipeline_mode=pl.Buffered(1))


MIXER_WEIGHTS = ('g_mix', 'w_in', 'a_blk_re', 'a_blk_im', 'w_v', 'w_y', 'd_skip', 'w_glu', 'b_glu', 'g_v', 'b_v',
                 'mixw', 'mixb', 'g_out_s', 'g_out_g', 'w_out')
FFN_WEIGHTS = ('g_ffn', 'w_up', 'w_down', 'g_ple', 'w_ple_gate', 'w_ple')


def _mixer(x, h0r, h0i, w, layer, *, h0_layer, nseq, t_len, blk_pitch, keep_v):
    nb, length, _ = x.shape
    t_blk = t_len if nb == nseq else nseq * t_len
    rows = nseq * t_len
    steps = length // t_blk
    slab_rows = nseq * S5_BLOCK * blk_pitch
    x_spec = pl.BlockSpec((nb, t_blk, D_MODEL), lambda j: (0, j, 0))
    st_spec = pl.BlockSpec((nseq, SSM_FLAT), lambda j: (0, 0))
    consts = [w[k + ('_s' if keep_v else '_p')] if k in ('mixw', 'mixb') else w[k] for k in MIXER_WEIGHTS]
    assert len(consts) + 3 == MIXER_INPUTS
    out_specs = [x_spec, st_spec, st_spec]
    out_shape = [jax.ShapeDtypeStruct(x.shape, _F32),
                 jax.ShapeDtypeStruct((nseq, SSM_FLAT), _F32),
                 jax.ShapeDtypeStruct((nseq, SSM_FLAT), _F32)]
    scratch = [pltpu.VMEM((S5_SLICES, slab_rows, LANES), _F32),
               pltpu.VMEM((S5_SLICES, slab_rows, LANES), _F32),
               pltpu.VMEM((rows, GMLP_WIDTH), _BF16),
               pltpu.VMEM((2 * S5_SLABS, nseq * blk_pitch, LANES), _F32)]
    if keep_v:
        out_specs.append(pl.BlockSpec((rows, GMLP_WIDTH), lambda j: (j, 0)))
        out_shape.append(jax.ShapeDtypeStruct((steps * rows, GMLP_WIDTH), _F32))
    res = pl.pallas_call(
        functools.partial(_mixer_kernel, nseq=nseq, t_len=t_len, blk_pitch=blk_pitch, keep_v=keep_v),
        grid=(steps,),
        in_specs=([x_spec, _layer_spec(h0r, h0_layer), _layer_spec(h0i, h0_layer)]
                  + [_layer_spec(c, layer) for c in consts]),
        out_specs=tuple(out_specs),
        out_shape=tuple(out_shape),
        scratch_shapes=scratch,
        compiler_params=pltpu.CompilerParams(dimension_semantics=("arbitrary",),
                                             vmem_limit_bytes=MIXER_VMEM_BYTES),
        name="mixer_sample" if keep_v else "mixer_prompt",
    )(x, h0r, h0i, *consts)
    return res if keep_v else (*res, None)


def _ffn_kernel(hp_ref, hs_ref, pp_ref, ps_ref, g_ffn_ref, w_up_ref, w_down_ref, g_ple_ref, w_gate_ref,
                w_ple_ref, g_fin_ref, outp_ref, outs_ref, *, final, prompt_steps):
    def rows_block(h_ref, p_ref, out_ref):
        h = h_ref[...]
        f = _rmsnorm(h, g_ffn_ref[...]).astype(_BF16)
        acc = h
        for c in range(D_FF // FFN_COLS):
            up = _dot(f, w_up_ref[:, c * FFN_COLS:(c + 1) * FFN_COLS])
            act = jnp.square(jnp.maximum(up, 0.0)).astype(_BF16)
            acc = acc + _dot(act, w_down_ref[c * FFN_COLS:(c + 1) * FFN_COLS, :])
        gate = jax.nn.sigmoid(_dot(_rmsnorm(acc, g_ple_ref[...]).astype(_BF16), w_gate_ref[...]))
        out = acc + gate * _dot(p_ref[...].astype(_BF16), w_ple_ref[...])
        if final:
            out = _rmsnorm(out, g_fin_ref[...])
        out_ref[...] = out

    is_prompt = pl.program_id(0) < prompt_steps
    pl.when(is_prompt)(functools.partial(rows_block, hp_ref, pp_ref, outp_ref))
    pl.when(jnp.logical_not(is_prompt))(functools.partial(rows_block, hs_ref, ps_ref, outs_ref))


def _ffn(h_p, h_s, p_p, p_s, w, g_final, layer, *, final):
    steps_p = h_p.shape[0] // FFN_ROWS
    steps_s = h_s.shape[0] // FFN_ROWS
    consts = [w[k] for k in FFN_WEIGHTS]
    prompt_blk = lambda j: jnp.minimum(j, steps_p - 1)
    sample_blk = lambda j: jnp.maximum(j - steps_p, 0)
    return pl.pallas_call(
        functools.partial(_ffn_kernel, final=final, prompt_steps=steps_p),
        grid=(steps_p + steps_s,),
        in_specs=([pl.BlockSpec((FFN_ROWS, D_MODEL), lambda j: (prompt_blk(j), 0)),
                   pl.BlockSpec((FFN_ROWS, D_MODEL), lambda j: (sample_blk(j), 0),
                                pipeline_mode=pl.Buffered(1)),
                   pl.BlockSpec((None, FFN_ROWS, PLE_DIM), lambda j: (layer, prompt_blk(j), 0)),
                   pl.BlockSpec((None, FFN_ROWS, PLE_DIM), lambda j: (layer, sample_blk(j), 0),
                                pipeline_mode=pl.Buffered(1))]
                  + [_layer_spec(c, layer) for c in consts] + [_layer_spec(g_final, 0)]),
        out_specs=(pl.BlockSpec((FFN_ROWS, D_MODEL), lambda j: (prompt_blk(j), 0)),
                   pl.BlockSpec((FFN_ROWS, D_MODEL), lambda j: (sample_blk(j), 0))),
        out_shape=(jax.ShapeDtypeStruct(h_p.shape, _F32), jax.ShapeDtypeStruct(h_s.shape, _F32)),
        compiler_params=pltpu.CompilerParams(dimension_semantics=("arbitrary",),
                                             vmem_limit_bytes=FFN_VMEM_BYTES),
        name="ffn",
    )(h_p, h_s, p_p, p_s, *consts, g_final)


def kernel(x_prompt, x_sample, state_ssm_re, state_ssm_im, p_prompt, p_sample, g_mix, w_in, lam_re, lam_im, log_dt, b_re, b_im, c_re, c_im, d_skip, w_glu, b_glu, g_v, b_v, w_s, b_s, g_out_s, g_out_g, w_out, g_ffn, w_up, w_down, g_ple, w_ple_gate, w_ple, g_final):
    batch, seq, _ = x_prompt.shape
    dec_batch, dec_seq, _ = x_sample.shape
    n_p, n_s = batch * seq, dec_batch * dec_seq
    assert seq % CHUNK == 0 and batch == SUBLANES
    assert dec_seq == S5_BLOCK and CHUNK % dec_seq == 0 and n_s % ROW_TILE == 0

    abr, abi, bbr, bbi = _discretise(lam_re, lam_im, log_dt, b_re, b_im)
    w_v, w_y, a_blk = _s5_matrices(abr, abi, bbr, bbi, c_re, c_im)
    rowv = lambda v: v.reshape(DEPTH, 1, -1)
    head_of = (jnp.arange(GMLP_WIDTH)[None, :] // GMLP_HEAD_DIM == jnp.arange(GMLP_HEADS)[:, None]).astype(_F32)
    pos_of = (jnp.arange(CHUNK)[:, None] % dec_seq == jnp.arange(dec_seq)[None, :]).astype(_F32)
    w_s4 = w_s[:, :, :dec_seq, :dec_seq]
    w = dict(
        g_mix=rowv(g_mix), w_in=w_in.astype(_BF16), a_blk_re=a_blk[0], a_blk_im=a_blk[1], w_v=w_v, w_y=w_y,
        d_skip=rowv(d_skip), w_glu=w_glu.astype(_BF16), b_glu=rowv(b_glu), g_v=rowv(g_v), b_v=rowv(b_v),
        g_out_s=rowv(g_out_s), g_out_g=rowv(g_out_g), w_out=w_out.astype(_BF16),
        g_ffn=rowv(g_ffn), w_up=w_up.astype(_BF16), w_down=w_down.astype(_BF16), g_ple=rowv(g_ple),
        w_ple_gate=w_ple_gate.astype(_BF16), w_ple=w_ple.astype(_BF16),
        mixw_p=w_s,
        mixb_p=jnp.einsum('dht,hc->dtc', b_s, head_of, precision=_EXACT),
        mixw_s=jnp.einsum('rt,dhts,cs->dhrc', pos_of, w_s4, pos_of, precision=_EXACT),
        mixb_s=jnp.einsum('dht,rt,hc->drc', b_s[:, :, :dec_seq], pos_of, head_of, precision=_EXACT))
    g_fin = g_final.reshape(1, 1, D_MODEL)

    zeros = jnp.zeros((1, batch, SSM_FLAT), _F32)
    s0_re = state_ssm_re.reshape(DEPTH, dec_batch, SSM_FLAT)
    s0_im = state_ssm_im.reshape(DEPTH, dec_batch, SSM_FLAT)
    pp = p_prompt.reshape(DEPTH, n_p, PLE_DIM)
    ps = p_sample.reshape(DEPTH, n_s, PLE_DIM)
    h_p = x_prompt
    h_s = x_sample.reshape(1, n_s, D_MODEL)
    re_p, im_p, re_s, im_s, v_s = [], [], [], [], []
    for i in range(DEPTH):
        final = i == DEPTH - 1
        h_p, sre, sim, _ = _mixer(h_p, zeros, zeros, w, i, h0_layer=0, nseq=batch, t_len=CHUNK,
                                  blk_pitch=CHUNK // S5_BLOCK + SCAN_PAD_BLOCKS, keep_v=False)
        re_p.append(sre)
        im_p.append(sim)
        h_s, sre, sim, vn = _mixer(h_s, s0_re, s0_im, w, i, h0_layer=i, nseq=dec_batch, t_len=dec_seq,
                                   blk_pitch=1, keep_v=True)
        re_s.append(sre)
        im_s.append(sim)
        v_s.append(vn)
        h_p, h_s = _ffn(h_p.reshape(n_p, D_MODEL), h_s.reshape(n_s, D_MODEL), pp, ps, w, g_fin, i, final=final)
        h_p = h_p.reshape(batch, seq, D_MODEL)
        h_s = h_s.reshape(1, n_s, D_MODEL)

    st_p = (DEPTH, batch, SSM_GROUPS, SSM_STATE)
    st_s = (DEPTH, dec_batch, SSM_GROUPS, SSM_STATE)
    return (h_p, h_s.reshape(dec_batch, dec_seq, D_MODEL),
            jnp.stack(re_p).reshape(st_p), jnp.stack(im_p).reshape(st_p),
            jnp.stack(re_s).reshape(st_s), jnp.stack(im_s).reshape(st_s),
            jnp.stack(v_s).reshape(DEPTH, dec_batch, dec_seq, GMLP_WIDTH))
```

```python
import functools
import math

import jax
import jax.numpy as jnp
from jax import lax
from jax.experimental import pallas as pl
from jax.experimental.pallas import tpu as pltpu

D_MODEL = 1024
DEPTH = 2
SSM_WIDTH = 512
GMLP_WIDTH = 512
SSM_GROUP = 16
SSM_GROUPS = 32
SSM_STATE = 64
SSM_FLAT = SSM_GROUPS * SSM_STATE
CHUNK = 128
GMLP_HEADS = 4
GMLP_HEAD_DIM = 128
PLE_DIM = 256
D_FF = 4096
IN_WIDTH = SSM_WIDTH + 2 * GMLP_WIDTH
EPS = 1e-6
LAM_RE_MAX = -1e-4

LANES = 128
SUBLANES = 8
S5_SLICES = SSM_WIDTH // LANES
S5_SLICE_STATES = SSM_FLAT // S5_SLICES
S5_SLABS = S5_SLICE_STATES // LANES
S5_BLOCK = 4
SCAN_PAD_BLOCKS = 4
SCAN_UNROLL = 2
ROW_TILE = 256
FFN_ROWS = 512
FFN_COLS = 1024
MIXER_VMEM_BYTES = 56 * 1024 * 1024
FFN_VMEM_BYTES = 52 * 1024 * 1024

_F32 = jnp.float32
_BF16 = jnp.bfloat16
_EXACT = lax.Precision.HIGHEST


def _dot(a, b):
    return jnp.dot(a, b, preferred_element_type=_F32)


def _rmsnorm(x, g):
    r = lax.rsqrt(jnp.mean(x * x, axis=-1, keepdims=True) + EPS)
    return x * r * g


def _gelu(x):
    c = math.sqrt(2.0 / math.pi)
    return x * (0.5 * (1.0 + jnp.tanh(c * (x + 0.044715 * (x * x * x)))))


def _disc_kernel(lr_ref, li_ref, ldt_ref, br_ref, bi_ref, abr_ref, abi_ref, bbr_ref, bbi_ref):
    lr = jnp.minimum(lr_ref[...], LAM_RE_MAX)
    li = li_ref[...]
    dt = jnp.exp(ldt_ref[...])
    mag = jnp.exp(lr * dt)
    abr = mag * jnp.cos(li * dt)
    abi = mag * jnp.sin(li * dt)
    den = lr * lr + li * li
    nr = abr - 1.0
    ni = abi
    qr = (nr * lr + ni * li) / den
    qi = (ni * lr - nr * li) / den
    abr_ref[...] = abr
    abi_ref[...] = abi
    for h in range(SSM_GROUP):
        br = br_ref[h]
        bi = bi_ref[h]
        bbr_ref[h] = qr * br - qi * bi
        bbi_ref[h] = qr * bi + qi * br


def _discretise(lam_re, lam_im, log_dt, b_re, b_im):
    dg = DEPTH * SSM_GROUPS
    lr = lam_re.reshape(dg, SSM_STATE)
    li = lam_im.reshape(dg, SSM_STATE)
    ldt = log_dt.reshape(dg, 1)
    br = jnp.moveaxis(b_re, -1, 0).reshape(SSM_GROUP, dg, SSM_STATE)
    bi = jnp.moveaxis(b_im, -1, 0).reshape(SSM_GROUP, dg, SSM_STATE)
    vec = jax.ShapeDtypeStruct((dg, SSM_STATE), _F32)
    mat = jax.ShapeDtypeStruct((SSM_GROUP, dg, SSM_STATE), _F32)
    abr, abi, bbr, bbi = pl.pallas_call(
        _disc_kernel, out_shape=(vec, vec, mat, mat), name="s5_discretise")(lr, li, ldt, br, bi)
    vshp = (DEPTH, SSM_GROUPS, SSM_STATE)
    return abr.reshape(vshp), abi.reshape(vshp), bbr, bbi


def _cmul(a, b):
    return a[0] * b[0] - a[1] * b[1], a[0] * b[1] + a[1] * b[0]


S5_SLICE_GROUPS = SSM_GROUPS // S5_SLICES


def _same_group(shape, row_group, col_group):
    rg = lax.broadcasted_iota(jnp.int32, shape, 0) >> (row_group.bit_length() - 1)
    cg = lax.broadcasted_iota(jnp.int32, shape, 1) >> (col_group.bit_length() - 1)
    return rg == cg


def _prepare_kernel(a_ref, bb_ref, cre_ref, cim_ref, t_state_ref, wv_ref, wy_ref):
    r = S5_BLOCK
    n_st = S5_SLICE_STATES
    mask_v = _same_group((LANES, n_st), SSM_GROUP, SSM_STATE)
    mask_u = _same_group((LANES, LANES), SSM_GROUP, SSM_GROUP)

    def expand(x):
        return jnp.where(mask_v, _dot(x.astype(_BF16), t_state_ref[...]), 0.0)

    def contract(x, y):
        return lax.dot_general(x, y, (((1,), (1,)), ((), ())), precision=_EXACT,
                               preferred_element_type=_F32)

    for s in range(S5_SLICES):
        rows = slice(s * LANES, (s + 1) * LANES)
        a1 = (a_ref[0, rows, :], a_ref[1, rows, :])
        bb = (bb_ref[0, rows, :], bb_ref[1, rows, :])
        cc = (cre_ref[rows, :], cim_ref[rows, :])
        apow = [None, a1]
        for _ in range(r - 1):
            apow.append(_cmul(apow[-1], a1))
        ab = [bb] + [_cmul(apow[k], bb) for k in range(1, r)]
        for m in range(r):
            for part in range(2):
                wv_ref[s, m * LANES:(m + 1) * LANES, part * n_st:(part + 1) * n_st] = (
                    expand(ab[r - 1 - m][part]).astype(_BF16))
        feed = [jnp.where(mask_u, contract(ab[k][0], cc[0]) - contract(ab[k][1], cc[1]), 0.0).astype(_BF16)
                for k in range(r)]
        for i in range(r):
            cols = slice(i * LANES, (i + 1) * LANES)
            ca = _cmul(cc, apow[i + 1])
            wy_ref[s, 0:n_st, cols] = expand(ca[0]).T.astype(_BF16)
            wy_ref[s, n_st:2 * n_st, cols] = (-expand(ca[1])).T.astype(_BF16)
            for m in range(r):
                blk = feed[i - m] if m <= i else jnp.zeros((LANES, LANES), _BF16)
                wy_ref[s, 2 * n_st + m * LANES:2 * n_st + (m + 1) * LANES, cols] = blk


def _s5_matrices(abr, abi, bbr, bbi, c_re, c_im):
    r = S5_BLOCK
    n_rows = SSM_GROUPS * SSM_GROUP
    a_rep = jnp.repeat(jnp.stack([abr, abi], axis=1), SSM_GROUP, axis=2)
    bb = jnp.stack([bbr, bbi]).reshape(2, SSM_GROUP, DEPTH, SSM_GROUPS, SSM_STATE)
    bb = jnp.transpose(bb, (2, 0, 3, 1, 4)).reshape(DEPTH, 2, n_rows, SSM_STATE)
    cre = c_re.reshape(DEPTH, n_rows, SSM_STATE)
    cim = c_im.reshape(DEPTH, n_rows, SSM_STATE)
    t_state = jnp.tile(jnp.eye(SSM_STATE, dtype=_BF16), (1, S5_SLICE_GROUPS))

    def per_layer(a):
        nd = a.ndim - 1
        return pl.BlockSpec((None,) + a.shape[1:], lambda d: (d,) + (0,) * nd)

    v_shape = (DEPTH, S5_SLICES, r * LANES, 2 * S5_SLICE_STATES)
    y_shape = (DEPTH, S5_SLICES, 2 * S5_SLICE_STATES + r * LANES, r * LANES)
    w_v, w_y = pl.pallas_call(
        _prepare_kernel,
        grid=(DEPTH,),
        in_specs=[per_layer(a_rep), per_layer(bb), per_layer(cre), per_layer(cim),
                  pl.BlockSpec(t_state.shape, lambda d: (0, 0))],
        out_specs=(pl.BlockSpec((None,) + v_shape[1:], lambda d: (d, 0, 0, 0)),
                   pl.BlockSpec((None,) + y_shape[1:], lambda d: (d, 0, 0, 0))),
        out_shape=(jax.ShapeDtypeStruct(v_shape, _BF16), jax.ShapeDtypeStruct(y_shape, _BF16)),
        name="s5_prepare",
    )(a_rep, bb, cre, cim, t_state)
    a_pow = (abr, abi)
    for _ in range(r - 1):
        a_pow = _cmul(a_pow, (abr, abi))
    a_blk = jnp.stack(a_pow, axis=1).reshape(DEPTH, 2, SSM_FLAT)
    return w_v, w_y, a_blk


def _tile_rows(ref, i, rows):
    nb, tb, c = ref.shape
    if nb == 1:
        return ref[0, i * rows:(i + 1) * rows, :]
    per = rows // tb
    return ref[i * per:(i + 1) * per].reshape(rows, c)


def _store_tile_rows(ref, i, rows, val):
    nb, tb, c = ref.shape
    if nb == 1:
        ref[0, i * rows:(i + 1) * rows, :] = val
    else:
        per = rows // tb
        ref[i * per:(i + 1) * per] = val.reshape(per, tb, c)


def _load_slabs(ref, i, rows, t_len, seq_rows):
    if seq_rows == t_len:
        return jnp.concatenate([ref[s, i * rows:(i + 1) * rows, :] for s in range(S5_SLICES)], axis=-1)
    per = rows // t_len
    return jnp.concatenate(
        [jnp.concatenate([ref[s, n * seq_rows:n * seq_rows + t_len, :] for s in range(S5_SLICES)], axis=-1)
         for n in range(i * per, (i + 1) * per)], axis=0)


def _store_slabs(ref, i, rows, t_len, seq_rows, val):
    for s in range(S5_SLICES):
        cols = slice(s * LANES, (s + 1) * LANES)
        if seq_rows == t_len:
            ref[s, i * rows:(i + 1) * rows, :] = val[:, cols]
        else:
            per = rows // t_len
            for k in range(per):
                n = i * per + k
                ref[s, n * seq_rows:n * seq_rows + t_len, :] = val[k * t_len:(k + 1) * t_len, cols]


MIXER_INPUTS = 12
MIXER_VECS = ('g_mix', 'd_skip', 'b_glu', 'g_v', 'b_v', 'g_out_s', 'g_out_g')
FFN_VECS = ('g_ffn', 'g_ple', 'g_final')


def _vec_row(ref, names, name, width):
    k = names.index(name)
    return ref[k:k + 1, :width]


def _mixer_kernel(*refs, nseq, t_len, blk_pitch, keep_v):
    (x_ref, h0r_ref, h0i_ref, vec_ref, w_in_ref, a_blk_ref, wv_ref, wy_ref,
     w_glu_ref, mixw_ref, mixb_ref, w_out_ref) = refs[:MIXER_INPUTS]
    g_mix, d_skip, b_glu, g_v, b_v, g_out_s, g_out_g = (
        _vec_row(vec_ref, MIXER_VECS, n, D_MODEL if n == 'g_mix' else SSM_WIDTH) for n in MIXER_VECS)
    out_ref, sre_ref, sim_ref = refs[MIXER_INPUTS:MIXER_INPUTS + 3]
    if keep_v:
        vn_ref, us_scr, ys_scr, yg_scr, hb_scr = refs[MIXER_INPUTS + 3:]
    else:
        us_scr, ys_scr, yg_scr, hb_scr = refs[MIXER_INPUTS + 3:]
    rows = nseq * t_len
    n_tiles = rows // ROW_TILE
    n_blk = t_len // S5_BLOCK
    seq_rows = S5_BLOCK * blk_pitch
    n_buf = nseq * blk_pitch

    @pl.when(pl.program_id(0) == 0)
    def _():
        sre_ref[...] = h0r_ref[...]
        sim_ref[...] = h0i_ref[...]
        us_scr[...] = jnp.zeros_like(us_scr)

    row = lax.broadcasted_iota(jnp.int32, (CHUNK, CHUNK), 0)
    col = lax.broadcasted_iota(jnp.int32, (CHUNK, CHUNK), 1)
    t_shift = t_len.bit_length() - 1
    causal = ((row >> t_shift) == (col >> t_shift)) & ((col & (t_len - 1)) <= (row & (t_len - 1)))
    mixw = [jnp.where(causal, mixw_ref[h], 0.0).astype(_BF16) for h in range(GMLP_HEADS)]

    def in_proj(i):
        x = _tile_rows(x_ref, i, ROW_TILE)
        a = _rmsnorm(x, g_mix).astype(_BF16)
        return _dot(a, w_in_ref[...])

    z_next = in_proj(0)
    for i in range(n_tiles):
        z = z_next
        if i + 1 < n_tiles:
            z_next = in_proj(i + 1)
        _store_slabs(us_scr, i, ROW_TILE, t_len, seq_rows, z[:, :SSM_WIDTH])
        zg = _gelu(z[:, SSM_WIDTH:])
        ug = zg[:, :GMLP_WIDTH]
        v = zg[:, GMLP_WIDTH:]
        mu = jnp.mean(v, axis=-1, keepdims=True)
        vc = v - mu
        r = lax.rsqrt(jnp.mean(vc * vc, axis=-1, keepdims=True) + EPS)
        vn = vc * r * g_v + b_v
        if keep_v:
            vn_ref[i * ROW_TILE:(i + 1) * ROW_TILE, :] = vn
        vn = vn.astype(_BF16)
        for b in range(ROW_TILE // CHUNK):
            rb = slice(b * CHUNK, (b + 1) * CHUNK)
            mixed = [_dot(mixw[h], vn[rb, h * GMLP_HEAD_DIM:(h + 1) * GMLP_HEAD_DIM])
                     for h in range(GMLP_HEADS)]
            m = jnp.concatenate(mixed, axis=-1) + mixb_ref[...]
            yg = _rmsnorm(ug[rb, :] * m, g_out_g).astype(_BF16)
            yg_scr[i * ROW_TILE + b * CHUNK:i * ROW_TILE + (b + 1) * CHUNK, :] = yg

    for s in range(S5_SLICES):
        u_blk = jnp.concatenate([us_scr[s, pl.ds(m, n_buf, stride=S5_BLOCK), :] for m in range(S5_BLOCK)],
                                axis=-1).astype(_BF16)
        v_in = _dot(u_blk, wv_ref[s])
        for c in range(2 * S5_SLABS):
            hb_scr[c] = v_in[:, c * LANES:(c + 1) * LANES]

        s0 = s * S5_SLICE_STATES
        a_re = [jnp.broadcast_to(a_blk_ref[0:1, s0 + c * LANES:s0 + (c + 1) * LANES], (nseq, LANES))
                for c in range(S5_SLABS)]
        a_im = [jnp.broadcast_to(a_blk_ref[1:2, s0 + c * LANES:s0 + (c + 1) * LANES], (nseq, LANES))
                for c in range(S5_SLABS)]
        h_init = tuple(
            (sre_ref[:, s0 + c * LANES:s0 + (c + 1) * LANES], sim_ref[:, s0 + c * LANES:s0 + (c + 1) * LANES])
            for c in range(S5_SLABS))

        def scan_step(j, h):
            new = []
            for c in range(S5_SLABS):
                rows_j = pl.ds(j, nseq, stride=blk_pitch)
                hr, hi = h[c]
                v_re = hb_scr[c, rows_j, :]
                v_im = hb_scr[S5_SLABS + c, rows_j, :]
                hb_scr[c, rows_j, :] = hr
                hb_scr[S5_SLABS + c, rows_j, :] = hi
                new.append((a_re[c] * hr - a_im[c] * hi + v_re, a_re[c] * hi + a_im[c] * hr + v_im))
            return tuple(new)
        if n_blk == 1:
            h_fin = scan_step(0, h_init)
        else:
            h_fin = lax.fori_loop(0, n_blk, scan_step, h_init, unroll=SCAN_UNROLL)
        for c in range(S5_SLABS):
            sre_ref[:, s0 + c * LANES:s0 + (c + 1) * LANES] = h_fin[c][0]
            sim_ref[:, s0 + c * LANES:s0 + (c + 1) * LANES] = h_fin[c][1]

        lhs = jnp.concatenate([hb_scr[c].astype(_BF16) for c in range(2 * S5_SLABS)] + [u_blk], axis=-1)
        y_blk = _dot(lhs, wy_ref[s])
        for i in range(S5_BLOCK):
            ys_scr[s, pl.ds(i, n_buf, stride=S5_BLOCK), :] = y_blk[:, i * LANES:(i + 1) * LANES]

    def glu_in(i):
        ys = (_load_slabs(ys_scr, i, ROW_TILE, t_len, seq_rows)
              + d_skip * _load_slabs(us_scr, i, ROW_TILE, t_len, seq_rows))
        ys = _gelu(ys)
        return ys, _dot(ys.astype(_BF16), w_glu_ref[...])

    nxt = glu_in(0)
    for i in range(n_tiles):
        ys, gate = nxt
        if i + 1 < n_tiles:
            nxt = glu_in(i + 1)
        ys = ys * jax.nn.sigmoid(gate + b_glu)
        ys = _rmsnorm(ys, g_out_s).astype(_BF16)
        y = (_dot(ys, w_out_ref[:SSM_WIDTH, :])
             + _dot(yg_scr[i * ROW_TILE:(i + 1) * ROW_TILE, :], w_out_ref[SSM_WIDTH:, :]))
        _store_tile_rows(out_ref, i, ROW_TILE, _tile_rows(x_ref, i, ROW_TILE) + y)


def _layer_spec(arr, layer):
    nd = arr.ndim - 1
    return pl.BlockSpec((None,) + arr.shape[1:], lambda j: (layer,) + (0,) * nd,
                        pipeline_mode=pl.Buffered(1))


MIXER_WEIGHTS = ('mixer_vecs', 'w_in', 'a_blk', 'w_v', 'w_y', 'w_glu', 'mixw', 'mixb', 'w_out')
FFN_WEIGHTS = ('ffn_vecs', 'w_up', 'w_down', 'w_ple_gate', 'w_ple')


def _mixer(x, h0r, h0i, w, layer, *, h0_layer, nseq, t_len, blk_pitch, keep_v):
    nb, length, _ = x.shape
    t_blk = t_len if nb == nseq else nseq * t_len
    rows = nseq * t_len
    steps = length // t_blk
    slab_rows = nseq * S5_BLOCK * blk_pitch
    x_spec = pl.BlockSpec((nb, t_blk, D_MODEL), lambda j: (0, j, 0))
    st_spec = pl.BlockSpec((nseq, SSM_FLAT), lambda j: (0, 0))
    consts = [w[k + ('_s' if keep_v else '_p')] if k in ('mixw', 'mixb') else w[k] for k in MIXER_WEIGHTS]
    assert len(consts) + 3 == MIXER_INPUTS
    out_specs = [x_spec, st_spec, st_spec]
    out_shape = [jax.ShapeDtypeStruct(x.shape, _F32),
                 jax.ShapeDtypeStruct((nseq, SSM_FLAT), _F32),
                 jax.ShapeDtypeStruct((nseq, SSM_FLAT), _F32)]
    scratch = [pltpu.VMEM((S5_SLICES, slab_rows, LANES), _F32),
               pltpu.VMEM((S5_SLICES, slab_rows, LANES), _F32),
               pltpu.VMEM((rows, GMLP_WIDTH), _BF16),
               pltpu.VMEM((2 * S5_SLABS, nseq * blk_pitch, LANES), _F32)]
    if keep_v:
        out_specs.append(pl.BlockSpec((rows, GMLP_WIDTH), lambda j: (j, 0)))
        out_shape.append(jax.ShapeDtypeStruct((steps * rows, GMLP_WIDTH), _F32))
    res = pl.pallas_call(
        functools.partial(_mixer_kernel, nseq=nseq, t_len=t_len, blk_pitch=blk_pitch, keep_v=keep_v),
        grid=(steps,),
        in_specs=([x_spec, _layer_spec(h0r, h0_layer), _layer_spec(h0i, h0_layer)]
                  + [_layer_spec(c, layer) for c in consts]),
        out_specs=tuple(out_specs),
        out_shape=tuple(out_shape),
        scratch_shapes=scratch,
        compiler_params=pltpu.CompilerParams(dimension_semantics=("arbitrary",),
                                             vmem_limit_bytes=MIXER_VMEM_BYTES),
        name="mixer_sample" if keep_v else "mixer_prompt",
    )(x, h0r, h0i, *consts)
    return res if keep_v else (*res, None)


def _ffn_kernel(hp_ref, hs_ref, pp_ref, ps_ref, vec_ref, w_up_ref, w_down_ref, w_gate_ref, w_ple_ref,
                outp_ref, outs_ref, *, final, prompt_steps):
    g_ffn, g_ple, g_fin = (_vec_row(vec_ref, FFN_VECS, n, D_MODEL) for n in FFN_VECS)

    def rows_block(h_ref, p_ref, out_ref):
        h = h_ref[...]
        f = _rmsnorm(h, g_ffn).astype(_BF16)
        acc = h
        for c in range(D_FF // FFN_COLS):
            up = _dot(f, w_up_ref[:, c * FFN_COLS:(c + 1) * FFN_COLS])
            act = jnp.square(jnp.maximum(up, 0.0)).astype(_BF16)
            acc = acc + _dot(act, w_down_ref[c * FFN_COLS:(c + 1) * FFN_COLS, :])
        gate = jax.nn.sigmoid(_dot(_rmsnorm(acc, g_ple).astype(_BF16), w_gate_ref[...]))
        out = acc + gate * _dot(p_ref[...].astype(_BF16), w_ple_ref[...])
        if final:
            out = _rmsnorm(out, g_fin)
        out_ref[...] = out

    is_prompt = pl.program_id(0) < prompt_steps
    pl.when(is_prompt)(functools.partial(rows_block, hp_ref, pp_ref, outp_ref))
    pl.when(jnp.logical_not(is_prompt))(functools.partial(rows_block, hs_ref, ps_ref, outs_ref))


def _ffn(h_p, h_s, p_p, p_s, w, layer, *, final):
    steps_p = h_p.shape[0] // FFN_ROWS
    steps_s = h_s.shape[0] // FFN_ROWS
    consts = [w[k] for k in FFN_WEIGHTS]
    prompt_blk = lambda j: jnp.minimum(j, steps_p - 1)
    sample_blk = lambda j: jnp.maximum(j - steps_p, 0)
    return pl.pallas_call(
        functools.partial(_ffn_kernel, final=final, prompt_steps=steps_p),
        grid=(steps_p + steps_s,),
        in_specs=([pl.BlockSpec((FFN_ROWS, D_MODEL), lambda j: (prompt_blk(j), 0)),
                   pl.BlockSpec((FFN_ROWS, D_MODEL), lambda j: (sample_blk(j), 0),
                                pipeline_mode=pl.Buffered(1)),
                   pl.BlockSpec((None, FFN_ROWS, PLE_DIM), lambda j: (layer, prompt_blk(j), 0)),
                   pl.BlockSpec((None, FFN_ROWS, PLE_DIM), lambda j: (layer, sample_blk(j), 0),
                                pipeline_mode=pl.Buffered(1))]
                  + [_layer_spec(c, layer) for c in consts]),
        out_specs=(pl.BlockSpec((FFN_ROWS, D_MODEL), lambda j: (prompt_blk(j), 0)),
                   pl.BlockSpec((FFN_ROWS, D_MODEL), lambda j: (sample_blk(j), 0))),
        out_shape=(jax.ShapeDtypeStruct(h_p.shape, _F32), jax.ShapeDtypeStruct(h_s.shape, _F32)),
        compiler_params=pltpu.CompilerParams(dimension_semantics=("arbitrary",),
                                             vmem_limit_bytes=FFN_VMEM_BYTES),
        name="ffn",
    )(h_p, h_s, p_p, p_s, *consts)


def kernel(x_prompt, x_sample, state_ssm_re, state_ssm_im, p_prompt, p_sample, g_mix, w_in, lam_re, lam_im, log_dt, b_re, b_im, c_re, c_im, d_skip, w_glu, b_glu, g_v, b_v, w_s, b_s, g_out_s, g_out_g, w_out, g_ffn, w_up, w_down, g_ple, w_ple_gate, w_ple, g_final):
    batch, seq, _ = x_prompt.shape
    dec_batch, dec_seq, _ = x_sample.shape
    n_p, n_s = batch * seq, dec_batch * dec_seq
    assert seq % CHUNK == 0 and batch == SUBLANES
    assert dec_seq == S5_BLOCK and CHUNK % dec_seq == 0 and n_s % ROW_TILE == 0

    abr, abi, bbr, bbi = _discretise(lam_re, lam_im, log_dt, b_re, b_im)
    w_v, w_y, a_blk = _s5_matrices(abr, abi, bbr, bbi, c_re, c_im)

    def pack(vecs):
        return jnp.stack([jnp.pad(v, ((0, 0), (0, D_MODEL - v.shape[-1]))) for v in vecs], axis=1)
    head_of = (jnp.arange(GMLP_WIDTH)[None, :] // GMLP_HEAD_DIM == jnp.arange(GMLP_HEADS)[:, None]).astype(_F32)
    pos_of = (jnp.arange(CHUNK)[:, None] % dec_seq == jnp.arange(dec_seq)[None, :]).astype(_F32)
    w_s4 = w_s[:, :, :dec_seq, :dec_seq]
    w = dict(
        mixer_vecs=pack([g_mix, d_skip, b_glu, g_v, b_v, g_out_s, g_out_g]),
        ffn_vecs=pack([g_ffn, g_ple, jnp.broadcast_to(g_final, (DEPTH, D_MODEL))]),
        w_in=w_in.astype(_BF16), a_blk=a_blk, w_v=w_v, w_y=w_y, w_glu=w_glu.astype(_BF16),
        w_out=w_out.astype(_BF16), w_up=w_up.astype(_BF16), w_down=w_down.astype(_BF16),
        w_ple_gate=w_ple_gate.astype(_BF16), w_ple=w_ple.astype(_BF16),
        mixw_p=w_s,
        mixb_p=jnp.einsum('dht,hc->dtc', b_s, head_of, precision=_EXACT),
        mixw_s=jnp.einsum('rt,dhts,cs->dhrc', pos_of, w_s4, pos_of, precision=_EXACT),
        mixb_s=jnp.einsum('dht,rt,hc->drc', b_s[:, :, :dec_seq], pos_of, head_of, precision=_EXACT))

    zeros = jnp.zeros((1, batch, SSM_FLAT), _F32)
    s0_re = state_ssm_re.reshape(DEPTH, dec_batch, SSM_FLAT)
    s0_im = state_ssm_im.reshape(DEPTH, dec_batch, SSM_FLAT)
    pp = p_prompt.reshape(DEPTH, n_p, PLE_DIM)
    ps = p_sample.reshape(DEPTH, n_s, PLE_DIM)
    h_p = x_prompt
    h_s = x_sample.reshape(1, n_s, D_MODEL)
    re_p, im_p, re_s, im_s, v_s = [], [], [], [], []
    for i in range(DEPTH):
        final = i == DEPTH - 1
        h_p, sre, sim, _ = _mixer(h_p, zeros, zeros, w, i, h0_layer=0, nseq=batch, t_len=CHUNK,
                                  blk_pitch=CHUNK // S5_BLOCK + SCAN_PAD_BLOCKS, keep_v=False)
        re_p.append(sre)
        im_p.append(sim)
        h_s, sre, sim, vn = _mixer(h_s, s0_re, s0_im, w, i, h0_layer=i, nseq=dec_batch, t_len=dec_seq,
                                   blk_pitch=1, keep_v=True)
        re_s.append(sre)
        im_s.append(sim)
        v_s.append(vn)
        h_p, h_s = _ffn(h_p.reshape(n_p, D_MODEL), h_s.reshape(n_s, D_MODEL), pp, ps, w, i, final=final)
        h_p = h_p.reshape(batch, seq, D_MODEL)
        h_s = h_s.reshape(1, n_s, D_MODEL)

    st_p = (DEPTH, batch, SSM_GROUPS, SSM_STATE)
    st_s = (DEPTH, dec_batch, SSM_GROUPS, SSM_STATE)
    return (h_p, h_s.reshape(dec_batch, dec_seq, D_MODEL),
            jnp.stack(re_p).reshape(st_p), jnp.stack(im_p).reshape(st_p),
            jnp.stack(re_s).reshape(st_s), jnp.stack(im_s).reshape(st_s),
            jnp.stack(v_s).reshape(DEPTH, dec_batch, dec_seq, GMLP_WIDTH))
```

```python
import functools
import math

import jax
import jax.numpy as jnp
from jax import lax
from jax.experimental import pallas as pl
from jax.experimental.pallas import tpu as pltpu

D_MODEL = 1024
DEPTH = 2
SSM_WIDTH = 512
GMLP_WIDTH = 512
SSM_GROUP = 16
SSM_GROUPS = 32
SSM_STATE = 64
SSM_FLAT = SSM_GROUPS * SSM_STATE
CHUNK = 128
GMLP_HEADS = 4
GMLP_HEAD_DIM = 128
PLE_DIM = 256
D_FF = 4096
IN_WIDTH = SSM_WIDTH + 2 * GMLP_WIDTH
EPS = 1e-6
LAM_RE_MAX = -1e-4

LANES = 128
SUBLANES = 8
S5_SLICES = SSM_WIDTH // LANES
S5_SLICE_STATES = SSM_FLAT // S5_SLICES
S5_SLABS = S5_SLICE_STATES // LANES
S5_BLOCK = 4
SCAN_PAD_BLOCKS = 4
SCAN_UNROLL = 2
IN_TILE = 128
OUT_TILE = 256
FFN_ROWS = 512
FFN_COLS = 1024
MIXER_VMEM_BYTES = 56 * 1024 * 1024
FFN_VMEM_BYTES = 52 * 1024 * 1024

_F32 = jnp.float32
_BF16 = jnp.bfloat16
_EXACT = lax.Precision.HIGHEST


def _dot(a, b):
    return jnp.dot(a, b, preferred_element_type=_F32)


def _rmsnorm(x, g):
    r = lax.rsqrt(jnp.mean(x * x, axis=-1, keepdims=True) + EPS)
    return x * r * g


def _gelu(x):
    c = math.sqrt(2.0 / math.pi)
    return x * (0.5 * (1.0 + jnp.tanh(c * (x + 0.044715 * (x * x * x)))))


def _disc_kernel(lr_ref, li_ref, ldt_ref, br_ref, bi_ref, abr_ref, abi_ref, bbr_ref, bbi_ref):
    lr = jnp.minimum(lr_ref[...], LAM_RE_MAX)
    li = li_ref[...]
    dt = jnp.exp(ldt_ref[...])
    mag = jnp.exp(lr * dt)
    abr = mag * jnp.cos(li * dt)
    abi = mag * jnp.sin(li * dt)
    den = lr * lr + li * li
    nr = abr - 1.0
    ni = abi
    qr = (nr * lr + ni * li) / den
    qi = (ni * lr - nr * li) / den
    abr_ref[...] = abr
    abi_ref[...] = abi
    for h in range(SSM_GROUP):
        br = br_ref[h]
        bi = bi_ref[h]
        bbr_ref[h] = qr * br - qi * bi
        bbi_ref[h] = qr * bi + qi * br


def _discretise(lam_re, lam_im, log_dt, b_re, b_im):
    dg = DEPTH * SSM_GROUPS
    lr = lam_re.reshape(dg, SSM_STATE)
    li = lam_im.reshape(dg, SSM_STATE)
    ldt = log_dt.reshape(dg, 1)
    br = jnp.moveaxis(b_re, -1, 0).reshape(SSM_GROUP, dg, SSM_STATE)
    bi = jnp.moveaxis(b_im, -1, 0).reshape(SSM_GROUP, dg, SSM_STATE)
    vec = jax.ShapeDtypeStruct((dg, SSM_STATE), _F32)
    mat = jax.ShapeDtypeStruct((SSM_GROUP, dg, SSM_STATE), _F32)
    abr, abi, bbr, bbi = pl.pallas_call(
        _disc_kernel, out_shape=(vec, vec, mat, mat), name="s5_discretise")(lr, li, ldt, br, bi)
    vshp = (DEPTH, SSM_GROUPS, SSM_STATE)
    return abr.reshape(vshp), abi.reshape(vshp), bbr, bbi


def _cmul(a, b):
    return a[0] * b[0] - a[1] * b[1], a[0] * b[1] + a[1] * b[0]


S5_SLICE_GROUPS = SSM_GROUPS // S5_SLICES


def _same_group(shape, row_group, col_group):
    rg = lax.broadcasted_iota(jnp.int32, shape, 0) >> (row_group.bit_length() - 1)
    cg = lax.broadcasted_iota(jnp.int32, shape, 1) >> (col_group.bit_length() - 1)
    return rg == cg


def _prepare_kernel(a_ref, bb_ref, cre_ref, cim_ref, t_state_ref, wv_ref, wy_ref):
    r = S5_BLOCK
    n_st = S5_SLICE_STATES
    mask_v = _same_group((LANES, n_st), SSM_GROUP, SSM_STATE)
    mask_u = _same_group((LANES, LANES), SSM_GROUP, SSM_GROUP)

    def expand(x):
        return jnp.where(mask_v, _dot(x.astype(_BF16), t_state_ref[...]), 0.0)

    def contract(x, y):
        return lax.dot_general(x, y, (((1,), (1,)), ((), ())), precision=_EXACT,
                               preferred_element_type=_F32)

    for s in range(S5_SLICES):
        rows = slice(s * LANES, (s + 1) * LANES)
        a1 = (a_ref[0, rows, :], a_ref[1, rows, :])
        bb = (bb_ref[0, rows, :], bb_ref[1, rows, :])
        cc = (cre_ref[rows, :], cim_ref[rows, :])
        apow = [None, a1]
        for _ in range(r - 1):
            apow.append(_cmul(apow[-1], a1))
        ab = [bb] + [_cmul(apow[k], bb) for k in range(1, r)]
        for m in range(r):
            for part in range(2):
                wv_ref[s, m * LANES:(m + 1) * LANES, part * n_st:(part + 1) * n_st] = (
                    expand(ab[r - 1 - m][part]).astype(_BF16))
        feed = [jnp.where(mask_u, contract(ab[k][0], cc[0]) - contract(ab[k][1], cc[1]), 0.0).astype(_BF16)
                for k in range(r)]
        for i in range(r):
            cols = slice(i * LANES, (i + 1) * LANES)
            ca = _cmul(cc, apow[i + 1])
            wy_ref[s, 0:n_st, cols] = expand(ca[0]).T.astype(_BF16)
            wy_ref[s, n_st:2 * n_st, cols] = (-expand(ca[1])).T.astype(_BF16)
            for m in range(r):
                blk = feed[i - m] if m <= i else jnp.zeros((LANES, LANES), _BF16)
                wy_ref[s, 2 * n_st + m * LANES:2 * n_st + (m + 1) * LANES, cols] = blk


def _s5_matrices(abr, abi, bbr, bbi, c_re, c_im):
    r = S5_BLOCK
    n_rows = SSM_GROUPS * SSM_GROUP
    a_rep = jnp.repeat(jnp.stack([abr, abi], axis=1), SSM_GROUP, axis=2)
    bb = jnp.stack([bbr, bbi]).reshape(2, SSM_GROUP, DEPTH, SSM_GROUPS, SSM_STATE)
    bb = jnp.transpose(bb, (2, 0, 3, 1, 4)).reshape(DEPTH, 2, n_rows, SSM_STATE)
    cre = c_re.reshape(DEPTH, n_rows, SSM_STATE)
    cim = c_im.reshape(DEPTH, n_rows, SSM_STATE)
    t_state = jnp.tile(jnp.eye(SSM_STATE, dtype=_BF16), (1, S5_SLICE_GROUPS))

    def per_layer(a):
        nd = a.ndim - 1
        return pl.BlockSpec((None,) + a.shape[1:], lambda d: (d,) + (0,) * nd)

    v_shape = (DEPTH, S5_SLICES, r * LANES, 2 * S5_SLICE_STATES)
    y_shape = (DEPTH, S5_SLICES, 2 * S5_SLICE_STATES + r * LANES, r * LANES)
    w_v, w_y = pl.pallas_call(
        _prepare_kernel,
        grid=(DEPTH,),
        in_specs=[per_layer(a_rep), per_layer(bb), per_layer(cre), per_layer(cim),
                  pl.BlockSpec(t_state.shape, lambda d: (0, 0))],
        out_specs=(pl.BlockSpec((None,) + v_shape[1:], lambda d: (d, 0, 0, 0)),
                   pl.BlockSpec((None,) + y_shape[1:], lambda d: (d, 0, 0, 0))),
        out_shape=(jax.ShapeDtypeStruct(v_shape, _BF16), jax.ShapeDtypeStruct(y_shape, _BF16)),
        name="s5_prepare",
    )(a_rep, bb, cre, cim, t_state)
    a_pow = (abr, abi)
    for _ in range(r - 1):
        a_pow = _cmul(a_pow, (abr, abi))
    a_blk = jnp.stack(a_pow, axis=1).reshape(DEPTH, 2, SSM_FLAT)
    return w_v, w_y, a_blk


def _tile_rows(ref, i, rows):
    nb, tb, c = ref.shape
    if nb == 1:
        return ref[0, i * rows:(i + 1) * rows, :]
    per = rows // tb
    return ref[i * per:(i + 1) * per].reshape(rows, c)


def _store_tile_rows(ref, i, rows, val):
    nb, tb, c = ref.shape
    if nb == 1:
        ref[0, i * rows:(i + 1) * rows, :] = val
    else:
        per = rows // tb
        ref[i * per:(i + 1) * per] = val.reshape(per, tb, c)


def _load_slabs(ref, i, rows, t_len, seq_rows):
    if seq_rows == t_len:
        return jnp.concatenate([ref[s, i * rows:(i + 1) * rows, :] for s in range(S5_SLICES)], axis=-1)
    per = rows // t_len
    return jnp.concatenate(
        [jnp.concatenate([ref[s, n * seq_rows:n * seq_rows + t_len, :] for s in range(S5_SLICES)], axis=-1)
         for n in range(i * per, (i + 1) * per)], axis=0)


def _store_slabs(ref, i, rows, t_len, seq_rows, val):
    for s in range(S5_SLICES):
        cols = slice(s * LANES, (s + 1) * LANES)
        if seq_rows == t_len:
            ref[s, i * rows:(i + 1) * rows, :] = val[:, cols]
        else:
            per = rows // t_len
            for k in range(per):
                n = i * per + k
                ref[s, n * seq_rows:n * seq_rows + t_len, :] = val[k * t_len:(k + 1) * t_len, cols]


MIXER_INPUTS = 12
MIXER_VECS = ('g_mix', 'd_skip', 'b_glu', 'g_v', 'b_v', 'g_out_s', 'g_out_g')
FFN_VECS = ('g_ffn', 'g_ple', 'g_final')


def _vec_row(ref, names, name, width):
    k = names.index(name)
    return ref[k:k + 1, :width]


def _cast_rows(in_refs, out_refs):
    for src, dst in zip(in_refs, out_refs):
        dst[...] = src[...].astype(_BF16)


def _mixer_kernel(*refs, nseq, t_len, blk_pitch, keep_v, n_cast):
    (x_ref, h0r_ref, h0i_ref, vec_ref, w_in_ref, a_blk_ref, wv_ref, wy_ref,
     w_glu_ref, mixw_ref, mixb_ref, w_out_ref) = refs[:MIXER_INPUTS]
    g_mix, d_skip, b_glu, g_v, b_v, g_out_s, g_out_g = (
        _vec_row(vec_ref, MIXER_VECS, n, D_MODEL if n == 'g_mix' else SSM_WIDTH) for n in MIXER_VECS)
    n_in = MIXER_INPUTS + n_cast
    out_ref, sre_ref, sim_ref = refs[n_in:n_in + 3]
    n_out = 3 + int(keep_v)
    if keep_v:
        vn_ref = refs[n_in + 3]
    _cast_rows(refs[MIXER_INPUTS:n_in], refs[n_in + n_out:n_in + n_out + n_cast])
    us_scr, ys_scr, yg_scr, hb_scr = refs[n_in + n_out + n_cast:]
    rows = nseq * t_len
    n_in_tiles = rows // IN_TILE
    n_out_tiles = rows // OUT_TILE
    n_blk = t_len // S5_BLOCK
    seq_rows = S5_BLOCK * blk_pitch
    n_buf = nseq * blk_pitch

    @pl.when(pl.program_id(0) == 0)
    def _():
        sre_ref[...] = h0r_ref[...]
        sim_ref[...] = h0i_ref[...]
        us_scr[...] = jnp.zeros_like(us_scr)

    row = lax.broadcasted_iota(jnp.int32, (CHUNK, CHUNK), 0)
    col = lax.broadcasted_iota(jnp.int32, (CHUNK, CHUNK), 1)
    t_shift = t_len.bit_length() - 1
    causal = ((row >> t_shift) == (col >> t_shift)) & ((col & (t_len - 1)) <= (row & (t_len - 1)))
    mixw = [jnp.where(causal, mixw_ref[h], 0.0).astype(_BF16) for h in range(GMLP_HEADS)]

    def in_proj(i):
        x = _tile_rows(x_ref, i, IN_TILE)
        a = _rmsnorm(x, g_mix).astype(_BF16)
        return _dot(a, w_in_ref[...])

    z_next = in_proj(0)
    for i in range(n_in_tiles):
        z = z_next
        if i + 1 < n_in_tiles:
            z_next = in_proj(i + 1)
        _store_slabs(us_scr, i, IN_TILE, t_len, seq_rows, z[:, :SSM_WIDTH])
        zg = _gelu(z[:, SSM_WIDTH:])
        ug = zg[:, :GMLP_WIDTH]
        v = zg[:, GMLP_WIDTH:]
        mu = jnp.mean(v, axis=-1, keepdims=True)
        vc = v - mu
        r = lax.rsqrt(jnp.mean(vc * vc, axis=-1, keepdims=True) + EPS)
        vn = vc * r * g_v + b_v
        if keep_v:
            vn_ref[i * IN_TILE:(i + 1) * IN_TILE, :] = vn
        vn = vn.astype(_BF16)
        for b in range(IN_TILE // CHUNK):
            rb = slice(b * CHUNK, (b + 1) * CHUNK)
            mixed = [_dot(mixw[h], vn[rb, h * GMLP_HEAD_DIM:(h + 1) * GMLP_HEAD_DIM])
                     for h in range(GMLP_HEADS)]
            m = jnp.concatenate(mixed, axis=-1) + mixb_ref[...]
            yg = _rmsnorm(ug[rb, :] * m, g_out_g).astype(_BF16)
            yg_scr[i * IN_TILE + b * CHUNK:i * IN_TILE + (b + 1) * CHUNK, :] = yg

    for s in range(S5_SLICES):
        u_blk = jnp.concatenate([us_scr[s, pl.ds(m, n_buf, stride=S5_BLOCK), :] for m in range(S5_BLOCK)],
                                axis=-1).astype(_BF16)
        v_in = _dot(u_blk, wv_ref[s])
        for c in range(2 * S5_SLABS):
            hb_scr[c] = v_in[:, c * LANES:(c + 1) * LANES]

        s0 = s * S5_SLICE_STATES
        a_re = [jnp.broadcast_to(a_blk_ref[0:1, s0 + c * LANES:s0 + (c + 1) * LANES], (nseq, LANES))
                for c in range(S5_SLABS)]
        a_im = [jnp.broadcast_to(a_blk_ref[1:2, s0 + c * LANES:s0 + (c + 1) * LANES], (nseq, LANES))
                for c in range(S5_SLABS)]
        h_init = tuple(
            (sre_ref[:, s0 + c * LANES:s0 + (c + 1) * LANES], sim_ref[:, s0 + c * LANES:s0 + (c + 1) * LANES])
            for c in range(S5_SLABS))

        def scan_step(j, h):
            new = []
            for c in range(S5_SLABS):
                rows_j = pl.ds(j, nseq, stride=blk_pitch)
                hr, hi = h[c]
                v_re = hb_scr[c, rows_j, :]
                v_im = hb_scr[S5_SLABS + c, rows_j, :]
                hb_scr[c, rows_j, :] = hr
                hb_scr[S5_SLABS + c, rows_j, :] = hi
                new.append((a_re[c] * hr - a_im[c] * hi + v_re, a_re[c] * hi + a_im[c] * hr + v_im))
            return tuple(new)
        if n_blk == 1:
            h_fin = scan_step(0, h_init)
        else:
            h_fin = lax.fori_loop(0, n_blk, scan_step, h_init, unroll=SCAN_UNROLL)
        for c in range(S5_SLABS):
            sre_ref[:, s0 + c * LANES:s0 + (c + 1) * LANES] = h_fin[c][0]
            sim_ref[:, s0 + c * LANES:s0 + (c + 1) * LANES] = h_fin[c][1]

        lhs = jnp.concatenate([hb_scr[c].astype(_BF16) for c in range(2 * S5_SLABS)] + [u_blk], axis=-1)
        y_blk = _dot(lhs, wy_ref[s])
        for i in range(S5_BLOCK):
            ys_scr[s, pl.ds(i, n_buf, stride=S5_BLOCK), :] = y_blk[:, i * LANES:(i + 1) * LANES]

    def glu_in(i):
        ys = (_load_slabs(ys_scr, i, OUT_TILE, t_len, seq_rows)
              + d_skip * _load_slabs(us_scr, i, OUT_TILE, t_len, seq_rows))
        ys = _gelu(ys)
        return ys, _dot(ys.astype(_BF16), w_glu_ref[...])

    nxt = glu_in(0)
    for i in range(n_out_tiles):
        ys, gate = nxt
        if i + 1 < n_out_tiles:
            nxt = glu_in(i + 1)
        ys = ys * jax.nn.sigmoid(gate + b_glu)
        ys = _rmsnorm(ys, g_out_s).astype(_BF16)
        y = (_dot(ys, w_out_ref[:SSM_WIDTH, :])
             + _dot(yg_scr[i * OUT_TILE:(i + 1) * OUT_TILE, :], w_out_ref[SSM_WIDTH:, :]))
        _store_tile_rows(out_ref, i, OUT_TILE, _tile_rows(x_ref, i, OUT_TILE) + y)


def _layer_spec(arr, layer):
    nd = arr.ndim - 1
    return pl.BlockSpec((None,) + arr.shape[1:], lambda j: (layer,) + (0,) * nd,
                        pipeline_mode=pl.Buffered(1))


MIXER_WEIGHTS = ('mixer_vecs', 'w_in', 'a_blk', 'w_v', 'w_y', 'w_glu', 'mixw', 'mixb', 'w_out')
FFN_WEIGHTS = ('ffn_vecs', 'w_up', 'w_down', 'w_ple_gate', 'w_ple')


def _weights(w, names, layer):
    picked = [w[k] if isinstance(w[k], tuple) else (w[k], layer) for k in names]
    return [a for a, _ in picked], [_layer_spec(a, l) for a, l in picked]


def _cast_specs(cast, steps, blk_of):
    arrs, in_specs, out_specs, out_shape = [], [], [], []
    for a, layer in cast:
        _, r, c = a.shape
        rb = r // steps
        assert rb * steps == r and rb % (2 * SUBLANES) == 0
        arrs.append(a)
        in_specs.append(pl.BlockSpec((None, rb, c), lambda j, layer=layer: (layer, blk_of(j), 0)))
        out_specs.append(pl.BlockSpec((rb, c), lambda j: (blk_of(j), 0)))
        out_shape.append(jax.ShapeDtypeStruct((r, c), _BF16))
    return arrs, in_specs, out_specs, out_shape


def _mixer(x, h0r, h0i, w, layer, *, h0_layer, nseq, t_len, blk_pitch, keep_v, cast=()):
    nb, length, _ = x.shape
    t_blk = t_len if nb == nseq else nseq * t_len
    rows = nseq * t_len
    steps = length // t_blk
    slab_rows = nseq * S5_BLOCK * blk_pitch
    x_spec = pl.BlockSpec((nb, t_blk, D_MODEL), lambda j: (0, j, 0))
    st_spec = pl.BlockSpec((nseq, SSM_FLAT), lambda j: (0, 0))
    names = [k + ('_s' if keep_v else '_p') if k in ('mixw', 'mixb') else k for k in MIXER_WEIGHTS]
    consts, const_specs = _weights(w, names, layer)
    assert len(consts) + 3 == MIXER_INPUTS
    cast_arrs, cast_in, cast_out, cast_shape = _cast_specs(cast, steps, lambda j: j)
    out_specs = [x_spec, st_spec, st_spec]
    out_shape = [jax.ShapeDtypeStruct(x.shape, _F32),
                 jax.ShapeDtypeStruct((nseq, SSM_FLAT), _F32),
                 jax.ShapeDtypeStruct((nseq, SSM_FLAT), _F32)]
    scratch = [pltpu.VMEM((S5_SLICES, slab_rows, LANES), _F32),
               pltpu.VMEM((S5_SLICES, slab_rows, LANES), _F32),
               pltpu.VMEM((rows, GMLP_WIDTH), _BF16),
               pltpu.VMEM((2 * S5_SLABS, nseq * blk_pitch, LANES), _F32)]
    if keep_v:
        out_specs.append(pl.BlockSpec((rows, GMLP_WIDTH), lambda j: (j, 0)))
        out_shape.append(jax.ShapeDtypeStruct((steps * rows, GMLP_WIDTH), _F32))
    res = pl.pallas_call(
        functools.partial(_mixer_kernel, nseq=nseq, t_len=t_len, blk_pitch=blk_pitch, keep_v=keep_v,
                          n_cast=len(cast_arrs)),
        grid=(steps,),
        in_specs=([x_spec, _layer_spec(h0r, h0_layer), _layer_spec(h0i, h0_layer)] + const_specs + cast_in),
        out_specs=tuple(out_specs + cast_out),
        out_shape=tuple(out_shape + cast_shape),
        scratch_shapes=scratch,
        compiler_params=pltpu.CompilerParams(dimension_semantics=("arbitrary",),
                                             vmem_limit_bytes=MIXER_VMEM_BYTES),
        name="mixer_sample" if keep_v else "mixer_prompt",
    )(x, h0r, h0i, *consts, *cast_arrs)
    n_out = 3 + int(keep_v)
    out, sre, sim = res[:3]
    return out, sre, sim, (res[3] if keep_v else None), [(a[None], 0) for a in res[n_out:]]


FFN_INPUTS = 9


def _ffn_kernel(*refs, final, prompt_steps, n_cast):
    hp_ref, hs_ref, pp_ref, ps_ref, vec_ref, w_up_ref, w_down_ref, w_gate_ref, w_ple_ref = refs[:FFN_INPUTS]
    n_in = FFN_INPUTS + n_cast
    outp_ref, outs_ref = refs[n_in:n_in + 2]
    _cast_rows(refs[FFN_INPUTS:n_in], refs[n_in + 2:])
    g_ffn, g_ple, g_fin = (_vec_row(vec_ref, FFN_VECS, n, D_MODEL) for n in FFN_VECS)

    def rows_block(h_ref, p_ref, out_ref):
        h = h_ref[...]
        f = _rmsnorm(h, g_ffn).astype(_BF16)
        acc = h
        for c in range(D_FF // FFN_COLS):
            up = _dot(f, w_up_ref[:, c * FFN_COLS:(c + 1) * FFN_COLS])
            act = jnp.square(jnp.maximum(up, 0.0)).astype(_BF16)
            acc = acc + _dot(act, w_down_ref[c * FFN_COLS:(c + 1) * FFN_COLS, :])
        gate = jax.nn.sigmoid(_dot(_rmsnorm(acc, g_ple).astype(_BF16), w_gate_ref[...]))
        out = acc + gate * _dot(p_ref[...].astype(_BF16), w_ple_ref[...])
        if final:
            out = _rmsnorm(out, g_fin)
        out_ref[...] = out

    is_prompt = pl.program_id(0) < prompt_steps
    pl.when(is_prompt)(functools.partial(rows_block, hp_ref, pp_ref, outp_ref))
    pl.when(jnp.logical_not(is_prompt))(functools.partial(rows_block, hs_ref, ps_ref, outs_ref))


def _ffn(h_p, h_s, p_p, p_s, w, layer, *, final, cast=()):
    steps_p = h_p.shape[0] // FFN_ROWS
    steps_s = h_s.shape[0] // FFN_ROWS
    consts, const_specs = _weights(w, FFN_WEIGHTS, layer)
    assert len(consts) + 4 == FFN_INPUTS
    prompt_blk = lambda j: jnp.minimum(j, steps_p - 1)
    sample_blk = lambda j: jnp.maximum(j - steps_p, 0)
    cast_arrs, cast_in, cast_out, cast_shape = _cast_specs(cast, steps_p, prompt_blk)
    res = pl.pallas_call(
        functools.partial(_ffn_kernel, final=final, prompt_steps=steps_p, n_cast=len(cast_arrs)),
        grid=(steps_p + steps_s,),
        in_specs=([pl.BlockSpec((FFN_ROWS, D_MODEL), lambda j: (prompt_blk(j), 0)),
                   pl.BlockSpec((FFN_ROWS, D_MODEL), lambda j: (sample_blk(j), 0),
                                pipeline_mode=pl.Buffered(1)),
                   pl.BlockSpec((None, FFN_ROWS, PLE_DIM), lambda j: (layer, prompt_blk(j), 0)),
                   pl.BlockSpec((None, FFN_ROWS, PLE_DIM), lambda j: (layer, sample_blk(j), 0),
                                pipeline_mode=pl.Buffered(1))]
                  + const_specs + cast_in),
        out_specs=tuple([pl.BlockSpec((FFN_ROWS, D_MODEL), lambda j: (prompt_blk(j), 0)),
                         pl.BlockSpec((FFN_ROWS, D_MODEL), lambda j: (sample_blk(j), 0))] + cast_out),
        out_shape=tuple([jax.ShapeDtypeStruct(h_p.shape, _F32), jax.ShapeDtypeStruct(h_s.shape, _F32)]
                        + cast_shape),
        compiler_params=pltpu.CompilerParams(dimension_semantics=("arbitrary",),
                                             vmem_limit_bytes=FFN_VMEM_BYTES),
        name="ffn",
    )(h_p, h_s, p_p, p_s, *consts, *cast_arrs)
    return res[0], res[1], [(a[None], 0) for a in res[2:]]


def kernel(x_prompt, x_sample, state_ssm_re, state_ssm_im, p_prompt, p_sample, g_mix, w_in, lam_re, lam_im, log_dt, b_re, b_im, c_re, c_im, d_skip, w_glu, b_glu, g_v, b_v, w_s, b_s, g_out_s, g_out_g, w_out, g_ffn, w_up, w_down, g_ple, w_ple_gate, w_ple, g_final):
    batch, seq, _ = x_prompt.shape
    dec_batch, dec_seq, _ = x_sample.shape
    n_p, n_s = batch * seq, dec_batch * dec_seq
    assert seq % CHUNK == 0 and batch == SUBLANES
    assert dec_seq == S5_BLOCK and CHUNK % dec_seq == 0 and n_s % max(IN_TILE, OUT_TILE) == 0

    abr, abi, bbr, bbi = _discretise(lam_re, lam_im, log_dt, b_re, b_im)
    w_v, w_y, a_blk = _s5_matrices(abr, abi, bbr, bbi, c_re, c_im)

    def pack(vecs):
        return jnp.stack([jnp.pad(v, ((0, 0), (0, D_MODEL - v.shape[-1]))) for v in vecs], axis=1)
    head_of = (jnp.arange(GMLP_WIDTH)[None, :] // GMLP_HEAD_DIM == jnp.arange(GMLP_HEADS)[:, None]).astype(_F32)
    pos_of = (jnp.arange(CHUNK)[:, None] % dec_seq == jnp.arange(dec_seq)[None, :]).astype(_F32)
    w_s4 = w_s[:, :, :dec_seq, :dec_seq]
    w = dict(
        mixer_vecs=pack([g_mix, d_skip, b_glu, g_v, b_v, g_out_s, g_out_g]),
        ffn_vecs=pack([g_ffn, g_ple, jnp.broadcast_to(g_final, (DEPTH, D_MODEL))]),
        a_blk=a_blk, w_v=w_v, w_y=w_y,
        mixw_p=w_s,
        mixb_p=jnp.einsum('dht,hc->dtc', b_s, head_of, precision=_EXACT),
        mixw_s=jnp.einsum('rt,dhts,cs->dhrc', pos_of, w_s4, pos_of, precision=_EXACT),
        mixb_s=jnp.einsum('dht,rt,hc->drc', b_s[:, :, :dec_seq], pos_of, head_of, precision=_EXACT))

    zeros = jnp.zeros((1, batch, SSM_FLAT), _F32)
    s0_re = state_ssm_re.reshape(DEPTH, dec_batch, SSM_FLAT)
    s0_im = state_ssm_im.reshape(DEPTH, dec_batch, SSM_FLAT)
    pp = p_prompt.reshape(DEPTH, n_p, PLE_DIM)
    ps = p_sample.reshape(DEPTH, n_s, PLE_DIM)
    h_p = x_prompt
    h_s = x_sample.reshape(1, n_s, D_MODEL)
    re_p, im_p, re_s, im_s, v_s = [], [], [], [], []
    mixer_mats = dict(w_in=w_in, w_glu=w_glu, w_out=w_out)
    ffn_mats = dict(w_up=w_up, w_down=w_down, w_ple_gate=w_ple_gate, w_ple=w_ple)
    w.update({k: (a[:1].astype(_BF16), 0) for k, a in mixer_mats.items()})
    for i in range(DEPTH):
        final = i == DEPTH - 1
        h_p, sre, sim, _, cast = _mixer(h_p, zeros, zeros, w, i, h0_layer=0, nseq=batch, t_len=CHUNK,
                                        blk_pitch=CHUNK // S5_BLOCK + SCAN_PAD_BLOCKS, keep_v=False,
                                        cast=[(a, i) for a in ffn_mats.values()])
        w.update(zip(ffn_mats, cast))
        re_p.append(sre)
        im_p.append(sim)
        h_s, sre, sim, vn, _ = _mixer(h_s, s0_re, s0_im, w, i, h0_layer=i, nseq=dec_batch, t_len=dec_seq,
                                      blk_pitch=1, keep_v=True)
        re_s.append(sre)
        im_s.append(sim)
        v_s.append(vn)
        h_p, h_s, cast = _ffn(h_p.reshape(n_p, D_MODEL), h_s.reshape(n_s, D_MODEL), pp, ps, w, i, final=final,
                              cast=[] if final else [(a, i + 1) for a in mixer_mats.values()])
        w.update(zip(mixer_mats, cast))
        h_p = h_p.reshape(batch, seq, D_MODEL)
        h_s = h_s.reshape(1, n_s, D_MODEL)

    st_p = (DEPTH, batch, SSM_GROUPS, SSM_STATE)
    st_s = (DEPTH, dec_batch, SSM_GROUPS, SSM_STATE)
    return (h_p, h_s.reshape(dec_batch, dec_seq, D_MODEL),
            jnp.stack(re_p).reshape(st_p), jnp.stack(im_p).reshape(st_p),
            jnp.stack(re_s).reshape(st_s), jnp.stack(im_s).reshape(st_s),
            jnp.stack(v_s).reshape(DEPTH, dec_batch, dec_seq, GMLP_WIDTH))
```

```python
import functools
import math

import jax
import jax.numpy as jnp
from jax import lax
from jax.experimental import pallas as pl
from jax.experimental.pallas import tpu as pltpu

D_MODEL = 1024
DEPTH = 2
SSM_WIDTH = 512
GMLP_WIDTH = 512
SSM_GROUP = 16
SSM_GROUPS = 32
SSM_STATE = 64
SSM_FLAT = SSM_GROUPS * SSM_STATE
CHUNK = 128
GMLP_HEADS = 4
GMLP_HEAD_DIM = 128
PLE_DIM = 256
D_FF = 4096
IN_WIDTH = SSM_WIDTH + 2 * GMLP_WIDTH
EPS = 1e-6
LAM_RE_MAX = -1e-4

LANES = 128
SUBLANES = 8
S5_SLICES = SSM_WIDTH // LANES
S5_SLICE_STATES = SSM_FLAT // S5_SLICES
S5_SLABS = S5_SLICE_STATES // LANES
S5_BLOCK = 4
SCAN_PAD_BLOCKS = 4
IN_TILE = 128
OUT_TILE = 256
FFN_ROWS = 512
FFN_COLS = 1024
MIXER_VMEM_BYTES = 56 * 1024 * 1024
FFN_VMEM_BYTES = 52 * 1024 * 1024

_F32 = jnp.float32
_BF16 = jnp.bfloat16
_EXACT = lax.Precision.HIGHEST


def _dot(a, b):
    return jnp.dot(a, b, preferred_element_type=_F32)


def _rmsnorm(x, g):
    r = lax.rsqrt(jnp.mean(x * x, axis=-1, keepdims=True) + EPS)
    return x * r * g


def _gelu(x):
    c = math.sqrt(2.0 / math.pi)
    return x * (0.5 * (1.0 + jnp.tanh(c * (x + 0.044715 * (x * x * x)))))


def _disc_kernel(lr_ref, li_ref, ldt_ref, br_ref, bi_ref, abr_ref, abi_ref, bbr_ref, bbi_ref):
    lr = jnp.minimum(lr_ref[...], LAM_RE_MAX)
    li = li_ref[...]
    dt = jnp.exp(ldt_ref[...])
    mag = jnp.exp(lr * dt)
    abr = mag * jnp.cos(li * dt)
    abi = mag * jnp.sin(li * dt)
    den = lr * lr + li * li
    nr = abr - 1.0
    ni = abi
    qr = (nr * lr + ni * li) / den
    qi = (ni * lr - nr * li) / den
    abr_ref[...] = abr
    abi_ref[...] = abi
    for h in range(SSM_GROUP):
        br = br_ref[h]
        bi = bi_ref[h]
        bbr_ref[h] = qr * br - qi * bi
        bbi_ref[h] = qr * bi + qi * br


def _discretise(lam_re, lam_im, log_dt, b_re, b_im):
    dg = DEPTH * SSM_GROUPS
    lr = lam_re.reshape(dg, SSM_STATE)
    li = lam_im.reshape(dg, SSM_STATE)
    ldt = log_dt.reshape(dg, 1)
    br = jnp.moveaxis(b_re, -1, 0).reshape(SSM_GROUP, dg, SSM_STATE)
    bi = jnp.moveaxis(b_im, -1, 0).reshape(SSM_GROUP, dg, SSM_STATE)
    vec = jax.ShapeDtypeStruct((dg, SSM_STATE), _F32)
    mat = jax.ShapeDtypeStruct((SSM_GROUP, dg, SSM_STATE), _F32)
    abr, abi, bbr, bbi = pl.pallas_call(
        _disc_kernel, out_shape=(vec, vec, mat, mat), name="s5_discretise")(lr, li, ldt, br, bi)
    vshp = (DEPTH, SSM_GROUPS, SSM_STATE)
    return abr.reshape(vshp), abi.reshape(vshp), bbr, bbi


def _cmul(a, b):
    return a[0] * b[0] - a[1] * b[1], a[0] * b[1] + a[1] * b[0]


S5_SLICE_GROUPS = SSM_GROUPS // S5_SLICES


def _same_group(shape, row_group, col_group):
    rg = lax.broadcasted_iota(jnp.int32, shape, 0) >> (row_group.bit_length() - 1)
    cg = lax.broadcasted_iota(jnp.int32, shape, 1) >> (col_group.bit_length() - 1)
    return rg == cg


PREPARE_INPUTS = 5


def _prepare_kernel(*refs):
    a_ref, bb_ref, cre_ref, cim_ref, t_state_ref = refs[:PREPARE_INPUTS]
    n_cast = (len(refs) - PREPARE_INPUTS - 2) // 2
    wv_ref, wy_ref = refs[PREPARE_INPUTS + n_cast:PREPARE_INPUTS + n_cast + 2]
    _cast_rows(refs[PREPARE_INPUTS:PREPARE_INPUTS + n_cast], refs[PREPARE_INPUTS + n_cast + 2:])
    r = S5_BLOCK
    n_st = S5_SLICE_STATES
    mask_v = _same_group((LANES, n_st), SSM_GROUP, SSM_STATE)
    mask_u = _same_group((LANES, LANES), SSM_GROUP, SSM_GROUP)

    def expand(x):
        return jnp.where(mask_v, _dot(x.astype(_BF16), t_state_ref[...]), 0.0)

    def contract(x, y):
        return lax.dot_general(x, y, (((1,), (1,)), ((), ())), precision=_EXACT,
                               preferred_element_type=_F32)

    for s in range(S5_SLICES):
        rows = slice(s * LANES, (s + 1) * LANES)
        a1 = (a_ref[0, rows, :], a_ref[1, rows, :])
        bb = (bb_ref[0, rows, :], bb_ref[1, rows, :])
        cc = (cre_ref[rows, :], cim_ref[rows, :])
        apow = [None, a1]
        for _ in range(r - 1):
            apow.append(_cmul(apow[-1], a1))
        ab = [bb] + [_cmul(apow[k], bb) for k in range(1, r)]
        for m in range(r):
            for part in range(2):
                wv_ref[s, m * LANES:(m + 1) * LANES, part * n_st:(part + 1) * n_st] = (
                    expand(ab[r - 1 - m][part]).astype(_BF16))
        feed = [jnp.where(mask_u, contract(ab[k][0], cc[0]) - contract(ab[k][1], cc[1]), 0.0).astype(_BF16)
                for k in range(r)]
        for i in range(r):
            cols = slice(i * LANES, (i + 1) * LANES)
            ca = _cmul(cc, apow[i + 1])
            wy_ref[s, 0:n_st, cols] = expand(ca[0]).T.astype(_BF16)
            wy_ref[s, n_st:2 * n_st, cols] = (-expand(ca[1])).T.astype(_BF16)
            for m in range(r):
                blk = feed[i - m] if m <= i else jnp.zeros((LANES, LANES), _BF16)
                wy_ref[s, 2 * n_st + m * LANES:2 * n_st + (m + 1) * LANES, cols] = blk


def _s5_matrices(abr, abi, bbr, bbi, c_re, c_im, cast=()):
    r = S5_BLOCK
    n_rows = SSM_GROUPS * SSM_GROUP
    a_rep = jnp.repeat(jnp.stack([abr, abi], axis=1), SSM_GROUP, axis=2)
    bb = jnp.stack([bbr, bbi]).reshape(2, SSM_GROUP, DEPTH, SSM_GROUPS, SSM_STATE)
    bb = jnp.transpose(bb, (2, 0, 3, 1, 4)).reshape(DEPTH, 2, n_rows, SSM_STATE)
    cre = c_re.reshape(DEPTH, n_rows, SSM_STATE)
    cim = c_im.reshape(DEPTH, n_rows, SSM_STATE)
    t_state = jnp.tile(jnp.eye(SSM_STATE, dtype=_BF16), (1, S5_SLICE_GROUPS))

    def per_layer(a):
        nd = a.ndim - 1
        return pl.BlockSpec((None,) + a.shape[1:], lambda d: (d,) + (0,) * nd)

    v_shape = (DEPTH, S5_SLICES, r * LANES, 2 * S5_SLICE_STATES)
    y_shape = (DEPTH, S5_SLICES, 2 * S5_SLICE_STATES + r * LANES, r * LANES)
    cast_arrs, cast_in, cast_out, cast_shape = _cast_specs(cast, DEPTH, lambda d: d)
    w_v, w_y, *cast_res = pl.pallas_call(
        _prepare_kernel,
        grid=(DEPTH,),
        in_specs=[per_layer(a_rep), per_layer(bb), per_layer(cre), per_layer(cim),
                  pl.BlockSpec(t_state.shape, lambda d: (0, 0))] + cast_in,
        out_specs=tuple([pl.BlockSpec((None,) + v_shape[1:], lambda d: (d, 0, 0, 0)),
                         pl.BlockSpec((None,) + y_shape[1:], lambda d: (d, 0, 0, 0))] + cast_out),
        out_shape=tuple([jax.ShapeDtypeStruct(v_shape, _BF16), jax.ShapeDtypeStruct(y_shape, _BF16)]
                        + cast_shape),
        name="s5_prepare",
    )(a_rep, bb, cre, cim, t_state, *cast_arrs)
    a_pow = (abr, abi)
    for _ in range(r - 1):
        a_pow = _cmul(a_pow, (abr, abi))
    a_blk = jnp.stack(a_pow, axis=1).reshape(DEPTH, 2, SSM_FLAT)
    return w_v, w_y, a_blk, [(a[None], 0) for a in cast_res]


def _tile_rows(ref, i, rows):
    nb, tb, c = ref.shape
    if nb == 1:
        return ref[0, i * rows:(i + 1) * rows, :]
    per = rows // tb
    return ref[i * per:(i + 1) * per].reshape(rows, c)


def _store_tile_rows(ref, i, rows, val):
    nb, tb, c = ref.shape
    if nb == 1:
        ref[0, i * rows:(i + 1) * rows, :] = val
    else:
        per = rows // tb
        ref[i * per:(i + 1) * per] = val.reshape(per, tb, c)


def _load_slabs(ref, i, rows, t_len, seq_rows):
    if seq_rows == t_len:
        return jnp.concatenate([ref[s, i * rows:(i + 1) * rows, :] for s in range(S5_SLICES)], axis=-1)
    per = rows // t_len
    return jnp.concatenate(
        [jnp.concatenate([ref[s, n * seq_rows:n * seq_rows + t_len, :] for s in range(S5_SLICES)], axis=-1)
         for n in range(i * per, (i + 1) * per)], axis=0)


def _store_slabs(ref, i, rows, t_len, seq_rows, val):
    for s in range(S5_SLICES):
        cols = slice(s * LANES, (s + 1) * LANES)
        if seq_rows == t_len:
            ref[s, i * rows:(i + 1) * rows, :] = val[:, cols]
        else:
            per = rows // t_len
            for k in range(per):
                n = i * per + k
                ref[s, n * seq_rows:n * seq_rows + t_len, :] = val[k * t_len:(k + 1) * t_len, cols]


MIXER_INPUTS = 12
MIXER_VECS = ('g_mix', 'd_skip', 'b_glu', 'g_v', 'b_v', 'g_out_s', 'g_out_g')
FFN_VECS = ('g_ffn', 'g_ple', 'g_final')


def _vec_row(ref, names, name, width):
    k = names.index(name)
    return ref[k:k + 1, :width]


def _cast_rows(in_refs, out_refs):
    for src, dst in zip(in_refs, out_refs):
        dst[...] = src[...].astype(_BF16)


def _mixer_kernel(*refs, nseq, t_len, blk_pitch, keep_v, n_cast):
    (x_ref, h0r_ref, h0i_ref, vec_ref, w_in_ref, a_blk_ref, wv_ref, wy_ref,
     w_glu_ref, mixw_ref, mixb_ref, w_out_ref) = refs[:MIXER_INPUTS]
    g_mix, d_skip, b_glu, g_v, b_v, g_out_s, g_out_g = (
        _vec_row(vec_ref, MIXER_VECS, n, D_MODEL if n == 'g_mix' else SSM_WIDTH) for n in MIXER_VECS)
    n_in = MIXER_INPUTS + n_cast
    out_ref, sre_ref, sim_ref = refs[n_in:n_in + 3]
    n_out = 3 + int(keep_v)
    if keep_v:
        vn_ref = refs[n_in + 3]
    _cast_rows(refs[MIXER_INPUTS:n_in], refs[n_in + n_out:n_in + n_out + n_cast])
    us_scr, ys_scr, yg_scr, hb2_scr = refs[n_in + n_out + n_cast:]
    rows = nseq * t_len
    n_in_tiles = rows // IN_TILE
    n_out_tiles = rows // OUT_TILE
    n_blk = t_len // S5_BLOCK
    seq_rows = S5_BLOCK * blk_pitch
    n_buf = nseq * blk_pitch

    @pl.when(pl.program_id(0) == 0)
    def _():
        sre_ref[...] = h0r_ref[...]
        sim_ref[...] = h0i_ref[...]
        us_scr[...] = jnp.zeros_like(us_scr)

    row = lax.broadcasted_iota(jnp.int32, (CHUNK, CHUNK), 0)
    col = lax.broadcasted_iota(jnp.int32, (CHUNK, CHUNK), 1)
    t_shift = t_len.bit_length() - 1
    causal = ((row >> t_shift) == (col >> t_shift)) & ((col & (t_len - 1)) <= (row & (t_len - 1)))
    mixw = [jnp.where(causal, mixw_ref[h], 0.0).astype(_BF16) for h in range(GMLP_HEADS)]

    def in_proj(i):
        x = _tile_rows(x_ref, i, IN_TILE)
        a = _rmsnorm(x, g_mix).astype(_BF16)
        return _dot(a, w_in_ref[...])

    z_next = in_proj(0)
    for i in range(n_in_tiles):
        z = z_next
        if i + 1 < n_in_tiles:
            z_next = in_proj(i + 1)
        _store_slabs(us_scr, i, IN_TILE, t_len, seq_rows, z[:, :SSM_WIDTH])
        zg = _gelu(z[:, SSM_WIDTH:])
        ug = zg[:, :GMLP_WIDTH]
        v = zg[:, GMLP_WIDTH:]
        mu = jnp.mean(v, axis=-1, keepdims=True)
        vc = v - mu
        r = lax.rsqrt(jnp.mean(vc * vc, axis=-1, keepdims=True) + EPS)
        vn = vc * r * g_v + b_v
        if keep_v:
            vn_ref[i * IN_TILE:(i + 1) * IN_TILE, :] = vn
        vn = vn.astype(_BF16)
        for b in range(IN_TILE // CHUNK):
            rb = slice(b * CHUNK, (b + 1) * CHUNK)
            mixed = [_dot(mixw[h], vn[rb, h * GMLP_HEAD_DIM:(h + 1) * GMLP_HEAD_DIM])
                     for h in range(GMLP_HEADS)]
            m = jnp.concatenate(mixed, axis=-1) + mixb_ref[...]
            yg = _rmsnorm(ug[rb, :] * m, g_out_g).astype(_BF16)
            yg_scr[i * IN_TILE + b * CHUNK:i * IN_TILE + (b + 1) * CHUNK, :] = yg

    def block_inputs(s):
        u_blk = jnp.concatenate([us_scr[s, pl.ds(m, n_buf, stride=S5_BLOCK), :] for m in range(S5_BLOCK)],
                                axis=-1).astype(_BF16)
        v_in = _dot(u_blk, wv_ref[s])
        for c in range(2 * S5_SLABS):
            hb2_scr[s % 2, c] = v_in[:, c * LANES:(c + 1) * LANES]
        return u_blk

    u_next = block_inputs(0)
    for s in range(S5_SLICES):
        u_blk = u_next
        if s + 1 < S5_SLICES:
            u_next = block_inputs(s + 1)
        hb_scr = hb2_scr.at[s % 2]
        s0 = s * S5_SLICE_STATES
        a_re = [jnp.broadcast_to(a_blk_ref[0:1, s0 + c * LANES:s0 + (c + 1) * LANES], (nseq, LANES))
                for c in range(S5_SLABS)]
        a_im = [jnp.broadcast_to(a_blk_ref[1:2, s0 + c * LANES:s0 + (c + 1) * LANES], (nseq, LANES))
                for c in range(S5_SLABS)]
        h_init = tuple(
            (sre_ref[:, s0 + c * LANES:s0 + (c + 1) * LANES], sim_ref[:, s0 + c * LANES:s0 + (c + 1) * LANES])
            for c in range(S5_SLABS))

        def scan_step(j, h):
            new = []
            for c in range(S5_SLABS):
                rows_j = pl.ds(j, nseq, stride=blk_pitch)
                hr, hi = h[c]
                v_re = hb_scr[c, rows_j, :]
                v_im = hb_scr[S5_SLABS + c, rows_j, :]
                hb_scr[c, rows_j, :] = hr
                hb_scr[S5_SLABS + c, rows_j, :] = hi
                new.append((a_re[c] * hr - a_im[c] * hi + v_re, a_re[c] * hi + a_im[c] * hr + v_im))
            return tuple(new)
        h_fin = h_init
        for j in range(n_blk):
            h_fin = scan_step(j, h_fin)
        for c in range(S5_SLABS):
            sre_ref[:, s0 + c * LANES:s0 + (c + 1) * LANES] = h_fin[c][0]
            sim_ref[:, s0 + c * LANES:s0 + (c + 1) * LANES] = h_fin[c][1]

        lhs = jnp.concatenate([hb_scr[c].astype(_BF16) for c in range(2 * S5_SLABS)] + [u_blk], axis=-1)
        y_blk = _dot(lhs, wy_ref[s])
        for i in range(S5_BLOCK):
            ys_scr[s, pl.ds(i, n_buf, stride=S5_BLOCK), :] = y_blk[:, i * LANES:(i + 1) * LANES]

    def glu_in(i):
        ys = (_load_slabs(ys_scr, i, OUT_TILE, t_len, seq_rows)
              + d_skip * _load_slabs(us_scr, i, OUT_TILE, t_len, seq_rows))
        ys = _gelu(ys)
        return ys, _dot(ys.astype(_BF16), w_glu_ref[...])

    nxt = glu_in(0)
    for i in range(n_out_tiles):
        ys, gate = nxt
        if i + 1 < n_out_tiles:
            nxt = glu_in(i + 1)
        ys = ys * jax.nn.sigmoid(gate + b_glu)
        ys = _rmsnorm(ys, g_out_s).astype(_BF16)
        y = (_dot(ys, w_out_ref[:SSM_WIDTH, :])
             + _dot(yg_scr[i * OUT_TILE:(i + 1) * OUT_TILE, :], w_out_ref[SSM_WIDTH:, :]))
        _store_tile_rows(out_ref, i, OUT_TILE, _tile_rows(x_ref, i, OUT_TILE) + y)


def _layer_spec(arr, layer):
    nd = arr.ndim - 1
    return pl.BlockSpec((None,) + arr.shape[1:], lambda j: (layer,) + (0,) * nd,
                        pipeline_mode=pl.Buffered(1))


MIXER_WEIGHTS = ('mixer_vecs', 'w_in', 'a_blk', 'w_v', 'w_y', 'w_glu', 'mixw', 'mixb', 'w_out')
FFN_WEIGHTS = ('ffn_vecs', 'w_up', 'w_down', 'w_ple_gate', 'w_ple')


def _weights(w, names, layer):
    picked = [w[k] if isinstance(w[k], tuple) else (w[k], layer) for k in names]
    return [a for a, _ in picked], [_layer_spec(a, l) for a, l in picked]


def _cast_specs(cast, steps, blk_of):
    arrs, in_specs, out_specs, out_shape = [], [], [], []
    for a, layer in cast:
        _, r, c = a.shape
        rb = r // steps
        assert rb * steps == r and rb % (2 * SUBLANES) == 0
        arrs.append(a)
        in_specs.append(pl.BlockSpec((None, rb, c), lambda j, layer=layer: (layer, blk_of(j), 0)))
        out_specs.append(pl.BlockSpec((rb, c), lambda j: (blk_of(j), 0)))
        out_shape.append(jax.ShapeDtypeStruct((r, c), _BF16))
    return arrs, in_specs, out_specs, out_shape


def _mixer(x, h0r, h0i, w, layer, *, h0_layer, nseq, t_len, blk_pitch, keep_v, cast=()):
    nb, length, _ = x.shape
    t_blk = t_len if nb == nseq else nseq * t_len
    rows = nseq * t_len
    steps = length // t_blk
    slab_rows = nseq * S5_BLOCK * blk_pitch
    x_spec = pl.BlockSpec((nb, t_blk, D_MODEL), lambda j: (0, j, 0))
    st_spec = pl.BlockSpec((nseq, SSM_FLAT), lambda j: (0, 0))
    names = [k + ('_s' if keep_v else '_p') if k in ('mixw', 'mixb') else k for k in MIXER_WEIGHTS]
    consts, const_specs = _weights(w, names, layer)
    assert len(consts) + 3 == MIXER_INPUTS
    cast_arrs, cast_in, cast_out, cast_shape = _cast_specs(cast, steps, lambda j: j)
    out_specs = [x_spec, st_spec, st_spec]
    out_shape = [jax.ShapeDtypeStruct(x.shape, _F32),
                 jax.ShapeDtypeStruct((nseq, SSM_FLAT), _F32),
                 jax.ShapeDtypeStruct((nseq, SSM_FLAT), _F32)]
    scratch = [pltpu.VMEM((S5_SLICES, slab_rows, LANES), _F32),
               pltpu.VMEM((S5_SLICES, slab_rows, LANES), _F32),
               pltpu.VMEM((rows, GMLP_WIDTH), _BF16),
               pltpu.VMEM((2, 2 * S5_SLABS, nseq * blk_pitch, LANES), _F32)]
    if keep_v:
        out_specs.append(pl.BlockSpec((rows, GMLP_WIDTH), lambda j: (j, 0)))
        out_shape.append(jax.ShapeDtypeStruct((steps * rows, GMLP_WIDTH), _F32))
    res = pl.pallas_call(
        functools.partial(_mixer_kernel, nseq=nseq, t_len=t_len, blk_pitch=blk_pitch, keep_v=keep_v,
                          n_cast=len(cast_arrs)),
        grid=(steps,),
        in_specs=([x_spec, _layer_spec(h0r, h0_layer), _layer_spec(h0i, h0_layer)] + const_specs + cast_in),
        out_specs=tuple(out_specs + cast_out),
        out_shape=tuple(out_shape + cast_shape),
        scratch_shapes=scratch,
        compiler_params=pltpu.CompilerParams(dimension_semantics=("arbitrary",),
                                             vmem_limit_bytes=MIXER_VMEM_BYTES),
        name="mixer_sample" if keep_v else "mixer_prompt",
    )(x, h0r, h0i, *consts, *cast_arrs)
    n_out = 3 + int(keep_v)
    out, sre, sim = res[:3]
    return out, sre, sim, (res[3] if keep_v else None), [(a[None], 0) for a in res[n_out:]]


FFN_INPUTS = 9


def _ffn_kernel(*refs, final, prompt_steps, n_cast):
    hp_ref, hs_ref, pp_ref, ps_ref, vec_ref, w_up_ref, w_down_ref, w_gate_ref, w_ple_ref = refs[:FFN_INPUTS]
    n_in = FFN_INPUTS + n_cast
    outp_ref, outs_ref = refs[n_in:n_in + 2]
    _cast_rows(refs[FFN_INPUTS:n_in], refs[n_in + 2:])
    g_ffn, g_ple, g_fin = (_vec_row(vec_ref, FFN_VECS, n, D_MODEL) for n in FFN_VECS)

    def rows_block(h_ref, p_ref, out_ref):
        h = h_ref[...]
        f = _rmsnorm(h, g_ffn).astype(_BF16)
        acc = h
        for c in range(D_FF // FFN_COLS):
            up = _dot(f, w_up_ref[:, c * FFN_COLS:(c + 1) * FFN_COLS])
            act = jnp.square(jnp.maximum(up, 0.0)).astype(_BF16)
            acc = acc + _dot(act, w_down_ref[c * FFN_COLS:(c + 1) * FFN_COLS, :])
        gate = jax.nn.sigmoid(_dot(_rmsnorm(acc, g_ple).astype(_BF16), w_gate_ref[...]))
        out = acc + gate * _dot(p_ref[...].astype(_BF16), w_ple_ref[...])
        if final:
            out = _rmsnorm(out, g_fin)
        out_ref[...] = out

    is_prompt = pl.program_id(0) < prompt_steps
    pl.when(is_prompt)(functools.partial(rows_block, hp_ref, pp_ref, outp_ref))
    pl.when(jnp.logical_not(is_prompt))(functools.partial(rows_block, hs_ref, ps_ref, outs_ref))


def _ffn(h_p, h_s, p_p, p_s, w, layer, *, final, cast=()):
    steps_p = h_p.shape[0] // FFN_ROWS
    steps_s = h_s.shape[0] // FFN_ROWS
    consts, const_specs = _weights(w, FFN_WEIGHTS, layer)
    assert len(consts) + 4 == FFN_INPUTS
    prompt_blk = lambda j: jnp.minimum(j, steps_p - 1)
    sample_blk = lambda j: jnp.maximum(j - steps_p, 0)
    cast_arrs, cast_in, cast_out, cast_shape = _cast_specs(cast, steps_p, prompt_blk)
    res = pl.pallas_call(
        functools.partial(_ffn_kernel, final=final, prompt_steps=steps_p, n_cast=len(cast_arrs)),
        grid=(steps_p + steps_s,),
        in_specs=([pl.BlockSpec((FFN_ROWS, D_MODEL), lambda j: (prompt_blk(j), 0)),
                   pl.BlockSpec((FFN_ROWS, D_MODEL), lambda j: (sample_blk(j), 0),
                                pipeline_mode=pl.Buffered(1)),
                   pl.BlockSpec((None, FFN_ROWS, PLE_DIM), lambda j: (layer, prompt_blk(j), 0)),
                   pl.BlockSpec((None, FFN_ROWS, PLE_DIM), lambda j: (layer, sample_blk(j), 0),
                                pipeline_mode=pl.Buffered(1))]
                  + const_specs + cast_in),
        out_specs=tuple([pl.BlockSpec((FFN_ROWS, D_MODEL), lambda j: (prompt_blk(j), 0)),
                         pl.BlockSpec((FFN_ROWS, D_MODEL), lambda j: (sample_blk(j), 0))] + cast_out),
        out_shape=tuple([jax.ShapeDtypeStruct(h_p.shape, _F32), jax.ShapeDtypeStruct(h_s.shape, _F32)]
                        + cast_shape),
        compiler_params=pltpu.CompilerParams(dimension_semantics=("arbitrary",),
                                             vmem_limit_bytes=FFN_VMEM_BYTES),
        name="ffn",
    )(h_p, h_s, p_p, p_s, *consts, *cast_arrs)
    return res[0], res[1], [(a[None], 0) for a in res[2:]]


def kernel(x_prompt, x_sample, state_ssm_re, state_ssm_im, p_prompt, p_sample, g_mix, w_in, lam_re, lam_im, log_dt, b_re, b_im, c_re, c_im, d_skip, w_glu, b_glu, g_v, b_v, w_s, b_s, g_out_s, g_out_g, w_out, g_ffn, w_up, w_down, g_ple, w_ple_gate, w_ple, g_final):
    batch, seq, _ = x_prompt.shape
    dec_batch, dec_seq, _ = x_sample.shape
    n_p, n_s = batch * seq, dec_batch * dec_seq
    assert seq % CHUNK == 0 and batch == SUBLANES
    assert dec_seq == S5_BLOCK and CHUNK % dec_seq == 0 and n_s % max(IN_TILE, OUT_TILE) == 0

    mixer_mats = dict(w_in=w_in, w_glu=w_glu, w_out=w_out)
    ffn_mats = dict(w_up=w_up, w_down=w_down, w_ple_gate=w_ple_gate, w_ple=w_ple)
    abr, abi, bbr, bbi = _discretise(lam_re, lam_im, log_dt, b_re, b_im)
    w_v, w_y, a_blk, cast = _s5_matrices(abr, abi, bbr, bbi, c_re, c_im,
                                         cast=[(a, 0) for a in mixer_mats.values()])

    def pack(vecs):
        return jnp.stack([jnp.pad(v, ((0, 0), (0, D_MODEL - v.shape[-1]))) for v in vecs], axis=1)
    head_of = (jnp.arange(GMLP_WIDTH)[None, :] // GMLP_HEAD_DIM == jnp.arange(GMLP_HEADS)[:, None]).astype(_F32)
    pos_of = (jnp.arange(CHUNK)[:, None] % dec_seq == jnp.arange(dec_seq)[None, :]).astype(_F32)
    w_s4 = w_s[:, :, :dec_seq, :dec_seq]
    w = dict(
        mixer_vecs=pack([g_mix, d_skip, b_glu, g_v, b_v, g_out_s, g_out_g]),
        ffn_vecs=pack([g_ffn, g_ple, jnp.broadcast_to(g_final, (DEPTH, D_MODEL))]),
        a_blk=a_blk, w_v=w_v, w_y=w_y,
        mixw_p=w_s,
        mixb_p=jnp.einsum('dht,hc->dtc', b_s, head_of, precision=_EXACT),
        mixw_s=jnp.einsum('rt,dhts,cs->dhrc', pos_of, w_s4, pos_of, precision=_EXACT),
        mixb_s=jnp.einsum('dht,rt,hc->drc', b_s[:, :, :dec_seq], pos_of, head_of, precision=_EXACT))

    zeros = jnp.zeros((1, batch, SSM_FLAT), _F32)
    s0_re = state_ssm_re.reshape(DEPTH, dec_batch, SSM_FLAT)
    s0_im = state_ssm_im.reshape(DEPTH, dec_batch, SSM_FLAT)
    pp = p_prompt.reshape(DEPTH, n_p, PLE_DIM)
    ps = p_sample.reshape(DEPTH, n_s, PLE_DIM)
    h_p = x_prompt
    h_s = x_sample.reshape(1, n_s, D_MODEL)
    re_p, im_p, re_s, im_s, v_s = [], [], [], [], []
    w.update(zip(mixer_mats, cast))
    for i in range(DEPTH):
        final = i == DEPTH - 1
        h_p, sre, sim, _, cast = _mixer(h_p, zeros, zeros, w, i, h0_layer=0, nseq=batch, t_len=CHUNK,
                                        blk_pitch=CHUNK // S5_BLOCK + SCAN_PAD_BLOCKS, keep_v=False,
                                        cast=[(a, i) for a in ffn_mats.values()])
        w.update(zip(ffn_mats, cast))
        re_p.append(sre)
        im_p.append(sim)
        h_s, sre, sim, vn, _ = _mixer(h_s, s0_re, s0_im, w, i, h0_layer=i, nseq=dec_batch, t_len=dec_seq,
                                      blk_pitch=1, keep_v=True)
        re_s.append(sre)
        im_s.append(sim)
        v_s.append(vn)
        h_p, h_s, cast = _ffn(h_p.reshape(n_p, D_MODEL), h_s.reshape(n_s, D_MODEL), pp, ps, w, i, final=final,
                              cast=[] if final else [(a, i + 1) for a in mixer_mats.values()])
        w.update(zip(mixer_mats, cast))
        h_p = h_p.reshape(batch, seq, D_MODEL)
        h_s = h_s.reshape(1, n_s, D_MODEL)

    st_p = (DEPTH, batch, SSM_GROUPS, SSM_STATE)
    st_s = (DEPTH, dec_batch, SSM_GROUPS, SSM_STATE)
    return (h_p, h_s.reshape(dec_batch, dec_seq, D_MODEL),
            jnp.stack(re_p).reshape(st_p), jnp.stack(im_p).reshape(st_p),
            jnp.stack(re_s).reshape(st_s), jnp.stack(im_s).reshape(st_s),
            jnp.stack(v_s).reshape(DEPTH, dec_batch, dec_seq, GMLP_WIDTH))
```

```python
import functools
import math

import jax
import jax.numpy as jnp
from jax import lax
from jax.experimental import pallas as pl
from jax.experimental.pallas import tpu as pltpu

D_MODEL = 1024
DEPTH = 2
SSM_WIDTH = 512
GMLP_WIDTH = 512
SSM_GROUP = 16
SSM_GROUPS = 32
SSM_STATE = 64
SSM_FLAT = SSM_GROUPS * SSM_STATE
CHUNK = 128
GMLP_HEADS = 4
GMLP_HEAD_DIM = 128
PLE_DIM = 256
D_FF = 4096
IN_WIDTH = SSM_WIDTH + 2 * GMLP_WIDTH
EPS = 1e-6
LAM_RE_MAX = -1e-4

LANES = 128
SUBLANES = 8
S5_SLICES = SSM_WIDTH // LANES
S5_SLICE_STATES = SSM_FLAT // S5_SLICES
S5_SLABS = S5_SLICE_STATES // LANES
S5_BLOCK = 4
SCAN_PAD_BLOCKS = 4
IN_TILE = 128
OUT_TILE = 256
FFN_ROWS = 512
FFN_COLS = 1024
FFN_SPLIT = 2
MIXER_VMEM_BYTES = 56 * 1024 * 1024
FFN_VMEM_BYTES = 52 * 1024 * 1024

_F32 = jnp.float32
_BF16 = jnp.bfloat16
_EXACT = lax.Precision.HIGHEST


def _dot(a, b):
    return jnp.dot(a, b, preferred_element_type=_F32)


def _rmsnorm(x, g):
    r = lax.rsqrt(jnp.mean(x * x, axis=-1, keepdims=True) + EPS)
    return x * r * g


def _gelu(x):
    c = math.sqrt(2.0 / math.pi)
    return x * (0.5 * (1.0 + jnp.tanh(c * (x + 0.044715 * (x * x * x)))))


def _disc_kernel(lr_ref, li_ref, ldt_ref, br_ref, bi_ref, abr_ref, abi_ref, bbr_ref, bbi_ref):
    lr = jnp.minimum(lr_ref[...], LAM_RE_MAX)
    li = li_ref[...]
    dt = jnp.exp(ldt_ref[...])
    mag = jnp.exp(lr * dt)
    abr = mag * jnp.cos(li * dt)
    abi = mag * jnp.sin(li * dt)
    den = lr * lr + li * li
    nr = abr - 1.0
    ni = abi
    qr = (nr * lr + ni * li) / den
    qi = (ni * lr - nr * li) / den
    abr_ref[...] = abr
    abi_ref[...] = abi
    for h in range(SSM_GROUP):
        br = br_ref[h]
        bi = bi_ref[h]
        bbr_ref[h] = qr * br - qi * bi
        bbi_ref[h] = qr * bi + qi * br


def _discretise(lam_re, lam_im, log_dt, b_re, b_im):
    dg = DEPTH * SSM_GROUPS
    lr = lam_re.reshape(dg, SSM_STATE)
    li = lam_im.reshape(dg, SSM_STATE)
    ldt = log_dt.reshape(dg, 1)
    br = jnp.moveaxis(b_re, -1, 0).reshape(SSM_GROUP, dg, SSM_STATE)
    bi = jnp.moveaxis(b_im, -1, 0).reshape(SSM_GROUP, dg, SSM_STATE)
    vec = jax.ShapeDtypeStruct((dg, SSM_STATE), _F32)
    mat = jax.ShapeDtypeStruct((SSM_GROUP, dg, SSM_STATE), _F32)
    abr, abi, bbr, bbi = pl.pallas_call(
        _disc_kernel, out_shape=(vec, vec, mat, mat), name="s5_discretise")(lr, li, ldt, br, bi)
    vshp = (DEPTH, SSM_GROUPS, SSM_STATE)
    return abr.reshape(vshp), abi.reshape(vshp), bbr, bbi


def _cmul(a, b):
    return a[0] * b[0] - a[1] * b[1], a[0] * b[1] + a[1] * b[0]


S5_SLICE_GROUPS = SSM_GROUPS // S5_SLICES


def _same_group(shape, row_group, col_group):
    rg = lax.broadcasted_iota(jnp.int32, shape, 0) >> (row_group.bit_length() - 1)
    cg = lax.broadcasted_iota(jnp.int32, shape, 1) >> (col_group.bit_length() - 1)
    return rg == cg


PREPARE_INPUTS = 5


def _prepare_kernel(*refs):
    a_ref, bb_ref, cre_ref, cim_ref, t_state_ref = refs[:PREPARE_INPUTS]
    n_cast = (len(refs) - PREPARE_INPUTS - 2) // 2
    wv_ref, wy_ref = refs[PREPARE_INPUTS + n_cast:PREPARE_INPUTS + n_cast + 2]
    _cast_rows(refs[PREPARE_INPUTS:PREPARE_INPUTS + n_cast], refs[PREPARE_INPUTS + n_cast + 2:])
    r = S5_BLOCK
    n_st = S5_SLICE_STATES
    mask_v = _same_group((LANES, n_st), SSM_GROUP, SSM_STATE)
    mask_u = _same_group((LANES, LANES), SSM_GROUP, SSM_GROUP)

    def expand(x):
        return jnp.where(mask_v, _dot(x.astype(_BF16), t_state_ref[...]), 0.0)

    def contract(x, y):
        return lax.dot_general(x, y, (((1,), (1,)), ((), ())), precision=_EXACT,
                               preferred_element_type=_F32)

    for s in range(S5_SLICES):
        rows = slice(s * LANES, (s + 1) * LANES)
        a1 = (a_ref[0, rows, :], a_ref[1, rows, :])
        bb = (bb_ref[0, rows, :], bb_ref[1, rows, :])
        cc = (cre_ref[rows, :], cim_ref[rows, :])
        apow = [None, a1]
        for _ in range(r - 1):
            apow.append(_cmul(apow[-1], a1))
        ab = [bb] + [_cmul(apow[k], bb) for k in range(1, r)]
        for m in range(r):
            for part in range(2):
                wv_ref[s, m * LANES:(m + 1) * LANES, part * n_st:(part + 1) * n_st] = (
                    expand(ab[r - 1 - m][part]).astype(_BF16))
        feed = [jnp.where(mask_u, contract(ab[k][0], cc[0]) - contract(ab[k][1], cc[1]), 0.0).astype(_BF16)
                for k in range(r)]
        for i in range(r):
            cols = slice(i * LANES, (i + 1) * LANES)
            ca = _cmul(cc, apow[i + 1])
            wy_ref[s, 0:n_st, cols] = expand(ca[0]).T.astype(_BF16)
            wy_ref[s, n_st:2 * n_st, cols] = (-expand(ca[1])).T.astype(_BF16)
            for m in range(r):
                blk = feed[i - m] if m <= i else jnp.zeros((LANES, LANES), _BF16)
                wy_ref[s, 2 * n_st + m * LANES:2 * n_st + (m + 1) * LANES, cols] = blk


def _s5_matrices(abr, abi, bbr, bbi, c_re, c_im, cast=()):
    r = S5_BLOCK
    n_rows = SSM_GROUPS * SSM_GROUP
    a_rep = jnp.repeat(jnp.stack([abr, abi], axis=1), SSM_GROUP, axis=2)
    bb = jnp.stack([bbr, bbi]).reshape(2, SSM_GROUP, DEPTH, SSM_GROUPS, SSM_STATE)
    bb = jnp.transpose(bb, (2, 0, 3, 1, 4)).reshape(DEPTH, 2, n_rows, SSM_STATE)
    cre = c_re.reshape(DEPTH, n_rows, SSM_STATE)
    cim = c_im.reshape(DEPTH, n_rows, SSM_STATE)
    t_state = jnp.tile(jnp.eye(SSM_STATE, dtype=_BF16), (1, S5_SLICE_GROUPS))

    def per_layer(a):
        nd = a.ndim - 1
        return pl.BlockSpec((None,) + a.shape[1:], lambda d: (d,) + (0,) * nd)

    v_shape = (DEPTH, S5_SLICES, r * LANES, 2 * S5_SLICE_STATES)
    y_shape = (DEPTH, S5_SLICES, 2 * S5_SLICE_STATES + r * LANES, r * LANES)
    cast_arrs, cast_in, cast_out, cast_shape = _cast_specs(cast, DEPTH, lambda d: d)
    w_v, w_y, *cast_res = pl.pallas_call(
        _prepare_kernel,
        grid=(DEPTH,),
        in_specs=[per_layer(a_rep), per_layer(bb), per_layer(cre), per_layer(cim),
                  pl.BlockSpec(t_state.shape, lambda d: (0, 0))] + cast_in,
        out_specs=tuple([pl.BlockSpec((None,) + v_shape[1:], lambda d: (d, 0, 0, 0)),
                         pl.BlockSpec((None,) + y_shape[1:], lambda d: (d, 0, 0, 0))] + cast_out),
        out_shape=tuple([jax.ShapeDtypeStruct(v_shape, _BF16), jax.ShapeDtypeStruct(y_shape, _BF16)]
                        + cast_shape),
        name="s5_prepare",
    )(a_rep, bb, cre, cim, t_state, *cast_arrs)
    a_pow = (abr, abi)
    for _ in range(r - 1):
        a_pow = _cmul(a_pow, (abr, abi))
    a_blk = jnp.stack(a_pow, axis=1).reshape(DEPTH, 2, SSM_FLAT)
    return w_v, w_y, a_blk, [(a[None], 0) for a in cast_res]


def _tile_rows(ref, i, rows):
    nb, tb, c = ref.shape
    if nb == 1:
        return ref[0, i * rows:(i + 1) * rows, :]
    per = rows // tb
    return ref[i * per:(i + 1) * per].reshape(rows, c)


def _store_tile_rows(ref, i, rows, val):
    nb, tb, c = ref.shape
    if nb == 1:
        ref[0, i * rows:(i + 1) * rows, :] = val
    else:
        per = rows // tb
        ref[i * per:(i + 1) * per] = val.reshape(per, tb, c)


def _load_slabs(ref, i, rows, t_len, seq_rows):
    if seq_rows == t_len:
        return jnp.concatenate([ref[s, i * rows:(i + 1) * rows, :] for s in range(S5_SLICES)], axis=-1)
    per = rows // t_len
    return jnp.concatenate(
        [jnp.concatenate([ref[s, n * seq_rows:n * seq_rows + t_len, :] for s in range(S5_SLICES)], axis=-1)
         for n in range(i * per, (i + 1) * per)], axis=0)


def _store_slabs(ref, i, rows, t_len, seq_rows, val):
    for s in range(S5_SLICES):
        cols = slice(s * LANES, (s + 1) * LANES)
        if seq_rows == t_len:
            ref[s, i * rows:(i + 1) * rows, :] = val[:, cols]
        else:
            per = rows // t_len
            for k in range(per):
                n = i * per + k
                ref[s, n * seq_rows:n * seq_rows + t_len, :] = val[k * t_len:(k + 1) * t_len, cols]


MIXER_INPUTS = 12
MIXER_VECS = ('g_mix', 'd_skip', 'b_glu', 'g_v', 'b_v', 'g_out_s', 'g_out_g')
FFN_VECS = ('g_ffn', 'g_ple', 'g_final')


def _vec_row(ref, names, name, width):
    k = names.index(name)
    return ref[k:k + 1, :width]


def _cast_rows(in_refs, out_refs):
    for src, dst in zip(in_refs, out_refs):
        dst[...] = src[...].astype(_BF16)


def _mixer_kernel(*refs, nseq, t_len, blk_pitch, keep_v, n_cast):
    (x_ref, h0r_ref, h0i_ref, vec_ref, w_in_ref, a_blk_ref, wv_ref, wy_ref,
     w_glu_ref, mixw_ref, mixb_ref, w_out_ref) = refs[:MIXER_INPUTS]
    g_mix, d_skip, b_glu, g_v, b_v, g_out_s, g_out_g = (
        _vec_row(vec_ref, MIXER_VECS, n, D_MODEL if n == 'g_mix' else SSM_WIDTH) for n in MIXER_VECS)
    n_in = MIXER_INPUTS + n_cast
    out_ref, sre_ref, sim_ref = refs[n_in:n_in + 3]
    n_out = 3 + int(keep_v)
    if keep_v:
        vn_ref = refs[n_in + 3]
    us_scr, ys_scr, yg_scr, hb2_scr = refs[n_in + n_out + n_cast:]
    rows = nseq * t_len
    n_in_tiles = rows // IN_TILE
    n_out_tiles = rows // OUT_TILE
    n_blk = t_len // S5_BLOCK
    seq_rows = S5_BLOCK * blk_pitch
    n_buf = nseq * blk_pitch

    @pl.when(pl.program_id(0) == 0)
    def _():
        sre_ref[...] = h0r_ref[...]
        sim_ref[...] = h0i_ref[...]
        us_scr[...] = jnp.zeros_like(us_scr)

    row = lax.broadcasted_iota(jnp.int32, (CHUNK, CHUNK), 0)
    col = lax.broadcasted_iota(jnp.int32, (CHUNK, CHUNK), 1)
    t_shift = t_len.bit_length() - 1
    causal = ((row >> t_shift) == (col >> t_shift)) & ((col & (t_len - 1)) <= (row & (t_len - 1)))
    mixw = [jnp.where(causal, mixw_ref[h], 0.0).astype(_BF16) for h in range(GMLP_HEADS)]

    def in_proj(i):
        x = _tile_rows(x_ref, i, IN_TILE)
        a = _rmsnorm(x, g_mix).astype(_BF16)
        return _dot(a, w_in_ref[...])

    def block_inputs(s):
        u_blk = jnp.concatenate([us_scr[s, pl.ds(m, n_buf, stride=S5_BLOCK), :] for m in range(S5_BLOCK)],
                                axis=-1).astype(_BF16)
        v_in = _dot(u_blk, wv_ref[s])
        for c in range(2 * S5_SLABS):
            hb2_scr[s % 2, c] = v_in[:, c * LANES:(c + 1) * LANES]
        return u_blk

    z_next = in_proj(0)
    for i in range(n_in_tiles):
        z = z_next
        if i + 1 < n_in_tiles:
            z_next = in_proj(i + 1)
        _store_slabs(us_scr, i, IN_TILE, t_len, seq_rows, z[:, :SSM_WIDTH])
        if i + 1 == n_in_tiles:
            u_next = block_inputs(0)
        zg = _gelu(z[:, SSM_WIDTH:])
        ug = zg[:, :GMLP_WIDTH]
        v = zg[:, GMLP_WIDTH:]
        mu = jnp.mean(v, axis=-1, keepdims=True)
        vc = v - mu
        r = lax.rsqrt(jnp.mean(vc * vc, axis=-1, keepdims=True) + EPS)
        vn = vc * r * g_v + b_v
        if keep_v:
            vn_ref[i * IN_TILE:(i + 1) * IN_TILE, :] = vn
        vn = vn.astype(_BF16)
        for b in range(IN_TILE // CHUNK):
            rb = slice(b * CHUNK, (b + 1) * CHUNK)
            mixed = [_dot(mixw[h], vn[rb, h * GMLP_HEAD_DIM:(h + 1) * GMLP_HEAD_DIM])
                     for h in range(GMLP_HEADS)]
            m = jnp.concatenate(mixed, axis=-1) + mixb_ref[...]
            yg = _rmsnorm(ug[rb, :] * m, g_out_g).astype(_BF16)
            yg_scr[i * IN_TILE + b * CHUNK:i * IN_TILE + (b + 1) * CHUNK, :] = yg

    _cast_rows(refs[MIXER_INPUTS:n_in], refs[n_in + n_out:n_in + n_out + n_cast])
    for s in range(S5_SLICES):
        u_blk = u_next
        if s + 1 < S5_SLICES:
            u_next = block_inputs(s + 1)
        hb_scr = hb2_scr.at[s % 2]
        s0 = s * S5_SLICE_STATES
        a_re = [jnp.broadcast_to(a_blk_ref[0:1, s0 + c * LANES:s0 + (c + 1) * LANES], (nseq, LANES))
                for c in range(S5_SLABS)]
        a_im = [jnp.broadcast_to(a_blk_ref[1:2, s0 + c * LANES:s0 + (c + 1) * LANES], (nseq, LANES))
                for c in range(S5_SLABS)]
        h_init = tuple(
            (sre_ref[:, s0 + c * LANES:s0 + (c + 1) * LANES], sim_ref[:, s0 + c * LANES:s0 + (c + 1) * LANES])
            for c in range(S5_SLABS))

        def scan_step(j, h):
            new = []
            for c in range(S5_SLABS):
                rows_j = pl.ds(j, nseq, stride=blk_pitch)
                hr, hi = h[c]
                v_re = hb_scr[c, rows_j, :]
                v_im = hb_scr[S5_SLABS + c, rows_j, :]
                hb_scr[c, rows_j, :] = hr
                hb_scr[S5_SLABS + c, rows_j, :] = hi
                new.append((a_re[c] * hr - a_im[c] * hi + v_re, a_re[c] * hi + a_im[c] * hr + v_im))
            return tuple(new)
        h_fin = h_init
        for j in range(n_blk):
            h_fin = scan_step(j, h_fin)
        for c in range(S5_SLABS):
            sre_ref[:, s0 + c * LANES:s0 + (c + 1) * LANES] = h_fin[c][0]
            sim_ref[:, s0 + c * LANES:s0 + (c + 1) * LANES] = h_fin[c][1]

        lhs = jnp.concatenate([hb_scr[c].astype(_BF16) for c in range(2 * S5_SLABS)] + [u_blk], axis=-1)
        y_blk = _dot(lhs, wy_ref[s])
        for i in range(S5_BLOCK):
            ys_scr[s, pl.ds(i, n_buf, stride=S5_BLOCK), :] = y_blk[:, i * LANES:(i + 1) * LANES]

    def glu_in(i):
        ys = (_load_slabs(ys_scr, i, OUT_TILE, t_len, seq_rows)
              + d_skip * _load_slabs(us_scr, i, OUT_TILE, t_len, seq_rows))
        ys = _gelu(ys)
        return ys, _dot(ys.astype(_BF16), w_glu_ref[...])

    nxt = glu_in(0)
    for i in range(n_out_tiles):
        ys, gate = nxt
        if i + 1 < n_out_tiles:
            nxt = glu_in(i + 1)
        ys = ys * jax.nn.sigmoid(gate + b_glu)
        ys = _rmsnorm(ys, g_out_s).astype(_BF16)
        y = (_dot(ys, w_out_ref[:SSM_WIDTH, :])
             + _dot(yg_scr[i * OUT_TILE:(i + 1) * OUT_TILE, :], w_out_ref[SSM_WIDTH:, :]))
        _store_tile_rows(out_ref, i, OUT_TILE, _tile_rows(x_ref, i, OUT_TILE) + y)


def _layer_spec(arr, layer):
    nd = arr.ndim - 1
    return pl.BlockSpec((None,) + arr.shape[1:], lambda j: (layer,) + (0,) * nd,
                        pipeline_mode=pl.Buffered(1))


MIXER_WEIGHTS = ('mixer_vecs', 'w_in', 'a_blk', 'w_v', 'w_y', 'w_glu', 'mixw', 'mixb', 'w_out')
FFN_WEIGHTS = ('ffn_vecs', 'w_up', 'w_down', 'w_ple_gate', 'w_ple')


def _weights(w, names, layer):
    picked = [w[k] if isinstance(w[k], tuple) else (w[k], layer) for k in names]
    return [a for a, _ in picked], [_layer_spec(a, l) for a, l in picked]


def _cast_specs(cast, steps, blk_of):
    arrs, in_specs, out_specs, out_shape = [], [], [], []
    for a, layer in cast:
        _, r, c = a.shape
        rb = r // steps
        assert rb * steps == r and rb % (2 * SUBLANES) == 0
        arrs.append(a)
        in_specs.append(pl.BlockSpec((None, rb, c), lambda j, layer=layer: (layer, blk_of(j), 0)))
        out_specs.append(pl.BlockSpec((rb, c), lambda j: (blk_of(j), 0)))
        out_shape.append(jax.ShapeDtypeStruct((r, c), _BF16))
    return arrs, in_specs, out_specs, out_shape


def _mixer(x, h0r, h0i, w, layer, *, h0_layer, nseq, t_len, blk_pitch, keep_v, cast=()):
    nb, length, _ = x.shape
    t_blk = t_len if nb == nseq else nseq * t_len
    rows = nseq * t_len
    steps = length // t_blk
    slab_rows = nseq * S5_BLOCK * blk_pitch
    x_spec = pl.BlockSpec((nb, t_blk, D_MODEL), lambda j: (0, j, 0))
    st_spec = pl.BlockSpec((nseq, SSM_FLAT), lambda j: (0, 0))
    names = [k + ('_s' if keep_v else '_p') if k in ('mixw', 'mixb') else k for k in MIXER_WEIGHTS]
    consts, const_specs = _weights(w, names, layer)
    assert len(consts) + 3 == MIXER_INPUTS
    cast_arrs, cast_in, cast_out, cast_shape = _cast_specs(cast, steps, lambda j: j)
    out_specs = [x_spec, st_spec, st_spec]
    out_shape = [jax.ShapeDtypeStruct(x.shape, _F32),
                 jax.ShapeDtypeStruct((nseq, SSM_FLAT), _F32),
                 jax.ShapeDtypeStruct((nseq, SSM_FLAT), _F32)]
    scratch = [pltpu.VMEM((S5_SLICES, slab_rows, LANES), _F32),
               pltpu.VMEM((S5_SLICES, slab_rows, LANES), _F32),
               pltpu.VMEM((rows, GMLP_WIDTH), _BF16),
               pltpu.VMEM((2, 2 * S5_SLABS, nseq * blk_pitch, LANES), _F32)]
    if keep_v:
        out_specs.append(pl.BlockSpec((rows, GMLP_WIDTH), lambda j: (j, 0)))
        out_shape.append(jax.ShapeDtypeStruct((steps * rows, GMLP_WIDTH), _F32))
    res = pl.pallas_call(
        functools.partial(_mixer_kernel, nseq=nseq, t_len=t_len, blk_pitch=blk_pitch, keep_v=keep_v,
                          n_cast=len(cast_arrs)),
        grid=(steps,),
        in_specs=([x_spec, _layer_spec(h0r, h0_layer), _layer_spec(h0i, h0_layer)] + const_specs + cast_in),
        out_specs=tuple(out_specs + cast_out),
        out_shape=tuple(out_shape + cast_shape),
        scratch_shapes=scratch,
        compiler_params=pltpu.CompilerParams(dimension_semantics=("arbitrary",),
                                             vmem_limit_bytes=MIXER_VMEM_BYTES),
        name="mixer_sample" if keep_v else "mixer_prompt",
    )(x, h0r, h0i, *consts, *cast_arrs)
    n_out = 3 + int(keep_v)
    out, sre, sim = res[:3]
    return out, sre, sim, (res[3] if keep_v else None), [(a[None], 0) for a in res[n_out:]]


FFN_INPUTS = 9


def _ffn_kernel(*refs, final, prompt_steps, n_cast):
    hp_ref, hs_ref, pp_ref, ps_ref, vec_ref, w_up_ref, w_down_ref, w_gate_ref, w_ple_ref = refs[:FFN_INPUTS]
    n_in = FFN_INPUTS + n_cast
    outp_ref, outs_ref = refs[n_in:n_in + 2]
    _cast_rows(refs[FFN_INPUTS:n_in], refs[n_in + 2:])
    g_ffn, g_ple, g_fin = (_vec_row(vec_ref, FFN_VECS, n, D_MODEL) for n in FFN_VECS)

    def rows_block(h_ref, p_ref, out_ref):
        hr = FFN_ROWS // FFN_SPLIT
        parts = [slice(k * hr, (k + 1) * hr) for k in range(FFN_SPLIT)]
        accs = [h_ref[rs, :] for rs in parts]
        fs = [_rmsnorm(h, g_ffn).astype(_BF16) for h in accs]
        for c in range(D_FF // FFN_COLS):
            for k in range(FFN_SPLIT):
                up = _dot(fs[k], w_up_ref[:, c * FFN_COLS:(c + 1) * FFN_COLS])
                act = jnp.square(jnp.maximum(up, 0.0)).astype(_BF16)
                accs[k] = accs[k] + _dot(act, w_down_ref[c * FFN_COLS:(c + 1) * FFN_COLS, :])
        for k, rs in enumerate(parts):
            acc = accs[k]
            gate = jax.nn.sigmoid(_dot(_rmsnorm(acc, g_ple).astype(_BF16), w_gate_ref[...]))
            out = acc + gate * _dot(p_ref[rs, :].astype(_BF16), w_ple_ref[...])
            if final:
                out = _rmsnorm(out, g_fin)
            out_ref[rs, :] = out

    is_prompt = pl.program_id(0) < prompt_steps
    pl.when(is_prompt)(functools.partial(rows_block, hp_ref, pp_ref, outp_ref))
    pl.when(jnp.logical_not(is_prompt))(functools.partial(rows_block, hs_ref, ps_ref, outs_ref))


def _ffn(h_p, h_s, p_p, p_s, w, layer, *, final, cast=()):
    steps_p = h_p.shape[0] // FFN_ROWS
    steps_s = h_s.shape[0] // FFN_ROWS
    consts, const_specs = _weights(w, FFN_WEIGHTS, layer)
    assert len(consts) + 4 == FFN_INPUTS
    prompt_blk = lambda j: jnp.minimum(j, steps_p - 1)
    sample_blk = lambda j: jnp.maximum(j - steps_p, 0)
    cast_arrs, cast_in, cast_out, cast_shape = _cast_specs(cast, steps_p, prompt_blk)
    res = pl.pallas_call(
        functools.partial(_ffn_kernel, final=final, prompt_steps=steps_p, n_cast=len(cast_arrs)),
        grid=(steps_p + steps_s,),
        in_specs=([pl.BlockSpec((FFN_ROWS, D_MODEL), lambda j: (prompt_blk(j), 0)),
                   pl.BlockSpec((FFN_ROWS, D_MODEL), lambda j: (sample_blk(j), 0),
                                pipeline_mode=pl.Buffered(1)),
                   pl.BlockSpec((None, FFN_ROWS, PLE_DIM), lambda j: (layer, prompt_blk(j), 0)),
                   pl.BlockSpec((None, FFN_ROWS, PLE_DIM), lambda j: (layer, sample_blk(j), 0),
                                pipeline_mode=pl.Buffered(1))]
                  + const_specs + cast_in),
        out_specs=tuple([pl.BlockSpec((FFN_ROWS, D_MODEL), lambda j: (prompt_blk(j), 0)),
                         pl.BlockSpec((FFN_ROWS, D_MODEL), lambda j: (sample_blk(j), 0))] + cast_out),
        out_shape=tuple([jax.ShapeDtypeStruct(h_p.shape, _F32), jax.ShapeDtypeStruct(h_s.shape, _F32)]
                        + cast_shape),
        compiler_params=pltpu.CompilerParams(dimension_semantics=("arbitrary",),
                                             vmem_limit_bytes=FFN_VMEM_BYTES),
        name="ffn",
    )(h_p, h_s, p_p, p_s, *consts, *cast_arrs)
    return res[0], res[1], [(a[None], 0) for a in res[2:]]


def kernel(x_prompt, x_sample, state_ssm_re, state_ssm_im, p_prompt, p_sample, g_mix, w_in, lam_re, lam_im, log_dt, b_re, b_im, c_re, c_im, d_skip, w_glu, b_glu, g_v, b_v, w_s, b_s, g_out_s, g_out_g, w_out, g_ffn, w_up, w_down, g_ple, w_ple_gate, w_ple, g_final):
    batch, seq, _ = x_prompt.shape
    dec_batch, dec_seq, _ = x_sample.shape
    n_p, n_s = batch * seq, dec_batch * dec_seq
    assert seq % CHUNK == 0 and batch == SUBLANES
    assert dec_seq == S5_BLOCK and CHUNK % dec_seq == 0 and n_s % max(IN_TILE, OUT_TILE) == 0

    mixer_mats = dict(w_in=w_in, w_glu=w_glu, w_out=w_out)
    ffn_mats = dict(w_up=w_up, w_down=w_down, w_ple_gate=w_ple_gate, w_ple=w_ple)
    abr, abi, bbr, bbi = _discretise(lam_re, lam_im, log_dt, b_re, b_im)
    w_v, w_y, a_blk, cast = _s5_matrices(abr, abi, bbr, bbi, c_re, c_im,
                                         cast=[(a, 0) for a in mixer_mats.values()])

    def pack(vecs):
        return jnp.stack([jnp.pad(v, ((0, 0), (0, D_MODEL - v.shape[-1]))) for v in vecs], axis=1)
    head_of = (jnp.arange(GMLP_WIDTH)[None, :] // GMLP_HEAD_DIM == jnp.arange(GMLP_HEADS)[:, None]).astype(_F32)
    pos_of = (jnp.arange(CHUNK)[:, None] % dec_seq == jnp.arange(dec_seq)[None, :]).astype(_F32)
    w_s4 = w_s[:, :, :dec_seq, :dec_seq]
    w = dict(
        mixer_vecs=pack([g_mix, d_skip, b_glu, g_v, b_v, g_out_s, g_out_g]),
        ffn_vecs=pack([g_ffn, g_ple, jnp.broadcast_to(g_final, (DEPTH, D_MODEL))]),
        a_blk=a_blk, w_v=w_v, w_y=w_y,
        mixw_p=w_s,
        mixb_p=jnp.einsum('dht,hc->dtc', b_s, head_of, precision=_EXACT),
        mixw_s=jnp.einsum('rt,dhts,cs->dhrc', pos_of, w_s4, pos_of, precision=_EXACT),
        mixb_s=jnp.einsum('dht,rt,hc->drc', b_s[:, :, :dec_seq], pos_of, head_of, precision=_EXACT))

    zeros = jnp.zeros((1, batch, SSM_FLAT), _F32)
    s0_re = state_ssm_re.reshape(DEPTH, dec_batch, SSM_FLAT)
    s0_im = state_ssm_im.reshape(DEPTH, dec_batch, SSM_FLAT)
    pp = p_prompt.reshape(DEPTH, n_p, PLE_DIM)
    ps = p_sample.reshape(DEPTH, n_s, PLE_DIM)
    h_p = x_prompt
    h_s = x_sample.reshape(1, n_s, D_MODEL)
    re_p, im_p, re_s, im_s, v_s = [], [], [], [], []
    w.update(zip(mixer_mats, cast))
    for i in range(DEPTH):
        final = i == DEPTH - 1
        h_p, sre, sim, _, cast = _mixer(h_p, zeros, zeros, w, i, h0_layer=0, nseq=batch, t_len=CHUNK,
                                        blk_pitch=CHUNK // S5_BLOCK + SCAN_PAD_BLOCKS, keep_v=False,
                                        cast=[(a, i) for a in ffn_mats.values()])
        w.update(zip(ffn_mats, cast))
        re_p.append(sre)
        im_p.append(sim)
        h_s, sre, sim, vn, _ = _mixer(h_s, s0_re, s0_im, w, i, h0_layer=i, nseq=dec_batch, t_len=dec_seq,
                                      blk_pitch=1, keep_v=True)
        re_s.append(sre)
        im_s.append(sim)
        v_s.append(vn)
        h_p, h_s, cast = _ffn(h_p.reshape(n_p, D_MODEL), h_s.reshape(n_s, D_MODEL), pp, ps, w, i, final=final,
                              cast=[] if final else [(a, i + 1) for a in mixer_mats.values()])
        w.update(zip(mixer_mats, cast))
        h_p = h_p.reshape(batch, seq, D_MODEL)
        h_s = h_s.reshape(1, n_s, D_MODEL)

    st_p = (DEPTH, batch, SSM_GROUPS, SSM_STATE)
    st_s = (DEPTH, dec_batch, SSM_GROUPS, SSM_STATE)
    return (h_p, h_s.reshape(dec_batch, dec_seq, D_MODEL),
            jnp.stack(re_p).reshape(st_p), jnp.stack(im_p).reshape(st_p),
            jnp.stack(re_s).reshape(st_s), jnp.stack(im_s).reshape(st_s),
            jnp.stack(v_s).reshape(DEPTH, dec_batch, dec_seq, GMLP_WIDTH))
```

```python
import functools
import math

import jax
import jax.numpy as jnp
from jax import lax
from jax.experimental import pallas as pl
from jax.experimental.pallas import tpu as pltpu

D_MODEL = 1024
DEPTH = 2
SSM_WIDTH = 512
GMLP_WIDTH = 512
SSM_GROUP = 16
SSM_GROUPS = 32
SSM_STATE = 64
SSM_FLAT = SSM_GROUPS * SSM_STATE
CHUNK = 128
GMLP_HEADS = 4
GMLP_HEAD_DIM = 128
PLE_DIM = 256
D_FF = 4096
IN_WIDTH = SSM_WIDTH + 2 * GMLP_WIDTH
EPS = 1e-6
LAM_RE_MAX = -1e-4

LANES = 128
SUBLANES = 8
S5_SLICES = SSM_WIDTH // LANES
S5_SLICE_STATES = SSM_FLAT // S5_SLICES
S5_SLABS = S5_SLICE_STATES // LANES
S5_BLOCK = 4
SCAN_PAD_BLOCKS = 4
IN_TILE = 128
OUT_TILE = 256
FFN_ROWS = 1024
FFN_COLS = 1024
FFN_SPLIT = 2
MIXER_VMEM_BYTES = 56 * 1024 * 1024
FFN_VMEM_BYTES = 58 * 1024 * 1024

_F32 = jnp.float32
_BF16 = jnp.bfloat16
_EXACT = lax.Precision.HIGHEST


def _dot(a, b):
    return jnp.dot(a, b, preferred_element_type=_F32)


def _rmsnorm(x, g):
    r = lax.rsqrt(jnp.mean(x * x, axis=-1, keepdims=True) + EPS)
    return x * r * g


def _gelu(x):
    c = math.sqrt(2.0 / math.pi)
    return x * (0.5 * (1.0 + jnp.tanh(c * (x + 0.044715 * (x * x * x)))))


def _disc_kernel(lr_ref, li_ref, ldt_ref, br_ref, bi_ref, abr_ref, abi_ref, bbr_ref, bbi_ref):
    lr = jnp.minimum(lr_ref[...], LAM_RE_MAX)
    li = li_ref[...]
    dt = jnp.exp(ldt_ref[...])
    mag = jnp.exp(lr * dt)
    abr = mag * jnp.cos(li * dt)
    abi = mag * jnp.sin(li * dt)
    den = lr * lr + li * li
    nr = abr - 1.0
    ni = abi
    qr = (nr * lr + ni * li) / den
    qi = (ni * lr - nr * li) / den
    abr_ref[...] = abr
    abi_ref[...] = abi
    for h in range(SSM_GROUP):
        br = br_ref[h]
        bi = bi_ref[h]
        bbr_ref[h] = qr * br - qi * bi
        bbi_ref[h] = qr * bi + qi * br


def _discretise(lam_re, lam_im, log_dt, b_re, b_im):
    dg = DEPTH * SSM_GROUPS
    lr = lam_re.reshape(dg, SSM_STATE)
    li = lam_im.reshape(dg, SSM_STATE)
    ldt = log_dt.reshape(dg, 1)
    br = jnp.moveaxis(b_re, -1, 0).reshape(SSM_GROUP, dg, SSM_STATE)
    bi = jnp.moveaxis(b_im, -1, 0).reshape(SSM_GROUP, dg, SSM_STATE)
    vec = jax.ShapeDtypeStruct((dg, SSM_STATE), _F32)
    mat = jax.ShapeDtypeStruct((SSM_GROUP, dg, SSM_STATE), _F32)
    abr, abi, bbr, bbi = pl.pallas_call(
        _disc_kernel, out_shape=(vec, vec, mat, mat), name="s5_discretise")(lr, li, ldt, br, bi)
    vshp = (DEPTH, SSM_GROUPS, SSM_STATE)
    return abr.reshape(vshp), abi.reshape(vshp), bbr, bbi


def _cmul(a, b):
    return a[0] * b[0] - a[1] * b[1], a[0] * b[1] + a[1] * b[0]


S5_SLICE_GROUPS = SSM_GROUPS // S5_SLICES


def _same_group(shape, row_group, col_group):
    rg = lax.broadcasted_iota(jnp.int32, shape, 0) >> (row_group.bit_length() - 1)
    cg = lax.broadcasted_iota(jnp.int32, shape, 1) >> (col_group.bit_length() - 1)
    return rg == cg


PREPARE_INPUTS = 5


def _prepare_kernel(*refs):
    a_ref, bb_ref, cre_ref, cim_ref, t_state_ref = refs[:PREPARE_INPUTS]
    n_cast = (len(refs) - PREPARE_INPUTS - 2) // 2
    wv_ref, wy_ref = refs[PREPARE_INPUTS + n_cast:PREPARE_INPUTS + n_cast + 2]
    _cast_rows(refs[PREPARE_INPUTS:PREPARE_INPUTS + n_cast], refs[PREPARE_INPUTS + n_cast + 2:])
    r = S5_BLOCK
    n_st = S5_SLICE_STATES
    mask_v = _same_group((LANES, n_st), SSM_GROUP, SSM_STATE)
    mask_u = _same_group((LANES, LANES), SSM_GROUP, SSM_GROUP)

    def expand(x):
        return jnp.where(mask_v, _dot(x.astype(_BF16), t_state_ref[...]), 0.0)

    def contract(x, y):
        return lax.dot_general(x, y, (((1,), (1,)), ((), ())), precision=_EXACT,
                               preferred_element_type=_F32)

    for s in range(S5_SLICES):
        rows = slice(s * LANES, (s + 1) * LANES)
        a1 = (a_ref[0, rows, :], a_ref[1, rows, :])
        bb = (bb_ref[0, rows, :], bb_ref[1, rows, :])
        cc = (cre_ref[rows, :], cim_ref[rows, :])
        apow = [None, a1]
        for _ in range(r - 1):
            apow.append(_cmul(apow[-1], a1))
        ab = [bb] + [_cmul(apow[k], bb) for k in range(1, r)]
        for m in range(r):
            for part in range(2):
                wv_ref[s, m * LANES:(m + 1) * LANES, part * n_st:(part + 1) * n_st] = (
                    expand(ab[r - 1 - m][part]).astype(_BF16))
        feed = [jnp.where(mask_u, contract(ab[k][0], cc[0]) - contract(ab[k][1], cc[1]), 0.0).astype(_BF16)
                for k in range(r)]
        for i in range(r):
            cols = slice(i * LANES, (i + 1) * LANES)
            ca = _cmul(cc, apow[i + 1])
            wy_ref[s, 0:n_st, cols] = expand(ca[0]).T.astype(_BF16)
            wy_ref[s, n_st:2 * n_st, cols] = (-expand(ca[1])).T.astype(_BF16)
            for m in range(r):
                blk = feed[i - m] if m <= i else jnp.zeros((LANES, LANES), _BF16)
                wy_ref[s, 2 * n_st + m * LANES:2 * n_st + (m + 1) * LANES, cols] = blk


def _s5_matrices(abr, abi, bbr, bbi, c_re, c_im, cast=()):
    r = S5_BLOCK
    n_rows = SSM_GROUPS * SSM_GROUP
    a_rep = jnp.repeat(jnp.stack([abr, abi], axis=1), SSM_GROUP, axis=2)
    bb = jnp.stack([bbr, bbi]).reshape(2, SSM_GROUP, DEPTH, SSM_GROUPS, SSM_STATE)
    bb = jnp.transpose(bb, (2, 0, 3, 1, 4)).reshape(DEPTH, 2, n_rows, SSM_STATE)
    cre = c_re.reshape(DEPTH, n_rows, SSM_STATE)
    cim = c_im.reshape(DEPTH, n_rows, SSM_STATE)
    t_state = jnp.tile(jnp.eye(SSM_STATE, dtype=_BF16), (1, S5_SLICE_GROUPS))

    def per_layer(a):
        nd = a.ndim - 1
        return pl.BlockSpec((None,) + a.shape[1:], lambda d: (d,) + (0,) * nd)

    v_shape = (DEPTH, S5_SLICES, r * LANES, 2 * S5_SLICE_STATES)
    y_shape = (DEPTH, S5_SLICES, 2 * S5_SLICE_STATES + r * LANES, r * LANES)
    cast_arrs, cast_in, cast_out, cast_shape = _cast_specs(cast, DEPTH, lambda d: d)
    w_v, w_y, *cast_res = pl.pallas_call(
        _prepare_kernel,
        grid=(DEPTH,),
        in_specs=[per_layer(a_rep), per_layer(bb), per_layer(cre), per_layer(cim),
                  pl.BlockSpec(t_state.shape, lambda d: (0, 0))] + cast_in,
        out_specs=tuple([pl.BlockSpec((None,) + v_shape[1:], lambda d: (d, 0, 0, 0)),
                         pl.BlockSpec((None,) + y_shape[1:], lambda d: (d, 0, 0, 0))] + cast_out),
        out_shape=tuple([jax.ShapeDtypeStruct(v_shape, _BF16), jax.ShapeDtypeStruct(y_shape, _BF16)]
                        + cast_shape),
        name="s5_prepare",
    )(a_rep, bb, cre, cim, t_state, *cast_arrs)
    a_pow = (abr, abi)
    for _ in range(r - 1):
        a_pow = _cmul(a_pow, (abr, abi))
    a_blk = jnp.stack(a_pow, axis=1).reshape(DEPTH, 2, SSM_FLAT)
    return w_v, w_y, a_blk, [(a[None], 0) for a in cast_res]


def _tile_rows(ref, i, rows):
    nb, tb, c = ref.shape
    if nb == 1:
        return ref[0, i * rows:(i + 1) * rows, :]
    per = rows // tb
    return ref[i * per:(i + 1) * per].reshape(rows, c)


def _store_tile_rows(ref, i, rows, val):
    nb, tb, c = ref.shape
    if nb == 1:
        ref[0, i * rows:(i + 1) * rows, :] = val
    else:
        per = rows // tb
        ref[i * per:(i + 1) * per] = val.reshape(per, tb, c)


def _load_slabs(ref, i, rows, t_len, seq_rows):
    if seq_rows == t_len:
        return jnp.concatenate([ref[s, i * rows:(i + 1) * rows, :] for s in range(S5_SLICES)], axis=-1)
    per = rows // t_len
    return jnp.concatenate(
        [jnp.concatenate([ref[s, n * seq_rows:n * seq_rows + t_len, :] for s in range(S5_SLICES)], axis=-1)
         for n in range(i * per, (i + 1) * per)], axis=0)


def _store_slabs(ref, i, rows, t_len, seq_rows, val):
    for s in range(S5_SLICES):
        cols = slice(s * LANES, (s + 1) * LANES)
        if seq_rows == t_len:
            ref[s, i * rows:(i + 1) * rows, :] = val[:, cols]
        else:
            per = rows // t_len
            for k in range(per):
                n = i * per + k
                ref[s, n * seq_rows:n * seq_rows + t_len, :] = val[k * t_len:(k + 1) * t_len, cols]


MIXER_INPUTS = 12
MIXER_VECS = ('g_mix', 'd_skip', 'b_glu', 'g_v', 'b_v', 'g_out_s', 'g_out_g')
FFN_VECS = ('g_ffn', 'g_ple', 'g_final')


def _vec_row(ref, names, name, width):
    k = names.index(name)
    return ref[k:k + 1, :width]


def _cast_rows(in_refs, out_refs):
    for src, dst in zip(in_refs, out_refs):
        dst[...] = src[...].astype(_BF16)


def _mixer_kernel(*refs, nseq, t_len, blk_pitch, keep_v, n_cast):
    (x_ref, h0r_ref, h0i_ref, vec_ref, w_in_ref, a_blk_ref, wv_ref, wy_ref,
     w_glu_ref, mixw_ref, mixb_ref, w_out_ref) = refs[:MIXER_INPUTS]
    g_mix, d_skip, b_glu, g_v, b_v, g_out_s, g_out_g = (
        _vec_row(vec_ref, MIXER_VECS, n, D_MODEL if n == 'g_mix' else SSM_WIDTH) for n in MIXER_VECS)
    n_in = MIXER_INPUTS + n_cast
    out_ref, sre_ref, sim_ref = refs[n_in:n_in + 3]
    n_out = 3 + int(keep_v)
    if keep_v:
        vn_ref = refs[n_in + 3]
    us_scr, ys_scr, yg_scr, hb2_scr = refs[n_in + n_out + n_cast:]
    rows = nseq * t_len
    n_in_tiles = rows // IN_TILE
    n_out_tiles = rows // OUT_TILE
    n_blk = t_len // S5_BLOCK
    seq_rows = S5_BLOCK * blk_pitch
    n_buf = nseq * blk_pitch

    @pl.when(pl.program_id(0) == 0)
    def _():
        sre_ref[...] = h0r_ref[...]
        sim_ref[...] = h0i_ref[...]
        us_scr[...] = jnp.zeros_like(us_scr)

    row = lax.broadcasted_iota(jnp.int32, (CHUNK, CHUNK), 0)
    col = lax.broadcasted_iota(jnp.int32, (CHUNK, CHUNK), 1)
    t_shift = t_len.bit_length() - 1
    causal = ((row >> t_shift) == (col >> t_shift)) & ((col & (t_len - 1)) <= (row & (t_len - 1)))
    mixw = [jnp.where(causal, mixw_ref[h], 0.0).astype(_BF16) for h in range(GMLP_HEADS)]

    def in_proj(i):
        x = _tile_rows(x_ref, i, IN_TILE)
        a = _rmsnorm(x, g_mix).astype(_BF16)
        return _dot(a, w_in_ref[...])

    def block_inputs(s):
        u_blk = jnp.concatenate([us_scr[s, pl.ds(m, n_buf, stride=S5_BLOCK), :] for m in range(S5_BLOCK)],
                                axis=-1).astype(_BF16)
        v_in = _dot(u_blk, wv_ref[s])
        for c in range(2 * S5_SLABS):
            hb2_scr[s % 2, c] = v_in[:, c * LANES:(c + 1) * LANES]
        return u_blk

    z_next = in_proj(0)
    for i in range(n_in_tiles):
        z = z_next
        if i + 1 < n_in_tiles:
            z_next = in_proj(i + 1)
        _store_slabs(us_scr, i, IN_TILE, t_len, seq_rows, z[:, :SSM_WIDTH])
        if i + 1 == n_in_tiles:
            u_next = block_inputs(0)
        zg = _gelu(z[:, SSM_WIDTH:])
        ug = zg[:, :GMLP_WIDTH]
        v = zg[:, GMLP_WIDTH:]
        mu = jnp.mean(v, axis=-1, keepdims=True)
        vc = v - mu
        r = lax.rsqrt(jnp.mean(vc * vc, axis=-1, keepdims=True) + EPS)
        vn = vc * r * g_v + b_v
        if keep_v:
            vn_ref[i * IN_TILE:(i + 1) * IN_TILE, :] = vn
        vn = vn.astype(_BF16)
        for b in range(IN_TILE // CHUNK):
            rb = slice(b * CHUNK, (b + 1) * CHUNK)
            mixed = [_dot(mixw[h], vn[rb, h * GMLP_HEAD_DIM:(h + 1) * GMLP_HEAD_DIM])
                     for h in range(GMLP_HEADS)]
            m = jnp.concatenate(mixed, axis=-1) + mixb_ref[...]
            yg = _rmsnorm(ug[rb, :] * m, g_out_g).astype(_BF16)
            yg_scr[i * IN_TILE + b * CHUNK:i * IN_TILE + (b + 1) * CHUNK, :] = yg

    _cast_rows(refs[MIXER_INPUTS:n_in], refs[n_in + n_out:n_in + n_out + n_cast])
    for s in range(S5_SLICES):
        u_blk = u_next
        if s + 1 < S5_SLICES:
            u_next = block_inputs(s + 1)
        hb_scr = hb2_scr.at[s % 2]
        s0 = s * S5_SLICE_STATES
        a_re = [jnp.broadcast_to(a_blk_ref[0:1, s0 + c * LANES:s0 + (c + 1) * LANES], (nseq, LANES))
                for c in range(S5_SLABS)]
        a_im = [jnp.broadcast_to(a_blk_ref[1:2, s0 + c * LANES:s0 + (c + 1) * LANES], (nseq, LANES))
                for c in range(S5_SLABS)]
        h_init = tuple(
            (sre_ref[:, s0 + c * LANES:s0 + (c + 1) * LANES], sim_ref[:, s0 + c * LANES:s0 + (c + 1) * LANES])
            for c in range(S5_SLABS))

        def scan_step(j, h):
            new = []
            for c in range(S5_SLABS):
                rows_j = pl.ds(j, nseq, stride=blk_pitch)
                hr, hi = h[c]
                v_re = hb_scr[c, rows_j, :]
                v_im = hb_scr[S5_SLABS + c, rows_j, :]
                hb_scr[c, rows_j, :] = hr
                hb_scr[S5_SLABS + c, rows_j, :] = hi
                new.append((a_re[c] * hr - a_im[c] * hi + v_re, a_re[c] * hi + a_im[c] * hr + v_im))
            return tuple(new)
        h_fin = h_init
        for j in range(n_blk):
            h_fin = scan_step(j, h_fin)
        for c in range(S5_SLABS):
            sre_ref[:, s0 + c * LANES:s0 + (c + 1) * LANES] = h_fin[c][0]
            sim_ref[:, s0 + c * LANES:s0 + (c + 1) * LANES] = h_fin[c][1]

        lhs = jnp.concatenate([hb_scr[c].astype(_BF16) for c in range(2 * S5_SLABS)] + [u_blk], axis=-1)
        y_blk = _dot(lhs, wy_ref[s])
        for i in range(S5_BLOCK):
            ys_scr[s, pl.ds(i, n_buf, stride=S5_BLOCK), :] = y_blk[:, i * LANES:(i + 1) * LANES]

    def glu_in(i):
        ys = (_load_slabs(ys_scr, i, OUT_TILE, t_len, seq_rows)
              + d_skip * _load_slabs(us_scr, i, OUT_TILE, t_len, seq_rows))
        ys = _gelu(ys)
        return ys, _dot(ys.astype(_BF16), w_glu_ref[...])

    nxt = glu_in(0)
    for i in range(n_out_tiles):
        ys, gate = nxt
        if i + 1 < n_out_tiles:
            nxt = glu_in(i + 1)
        ys = ys * jax.nn.sigmoid(gate + b_glu)
        ys = _rmsnorm(ys, g_out_s).astype(_BF16)
        y = (_dot(ys, w_out_ref[:SSM_WIDTH, :])
             + _dot(yg_scr[i * OUT_TILE:(i + 1) * OUT_TILE, :], w_out_ref[SSM_WIDTH:, :]))
        _store_tile_rows(out_ref, i, OUT_TILE, _tile_rows(x_ref, i, OUT_TILE) + y)


def _layer_spec(arr, layer):
    nd = arr.ndim - 1
    return pl.BlockSpec((None,) + arr.shape[1:], lambda j: (layer,) + (0,) * nd,
                        pipeline_mode=pl.Buffered(1))


MIXER_WEIGHTS = ('mixer_vecs', 'w_in', 'a_blk', 'w_v', 'w_y', 'w_glu', 'mixw', 'mixb', 'w_out')
FFN_WEIGHTS = ('ffn_vecs', 'w_up', 'w_down', 'w_ple_gate', 'w_ple')


def _weights(w, names, layer):
    picked = [w[k] if isinstance(w[k], tuple) else (w[k], layer) for k in names]
    return [a for a, _ in picked], [_layer_spec(a, l) for a, l in picked]


def _cast_specs(cast, steps, blk_of):
    arrs, in_specs, out_specs, out_shape = [], [], [], []
    for a, layer in cast:
        _, r, c = a.shape
        rb = r // steps
        assert rb * steps == r and rb % (2 * SUBLANES) == 0
        arrs.append(a)
        in_specs.append(pl.BlockSpec((None, rb, c), lambda j, layer=layer: (layer, blk_of(j), 0)))
        out_specs.append(pl.BlockSpec((rb, c), lambda j: (blk_of(j), 0)))
        out_shape.append(jax.ShapeDtypeStruct((r, c), _BF16))
    return arrs, in_specs, out_specs, out_shape


def _mixer(x, h0r, h0i, w, layer, *, h0_layer, nseq, t_len, blk_pitch, keep_v, cast=()):
    nb, length, _ = x.shape
    t_blk = t_len if nb == nseq else nseq * t_len
    rows = nseq * t_len
    steps = length // t_blk
    slab_rows = nseq * S5_BLOCK * blk_pitch
    x_spec = pl.BlockSpec((nb, t_blk, D_MODEL), lambda j: (0, j, 0))
    st_spec = pl.BlockSpec((nseq, SSM_FLAT), lambda j: (0, 0))
    names = [k + ('_s' if keep_v else '_p') if k in ('mixw', 'mixb') else k for k in MIXER_WEIGHTS]
    consts, const_specs = _weights(w, names, layer)
    assert len(consts) + 3 == MIXER_INPUTS
    cast_arrs, cast_in, cast_out, cast_shape = _cast_specs(cast, steps, lambda j: j)
    out_specs = [x_spec, st_spec, st_spec]
    out_shape = [jax.ShapeDtypeStruct(x.shape, _F32),
                 jax.ShapeDtypeStruct((nseq, SSM_FLAT), _F32),
                 jax.ShapeDtypeStruct((nseq, SSM_FLAT), _F32)]
    scratch = [pltpu.VMEM((S5_SLICES, slab_rows, LANES), _F32),
               pltpu.VMEM((S5_SLICES, slab_rows, LANES), _F32),
               pltpu.VMEM((rows, GMLP_WIDTH), _BF16),
               pltpu.VMEM((2, 2 * S5_SLABS, nseq * blk_pitch, LANES), _F32)]
    if keep_v:
        out_specs.append(pl.BlockSpec((rows, GMLP_WIDTH), lambda j: (j, 0)))
        out_shape.append(jax.ShapeDtypeStruct((steps * rows, GMLP_WIDTH), _F32))
    res = pl.pallas_call(
        functools.partial(_mixer_kernel, nseq=nseq, t_len=t_len, blk_pitch=blk_pitch, keep_v=keep_v,
                          n_cast=len(cast_arrs)),
        grid=(steps,),
        in_specs=([x_spec, _layer_spec(h0r, h0_layer), _layer_spec(h0i, h0_layer)] + const_specs + cast_in),
        out_specs=tuple(out_specs + cast_out),
        out_shape=tuple(out_shape + cast_shape),
        scratch_shapes=scratch,
        compiler_params=pltpu.CompilerParams(dimension_semantics=("arbitrary",),
                                             vmem_limit_bytes=MIXER_VMEM_BYTES),
        name="mixer_sample" if keep_v else "mixer_prompt",
    )(x, h0r, h0i, *consts, *cast_arrs)
    n_out = 3 + int(keep_v)
    out, sre, sim = res[:3]
    return out, sre, sim, (res[3] if keep_v else None), [(a[None], 0) for a in res[n_out:]]


FFN_INPUTS = 9


def _ffn_kernel(*refs, final, prompt_steps, n_cast):
    hp_ref, hs_ref, pp_ref, ps_ref, vec_ref, w_up_ref, w_down_ref, w_gate_ref, w_ple_ref = refs[:FFN_INPUTS]
    n_in = FFN_INPUTS + n_cast
    outp_ref, outs_ref = refs[n_in:n_in + 2]
    _cast_rows(refs[FFN_INPUTS:n_in], refs[n_in + 2:])
    g_ffn, g_ple, g_fin = (_vec_row(vec_ref, FFN_VECS, n, D_MODEL) for n in FFN_VECS)

    def rows_block(h_ref, p_ref, out_ref):
        hr = h_ref.shape[0] // FFN_SPLIT
        parts = [slice(k * hr, (k + 1) * hr) for k in range(FFN_SPLIT)]
        accs = [h_ref[rs, :] for rs in parts]
        fs = [_rmsnorm(h, g_ffn).astype(_BF16) for h in accs]
        for c in range(D_FF // FFN_COLS):
            for k in range(FFN_SPLIT):
                up = _dot(fs[k], w_up_ref[:, c * FFN_COLS:(c + 1) * FFN_COLS])
                act = jnp.square(jnp.maximum(up, 0.0)).astype(_BF16)
                accs[k] = accs[k] + _dot(act, w_down_ref[c * FFN_COLS:(c + 1) * FFN_COLS, :])
        for k, rs in enumerate(parts):
            acc = accs[k]
            gate = jax.nn.sigmoid(_dot(_rmsnorm(acc, g_ple).astype(_BF16), w_gate_ref[...]))
            out = acc + gate * _dot(p_ref[rs, :].astype(_BF16), w_ple_ref[...])
            if final:
                out = _rmsnorm(out, g_fin)
            out_ref[rs, :] = out

    is_prompt = pl.program_id(0) < prompt_steps
    pl.when(is_prompt)(functools.partial(rows_block, hp_ref, pp_ref, outp_ref))
    pl.when(jnp.logical_not(is_prompt))(functools.partial(rows_block, hs_ref, ps_ref, outs_ref))


def _ffn(h_p, h_s, p_p, p_s, w, layer, *, final, cast=()):
    rows_s = min(FFN_ROWS, h_s.shape[0])
    steps_p = h_p.shape[0] // FFN_ROWS
    steps_s = h_s.shape[0] // rows_s
    consts, const_specs = _weights(w, FFN_WEIGHTS, layer)
    assert len(consts) + 4 == FFN_INPUTS
    prompt_blk = lambda j: jnp.minimum(j, steps_p - 1)
    sample_blk = lambda j: jnp.maximum(j - steps_p, 0)
    cast_arrs, cast_in, cast_out, cast_shape = _cast_specs(cast, steps_p, prompt_blk)
    res = pl.pallas_call(
        functools.partial(_ffn_kernel, final=final, prompt_steps=steps_p, n_cast=len(cast_arrs)),
        grid=(steps_p + steps_s,),
        in_specs=([pl.BlockSpec((FFN_ROWS, D_MODEL), lambda j: (prompt_blk(j), 0)),
                   pl.BlockSpec((rows_s, D_MODEL), lambda j: (sample_blk(j), 0),
                                pipeline_mode=pl.Buffered(1)),
                   pl.BlockSpec((None, FFN_ROWS, PLE_DIM), lambda j: (layer, prompt_blk(j), 0)),
                   pl.BlockSpec((None, rows_s, PLE_DIM), lambda j: (layer, sample_blk(j), 0),
                                pipeline_mode=pl.Buffered(1))]
                  + const_specs + cast_in),
        out_specs=tuple([pl.BlockSpec((FFN_ROWS, D_MODEL), lambda j: (prompt_blk(j), 0)),
                         pl.BlockSpec((rows_s, D_MODEL), lambda j: (sample_blk(j), 0))] + cast_out),
        out_shape=tuple([jax.ShapeDtypeStruct(h_p.shape, _F32), jax.ShapeDtypeStruct(h_s.shape, _F32)]
                        + cast_shape),
        compiler_params=pltpu.CompilerParams(dimension_semantics=("arbitrary",),
                                             vmem_limit_bytes=FFN_VMEM_BYTES),
        name="ffn",
    )(h_p, h_s, p_p, p_s, *consts, *cast_arrs)
    return res[0], res[1], [(a[None], 0) for a in res[2:]]


def kernel(x_prompt, x_sample, state_ssm_re, state_ssm_im, p_prompt, p_sample, g_mix, w_in, lam_re, lam_im, log_dt, b_re, b_im, c_re, c_im, d_skip, w_glu, b_glu, g_v, b_v, w_s, b_s, g_out_s, g_out_g, w_out, g_ffn, w_up, w_down, g_ple, w_ple_gate, w_ple, g_final):
    batch, seq, _ = x_prompt.shape
    dec_batch, dec_seq, _ = x_sample.shape
    n_p, n_s = batch * seq, dec_batch * dec_seq
    assert seq % CHUNK == 0 and batch == SUBLANES
    assert dec_seq % S5_BLOCK == 0 and CHUNK % dec_seq == 0 and n_s % max(IN_TILE, OUT_TILE) == 0

    mixer_mats = dict(w_in=w_in, w_glu=w_glu, w_out=w_out)
    ffn_mats = dict(w_up=w_up, w_down=w_down, w_ple_gate=w_ple_gate, w_ple=w_ple)
    abr, abi, bbr, bbi = _discretise(lam_re, lam_im, log_dt, b_re, b_im)
    w_v, w_y, a_blk, cast = _s5_matrices(abr, abi, bbr, bbi, c_re, c_im,
                                         cast=[(a, 0) for a in mixer_mats.values()])

    def pack(vecs):
        return jnp.stack([jnp.pad(v, ((0, 0), (0, D_MODEL - v.shape[-1]))) for v in vecs], axis=1)
    head_of = (jnp.arange(GMLP_WIDTH)[None, :] // GMLP_HEAD_DIM == jnp.arange(GMLP_HEADS)[:, None]).astype(_F32)
    pos_of = (jnp.arange(CHUNK)[:, None] % dec_seq == jnp.arange(dec_seq)[None, :]).astype(_F32)
    w_s4 = w_s[:, :, :dec_seq, :dec_seq]
    w = dict(
        mixer_vecs=pack([g_mix, d_skip, b_glu, g_v, b_v, g_out_s, g_out_g]),
        ffn_vecs=pack([g_ffn, g_ple, jnp.broadcast_to(g_final, (DEPTH, D_MODEL))]),
        a_blk=a_blk, w_v=w_v, w_y=w_y,
        mixw_p=w_s,
        mixb_p=jnp.einsum('dht,hc->dtc', b_s, head_of, precision=_EXACT),
        mixw_s=jnp.einsum('rt,dhts,cs->dhrc', pos_of, w_s4, pos_of, precision=_EXACT),
        mixb_s=jnp.einsum('dht,rt,hc->drc', b_s[:, :, :dec_seq], pos_of, head_of, precision=_EXACT))

    zeros = jnp.zeros((1, batch, SSM_FLAT), _F32)
    s0_re = state_ssm_re.reshape(DEPTH, dec_batch, SSM_FLAT)
    s0_im = state_ssm_im.reshape(DEPTH, dec_batch, SSM_FLAT)
    pp = p_prompt.reshape(DEPTH, n_p, PLE_DIM)
    ps = p_sample.reshape(DEPTH, n_s, PLE_DIM)
    h_p = x_prompt
    h_s = x_sample.reshape(1, n_s, D_MODEL)
    re_p, im_p, re_s, im_s, v_s = [], [], [], [], []
    w.update(zip(mixer_mats, cast))
    for i in range(DEPTH):
        final = i == DEPTH - 1
        h_p, sre, sim, _, cast = _mixer(h_p, zeros, zeros, w, i, h0_layer=0, nseq=batch, t_len=CHUNK,
                                        blk_pitch=CHUNK // S5_BLOCK + SCAN_PAD_BLOCKS, keep_v=False,
                                        cast=[(a, i) for a in ffn_mats.values()])
        w.update(zip(ffn_mats, cast))
        re_p.append(sre)
        im_p.append(sim)
        h_s, sre, sim, vn, _ = _mixer(h_s, s0_re, s0_im, w, i, h0_layer=i, nseq=dec_batch, t_len=dec_seq,
                                      blk_pitch=dec_seq // S5_BLOCK, keep_v=True)
        re_s.append(sre)
        im_s.append(sim)
        v_s.append(vn)
        h_p, h_s, cast = _ffn(h_p.reshape(n_p, D_MODEL), h_s.reshape(n_s, D_MODEL), pp, ps, w, i, final=final,
                              cast=[] if final else [(a, i + 1) for a in mixer_mats.values()])
        w.update(zip(mixer_mats, cast))
        h_p = h_p.reshape(batch, seq, D_MODEL)
        h_s = h_s.reshape(1, n_s, D_MODEL)

    st_p = (DEPTH, batch, SSM_GROUPS, SSM_STATE)
    st_s = (DEPTH, dec_batch, SSM_GROUPS, SSM_STATE)
    return (h_p, h_s.reshape(dec_batch, dec_seq, D_MODEL),
            jnp.stack(re_p).reshape(st_p), jnp.stack(im_p).reshape(st_p),
            jnp.stack(re_s).reshape(st_s), jnp.stack(im_s).reshape(st_s),
            jnp.stack(v_s).reshape(DEPTH, dec_batch, dec_seq, GMLP_WIDTH))
```

```python
import functools
import math

import jax
import jax.numpy as jnp
from jax import lax
from jax.experimental import pallas as pl
from jax.experimental.pallas import tpu as pltpu

D_MODEL = 1024
DEPTH = 2
SSM_WIDTH = 512
GMLP_WIDTH = 512
SSM_GROUP = 16
SSM_GROUPS = 32
SSM_STATE = 64
SSM_FLAT = SSM_GROUPS * SSM_STATE
CHUNK = 128
GMLP_HEADS = 4
GMLP_HEAD_DIM = 128
PLE_DIM = 256
D_FF = 4096
IN_WIDTH = SSM_WIDTH + 2 * GMLP_WIDTH
EPS = 1e-6
LAM_RE_MAX = -1e-4

LANES = 128
SUBLANES = 8
S5_SLICES = SSM_WIDTH // LANES
S5_SUBS = 2 * S5_SLICES
S5_SUB_CH = SSM_WIDTH // S5_SUBS
S5_SUB_STATES = SSM_FLAT // S5_SUBS
S5_SUB_SLABS = S5_SUB_STATES // LANES
S5_BLOCK = 4
SCAN_PAD_BLOCKS = 4
IN_TILE = 128
OUT_TILE = 256
FFN_ROWS = 512
FFN_COLS = 1024
FFN_SPLIT = 2
MIXER_VMEM_BYTES = 56 * 1024 * 1024
FFN_VMEM_BYTES = 52 * 1024 * 1024

_F32 = jnp.float32
_BF16 = jnp.bfloat16
_EXACT = lax.Precision.HIGHEST


def _dot(a, b):
    return jnp.dot(a, b, preferred_element_type=_F32)


def _rmsnorm(x, g):
    r = lax.rsqrt(jnp.mean(x * x, axis=-1, keepdims=True) + EPS)
    return x * r * g


def _gelu(x):
    c = math.sqrt(2.0 / math.pi)
    return x * (0.5 * (1.0 + jnp.tanh(c * (x + 0.044715 * (x * x * x)))))


def _disc_kernel(lr_ref, li_ref, ldt_ref, br_ref, bi_ref, abr_ref, abi_ref, bbr_ref, bbi_ref):
    lr = jnp.minimum(lr_ref[...], LAM_RE_MAX)
    li = li_ref[...]
    dt = jnp.exp(ldt_ref[...])
    mag = jnp.exp(lr * dt)
    abr = mag * jnp.cos(li * dt)
    abi = mag * jnp.sin(li * dt)
    den = lr * lr + li * li
    nr = abr - 1.0
    ni = abi
    qr = (nr * lr + ni * li) / den
    qi = (ni * lr - nr * li) / den
    abr_ref[...] = abr
    abi_ref[...] = abi
    for h in range(SSM_GROUP):
        br = br_ref[h]
        bi = bi_ref[h]
        bbr_ref[h] = qr * br - qi * bi
        bbi_ref[h] = qr * bi + qi * br


def _discretise(lam_re, lam_im, log_dt, b_re, b_im):
    dg = DEPTH * SSM_GROUPS
    lr = lam_re.reshape(dg, SSM_STATE)
    li = lam_im.reshape(dg, SSM_STATE)
    ldt = log_dt.reshape(dg, 1)
    br = jnp.moveaxis(b_re, -1, 0).reshape(SSM_GROUP, dg, SSM_STATE)
    bi = jnp.moveaxis(b_im, -1, 0).reshape(SSM_GROUP, dg, SSM_STATE)
    vec = jax.ShapeDtypeStruct((dg, SSM_STATE), _F32)
    mat = jax.ShapeDtypeStruct((SSM_GROUP, dg, SSM_STATE), _F32)
    abr, abi, bbr, bbi = pl.pallas_call(
        _disc_kernel, out_shape=(vec, vec, mat, mat), name="s5_discretise")(lr, li, ldt, br, bi)
    vshp = (DEPTH, SSM_GROUPS, SSM_STATE)
    return abr.reshape(vshp), abi.reshape(vshp), bbr, bbi


def _cmul(a, b):
    return a[0] * b[0] - a[1] * b[1], a[0] * b[1] + a[1] * b[0]


PREPARE_INPUTS = 6


def _prepare_kernel(*refs):
    a_ref, bb_ref, cre_ref, cim_ref, t_ref, tt_ref = refs[:PREPARE_INPUTS]
    n_cast = (len(refs) - PREPARE_INPUTS - 2) // 2
    wv_ref, wy_ref = refs[PREPARE_INPUTS + n_cast:PREPARE_INPUTS + n_cast + 2]
    _cast_rows(refs[PREPARE_INPUTS:PREPARE_INPUTS + n_cast], refs[PREPARE_INPUTS + n_cast + 2:])
    r = S5_BLOCK
    ch = S5_SUB_CH
    n_st = S5_SUB_STATES
    g_shift = SSM_GROUP.bit_length() - 1
    p_shift = SSM_STATE.bit_length() - 1

    def group_mask(n_rows, row_shift, n_cols, col_width, col_shift):
        rg = lax.broadcasted_iota(jnp.int32, (n_rows, n_cols), 0) >> row_shift
        cg = (lax.broadcasted_iota(jnp.int32, (n_rows, n_cols), 1) & (col_width - 1)) >> col_shift
        return rg == cg

    mask_v = group_mask(ch, g_shift, n_st, n_st, p_shift)
    mask_s = group_mask(n_st, p_shift, r * ch, ch, g_shift)
    mask_u = group_mask(ch, g_shift, r * ch, ch, g_shift)

    def dot_t(x, y, precision=None):
        return lax.dot_general(x, y, (((1,), (1,)), ((), ())), precision=precision,
                               preferred_element_type=_F32)

    def expand(x):
        return jnp.where(mask_v, _dot(x.astype(_BF16), t_ref[...]), 0.0)

    for sub in range(S5_SUBS):
        rows = slice(sub * ch, (sub + 1) * ch)
        a1 = (a_ref[0, rows, :], a_ref[1, rows, :])
        bb = (bb_ref[0, rows, :], bb_ref[1, rows, :])
        cc = (cre_ref[rows, :], cim_ref[rows, :])
        apow = [None, a1]
        for _ in range(r - 1):
            apow.append(_cmul(apow[-1], a1))
        ab = [bb] + [_cmul(apow[k], bb) for k in range(1, r)]
        ca = [cc] + [_cmul(cc, apow[k]) for k in range(1, r + 1)]
        for m in range(r):
            band = jnp.concatenate([expand(ab[r - 1 - m][0]), expand(ab[r - 1 - m][1])], axis=-1)
            wv_ref[sub, m * ch:(m + 1) * ch, :] = band.astype(_BF16)
        for part in range(2):
            ca_all = jnp.concatenate([ca[i + 1][part] for i in range(r)], axis=0).astype(_BF16)
            band = jnp.where(mask_s, dot_t(tt_ref[...], ca_all), 0.0)
            wy_ref[sub, part * n_st:(part + 1) * n_st, :] = (band if part == 0 else -band).astype(_BF16)
        none = jnp.zeros((ch, SSM_STATE), _F32)
        for m in range(r):
            shifted = [jnp.concatenate([ca[i - m][part] if i >= m else none for i in range(r)], axis=0)
                       for part in range(2)]
            band = dot_t(bb[0], shifted[0], _EXACT) - dot_t(bb[1], shifted[1], _EXACT)
            wy_ref[sub, 2 * n_st + m * ch:2 * n_st + (m + 1) * ch, :] = (
                jnp.where(mask_u, band, 0.0).astype(_BF16))


def _s5_matrices(abr, abi, bbr, bbi, c_re, c_im, cast=()):
    r = S5_BLOCK
    n_rows = SSM_GROUPS * SSM_GROUP
    a_rep = jnp.repeat(jnp.stack([abr, abi], axis=1), SSM_GROUP, axis=2)
    bb = jnp.stack([bbr, bbi]).reshape(2, SSM_GROUP, DEPTH, SSM_GROUPS, SSM_STATE)
    bb = jnp.transpose(bb, (2, 0, 3, 1, 4)).reshape(DEPTH, 2, n_rows, SSM_STATE)
    cre = c_re.reshape(DEPTH, n_rows, SSM_STATE)
    cim = c_im.reshape(DEPTH, n_rows, SSM_STATE)
    t_state = jnp.tile(jnp.eye(SSM_STATE, dtype=_BF16), (1, SSM_GROUPS // S5_SUBS))

    def per_layer(a):
        nd = a.ndim - 1
        return pl.BlockSpec((None,) + a.shape[1:], lambda d: (d,) + (0,) * nd)

    v_shape = (DEPTH, S5_SUBS, r * S5_SUB_CH, 2 * S5_SUB_STATES)
    y_shape = (DEPTH, S5_SUBS, 2 * S5_SUB_STATES + r * S5_SUB_CH, r * S5_SUB_CH)
    cast_arrs, cast_in, cast_out, cast_shape = _cast_specs(cast, DEPTH, lambda d: d)
    w_v, w_y, *cast_res = pl.pallas_call(
        _prepare_kernel,
        grid=(DEPTH,),
        in_specs=[per_layer(a_rep), per_layer(bb), per_layer(cre), per_layer(cim),
                  pl.BlockSpec(t_state.shape, lambda d: (0, 0)),
                  pl.BlockSpec(t_state.shape[::-1], lambda d: (0, 0))] + cast_in,
        out_specs=tuple([pl.BlockSpec((None,) + v_shape[1:], lambda d: (d, 0, 0, 0)),
                         pl.BlockSpec((None,) + y_shape[1:], lambda d: (d, 0, 0, 0))] + cast_out),
        out_shape=tuple([jax.ShapeDtypeStruct(v_shape, _BF16), jax.ShapeDtypeStruct(y_shape, _BF16)]
                        + cast_shape),
        name="s5_prepare",
    )(a_rep, bb, cre, cim, t_state, t_state.T, *cast_arrs)
    a_pow = (abr, abi)
    for _ in range(r - 1):
        a_pow = _cmul(a_pow, (abr, abi))
    a_blk = jnp.stack(a_pow, axis=1).reshape(DEPTH, 2, SSM_FLAT)
    return w_v, w_y, a_blk, [(a[None], 0) for a in cast_res]


def _tile_rows(ref, i, rows):
    nb, tb, c = ref.shape
    if nb == 1:
        return ref[0, i * rows:(i + 1) * rows, :]
    per = rows // tb
    return ref[i * per:(i + 1) * per].reshape(rows, c)


def _store_tile_rows(ref, i, rows, val):
    nb, tb, c = ref.shape
    if nb == 1:
        ref[0, i * rows:(i + 1) * rows, :] = val
    else:
        per = rows // tb
        ref[i * per:(i + 1) * per] = val.reshape(per, tb, c)


def _load_slabs(ref, i, rows, t_len, seq_rows):
    if seq_rows == t_len:
        return jnp.concatenate([ref[s, i * rows:(i + 1) * rows, :] for s in range(S5_SLICES)], axis=-1)
    per = rows // t_len
    return jnp.concatenate(
        [jnp.concatenate([ref[s, n * seq_rows:n * seq_rows + t_len, :] for s in range(S5_SLICES)], axis=-1)
         for n in range(i * per, (i + 1) * per)], axis=0)


def _store_slabs(ref, i, rows, t_len, seq_rows, val):
    for s in range(S5_SLICES):
        cols = slice(s * LANES, (s + 1) * LANES)
        if seq_rows == t_len:
            ref[s, i * rows:(i + 1) * rows, :] = val[:, cols]
        else:
            per = rows // t_len
            for k in range(per):
                n = i * per + k
                ref[s, n * seq_rows:n * seq_rows + t_len, :] = val[k * t_len:(k + 1) * t_len, cols]


MIXER_INPUTS = 12
MIXER_VECS = ('g_mix', 'd_skip', 'b_glu', 'g_v', 'b_v', 'g_out_s', 'g_out_g')
FFN_VECS = ('g_ffn', 'g_ple', 'g_final')


def _vec_row(ref, names, name, width):
    k = names.index(name)
    return ref[k:k + 1, :width]


def _cast_rows(in_refs, out_refs):
    for src, dst in zip(in_refs, out_refs):
        dst[...] = src[...].astype(_BF16)


def _mixer_kernel(*refs, nseq, t_len, blk_pitch, keep_v, n_cast):
    (x_ref, h0r_ref, h0i_ref, vec_ref, w_in_ref, a_blk_ref, wv_ref, wy_ref,
     w_glu_ref, mixw_ref, mixb_ref, w_out_ref) = refs[:MIXER_INPUTS]
    g_mix, d_skip, b_glu, g_v, b_v, g_out_s, g_out_g = (
        _vec_row(vec_ref, MIXER_VECS, n, D_MODEL if n == 'g_mix' else SSM_WIDTH) for n in MIXER_VECS)
    n_in = MIXER_INPUTS + n_cast
    out_ref, sre_ref, sim_ref = refs[n_in:n_in + 3]
    n_out = 3 + int(keep_v)
    if keep_v:
        vn_ref = refs[n_in + 3]
    us_scr, ys_scr, yg_scr, hb2_scr = refs[n_in + n_out + n_cast:]
    rows = nseq * t_len
    n_in_tiles = rows // IN_TILE
    n_out_tiles = rows // OUT_TILE
    n_blk = t_len // S5_BLOCK
    seq_rows = S5_BLOCK * blk_pitch
    n_buf = nseq * blk_pitch

    @pl.when(pl.program_id(0) == 0)
    def _():
        sre_ref[...] = h0r_ref[...]
        sim_ref[...] = h0i_ref[...]
        us_scr[...] = jnp.zeros_like(us_scr)

    row = lax.broadcasted_iota(jnp.int32, (CHUNK, CHUNK), 0)
    col = lax.broadcasted_iota(jnp.int32, (CHUNK, CHUNK), 1)
    t_shift = t_len.bit_length() - 1
    causal = ((row >> t_shift) == (col >> t_shift)) & ((col & (t_len - 1)) <= (row & (t_len - 1)))
    mixw = [jnp.where(causal, mixw_ref[h], 0.0).astype(_BF16) for h in range(GMLP_HEADS)]

    def in_proj(i):
        x = _tile_rows(x_ref, i, IN_TILE)
        a = _rmsnorm(x, g_mix).astype(_BF16)
        return _dot(a, w_in_ref[...])

    lane_lo = lax.broadcasted_iota(jnp.int32, (n_buf, LANES), 1) < S5_SUB_CH

    def regroup_halves(blocks):
        rolled = [pltpu.roll(x, S5_SUB_CH, axis=1) for x in blocks]
        pairs = range(0, len(blocks), 2)
        lo = [jnp.where(lane_lo, blocks[k], rolled[k + 1]) for k in pairs]
        hi = [jnp.where(lane_lo, rolled[k], blocks[k + 1]) for k in pairs]
        return jnp.concatenate(lo, axis=-1), jnp.concatenate(hi, axis=-1)

    def block_inputs(sub, u_halves):
        if sub % 2 == 0:
            s = sub // 2
            u_halves = regroup_halves([us_scr[s, pl.ds(m, n_buf, stride=S5_BLOCK), :]
                                       for m in range(S5_BLOCK)])
        u_blk = u_halves[sub % 2].astype(_BF16)
        v_in = _dot(u_blk, wv_ref[sub])
        for c in range(2 * S5_SUB_SLABS):
            hb2_scr[sub % 2, c] = v_in[:, c * LANES:(c + 1) * LANES]
        return u_blk, u_halves

    z_next = in_proj(0)
    for i in range(n_in_tiles):
        z = z_next
        if i + 1 < n_in_tiles:
            z_next = in_proj(i + 1)
        _store_slabs(us_scr, i, IN_TILE, t_len, seq_rows, z[:, :SSM_WIDTH])
        if i + 1 == n_in_tiles:
            u_next, u_halves = block_inputs(0, None)
        zg = _gelu(z[:, SSM_WIDTH:])
        ug = zg[:, :GMLP_WIDTH]
        v = zg[:, GMLP_WIDTH:]
        mu = jnp.mean(v, axis=-1, keepdims=True)
        vc = v - mu
        r = lax.rsqrt(jnp.mean(vc * vc, axis=-1, keepdims=True) + EPS)
        vn = vc * r * g_v + b_v
        if keep_v:
            vn_ref[i * IN_TILE:(i + 1) * IN_TILE, :] = vn
        vn = vn.astype(_BF16)
        for b in range(IN_TILE // CHUNK):
            rb = slice(b * CHUNK, (b + 1) * CHUNK)
            mixed = [_dot(mixw[h], vn[rb, h * GMLP_HEAD_DIM:(h + 1) * GMLP_HEAD_DIM])
                     for h in range(GMLP_HEADS)]
            m = jnp.concatenate(mixed, axis=-1) + mixb_ref[...]
            yg = _rmsnorm(ug[rb, :] * m, g_out_g).astype(_BF16)
            yg_scr[i * IN_TILE + b * CHUNK:i * IN_TILE + (b + 1) * CHUNK, :] = yg

    _cast_rows(refs[MIXER_INPUTS:n_in], refs[n_in + n_out:n_in + n_out + n_cast])
    y_prev = None
    for sub in range(S5_SUBS):
        u_blk = u_next
        if sub + 1 < S5_SUBS:
            u_next, u_halves = block_inputs(sub + 1, u_halves)
        hb_scr = hb2_scr.at[sub % 2]
        s0 = sub * S5_SUB_STATES
        a_re = [jnp.broadcast_to(a_blk_ref[0:1, s0 + c * LANES:s0 + (c + 1) * LANES], (nseq, LANES))
                for c in range(S5_SUB_SLABS)]
        a_im = [jnp.broadcast_to(a_blk_ref[1:2, s0 + c * LANES:s0 + (c + 1) * LANES], (nseq, LANES))
                for c in range(S5_SUB_SLABS)]
        h_init = tuple(
            (sre_ref[:, s0 + c * LANES:s0 + (c + 1) * LANES], sim_ref[:, s0 + c * LANES:s0 + (c + 1) * LANES])
            for c in range(S5_SUB_SLABS))

        def scan_step(j, h):
            new = []
            for c in range(S5_SUB_SLABS):
                rows_j = pl.ds(j, nseq, stride=blk_pitch)
                hr, hi = h[c]
                v_re = hb_scr[c, rows_j, :]
                v_im = hb_scr[S5_SUB_SLABS + c, rows_j, :]
                hb_scr[c, rows_j, :] = hr
                hb_scr[S5_SUB_SLABS + c, rows_j, :] = hi
                new.append((a_re[c] * hr - a_im[c] * hi + v_re, a_re[c] * hi + a_im[c] * hr + v_im))
            return tuple(new)
        h_fin = h_init
        for j in range(n_blk):
            h_fin = scan_step(j, h_fin)
        for c in range(S5_SUB_SLABS):
            sre_ref[:, s0 + c * LANES:s0 + (c + 1) * LANES] = h_fin[c][0]
            sim_ref[:, s0 + c * LANES:s0 + (c + 1) * LANES] = h_fin[c][1]

        lhs = jnp.concatenate([hb_scr[c].astype(_BF16) for c in range(2 * S5_SUB_SLABS)] + [u_blk], axis=-1)
        y_sub = _dot(lhs, wy_ref[sub])
        if sub % 2 == 0:
            y_prev = y_sub
        else:
            pieces = [y[:, k * LANES:(k + 1) * LANES] for k in range(S5_BLOCK // 2) for y in (y_prev, y_sub)]
            for first, y_rows in enumerate(regroup_halves(pieces)):
                for k in range(S5_BLOCK // 2):
                    ys_scr[sub // 2, pl.ds(2 * k + first, n_buf, stride=S5_BLOCK), :] = (
                        y_rows[:, k * LANES:(k + 1) * LANES])

    def glu_in(i):
        ys = (_load_slabs(ys_scr, i, OUT_TILE, t_len, seq_rows)
              + d_skip * _load_slabs(us_scr, i, OUT_TILE, t_len, seq_rows))
        ys = _gelu(ys)
        return ys, _dot(ys.astype(_BF16), w_glu_ref[...])

    nxt = glu_in(0)
    for i in range(n_out_tiles):
        ys, gate = nxt
        if i + 1 < n_out_tiles:
            nxt = glu_in(i + 1)
        ys = ys * jax.nn.sigmoid(gate + b_glu)
        ys = _rmsnorm(ys, g_out_s).astype(_BF16)
        y = (_dot(ys, w_out_ref[:SSM_WIDTH, :])
             + _dot(yg_scr[i * OUT_TILE:(i + 1) * OUT_TILE, :], w_out_ref[SSM_WIDTH:, :]))
        _store_tile_rows(out_ref, i, OUT_TILE, _tile_rows(x_ref, i, OUT_TILE) + y)


def _layer_spec(arr, layer):
    nd = arr.ndim - 1
    return pl.BlockSpec((None,) + arr.shape[1:], lambda j: (layer,) + (0,) * nd,
                        pipeline_mode=pl.Buffered(1))


MIXER_WEIGHTS = ('mixer_vecs', 'w_in', 'a_blk', 'w_v', 'w_y', 'w_glu', 'mixw', 'mixb', 'w_out')
FFN_WEIGHTS = ('ffn_vecs', 'w_up', 'w_down', 'w_ple_gate', 'w_ple')


def _weights(w, names, layer):
    picked = [w[k] if isinstance(w[k], tuple) else (w[k], layer) for k in names]
    return [a for a, _ in picked], [_layer_spec(a, l) for a, l in picked]


def _cast_specs(cast, steps, blk_of):
    arrs, in_specs, out_specs, out_shape = [], [], [], []
    for a, layer in cast:
        _, r, c = a.shape
        rb = r // steps
        assert rb * steps == r and rb % (2 * SUBLANES) == 0
        arrs.append(a)
        in_specs.append(pl.BlockSpec((None, rb, c), lambda j, layer=layer: (layer, blk_of(j), 0)))
        out_specs.append(pl.BlockSpec((rb, c), lambda j: (blk_of(j), 0)))
        out_shape.append(jax.ShapeDtypeStruct((r, c), _BF16))
    return arrs, in_specs, out_specs, out_shape


def _mixer(x, h0r, h0i, w, layer, *, h0_layer, nseq, t_len, blk_pitch, keep_v, cast=()):
    nb, length, _ = x.shape
    t_blk = t_len if nb == nseq else nseq * t_len
    rows = nseq * t_len
    steps = length // t_blk
    slab_rows = nseq * S5_BLOCK * blk_pitch
    x_spec = pl.BlockSpec((nb, t_blk, D_MODEL), lambda j: (0, j, 0))
    st_spec = pl.BlockSpec((nseq, SSM_FLAT), lambda j: (0, 0))
    names = [k + ('_s' if keep_v else '_p') if k in ('mixw', 'mixb') else k for k in MIXER_WEIGHTS]
    consts, const_specs = _weights(w, names, layer)
    assert len(consts) + 3 == MIXER_INPUTS
    cast_arrs, cast_in, cast_out, cast_shape = _cast_specs(cast, steps, lambda j: j)
    out_specs = [x_spec, st_spec, st_spec]
    out_shape = [jax.ShapeDtypeStruct(x.shape, _F32),
                 jax.ShapeDtypeStruct((nseq, SSM_FLAT), _F32),
                 jax.ShapeDtypeStruct((nseq, SSM_FLAT), _F32)]
    scratch = [pltpu.VMEM((S5_SLICES, slab_rows, LANES), _F32),
               pltpu.VMEM((S5_SLICES, slab_rows, LANES), _F32),
               pltpu.VMEM((rows, GMLP_WIDTH), _BF16),
               pltpu.VMEM((2, 2 * S5_SUB_SLABS, nseq * blk_pitch, LANES), _F32)]
    if keep_v:
        out_specs.append(pl.BlockSpec((rows, GMLP_WIDTH), lambda j: (j, 0)))
        out_shape.append(jax.ShapeDtypeStruct((steps * rows, GMLP_WIDTH), _F32))
    res = pl.pallas_call(
        functools.partial(_mixer_kernel, nseq=nseq, t_len=t_len, blk_pitch=blk_pitch, keep_v=keep_v,
                          n_cast=len(cast_arrs)),
        grid=(steps,),
        in_specs=([x_spec, _layer_spec(h0r, h0_layer), _layer_spec(h0i, h0_layer)] + const_specs + cast_in),
        out_specs=tuple(out_specs + cast_out),
        out_shape=tuple(out_shape + cast_shape),
        scratch_shapes=scratch,
        compiler_params=pltpu.CompilerParams(dimension_semantics=("arbitrary",),
                                             vmem_limit_bytes=MIXER_VMEM_BYTES),
        name="mixer_sample" if keep_v else "mixer_prompt",
    )(x, h0r, h0i, *consts, *cast_arrs)
    n_out = 3 + int(keep_v)
    out, sre, sim = res[:3]
    return out, sre, sim, (res[3] if keep_v else None), [(a[None], 0) for a in res[n_out:]]


FFN_INPUTS = 9


def _ffn_kernel(*refs, final, prompt_steps, n_cast):
    hp_ref, hs_ref, pp_ref, ps_ref, vec_ref, w_up_ref, w_down_ref, w_gate_ref, w_ple_ref = refs[:FFN_INPUTS]
    n_in = FFN_INPUTS + n_cast
    outp_ref, outs_ref = refs[n_in:n_in + 2]
    _cast_rows(refs[FFN_INPUTS:n_in], refs[n_in + 2:])
    g_ffn, g_ple, g_fin = (_vec_row(vec_ref, FFN_VECS, n, D_MODEL) for n in FFN_VECS)

    def rows_block(h_ref, p_ref, out_ref):
        hr = h_ref.shape[0] // FFN_SPLIT
        parts = [slice(k * hr, (k + 1) * hr) for k in range(FFN_SPLIT)]
        accs = [h_ref[rs, :] for rs in parts]
        fs = [_rmsnorm(h, g_ffn).astype(_BF16) for h in accs]
        for c in range(D_FF // FFN_COLS):
            for k in range(FFN_SPLIT):
                up = _dot(fs[k], w_up_ref[:, c * FFN_COLS:(c + 1) * FFN_COLS])
                act = jnp.square(jnp.maximum(up, 0.0)).astype(_BF16)
                accs[k] = accs[k] + _dot(act, w_down_ref[c * FFN_COLS:(c + 1) * FFN_COLS, :])
        for k, rs in enumerate(parts):
            acc = accs[k]
            gate = jax.nn.sigmoid(_dot(_rmsnorm(acc, g_ple).astype(_BF16), w_gate_ref[...]))
            out = acc + gate * _dot(p_ref[rs, :].astype(_BF16), w_ple_ref[...])
            if final:
                out = _rmsnorm(out, g_fin)
            out_ref[rs, :] = out

    is_prompt = pl.program_id(0) < prompt_steps
    pl.when(is_prompt)(functools.partial(rows_block, hp_ref, pp_ref, outp_ref))
    pl.when(jnp.logical_not(is_prompt))(functools.partial(rows_block, hs_ref, ps_ref, outs_ref))


def _ffn(h_p, h_s, p_p, p_s, w, layer, *, final, cast=()):
    rows_s = min(FFN_ROWS, h_s.shape[0])
    steps_p = h_p.shape[0] // FFN_ROWS
    steps_s = h_s.shape[0] // rows_s
    consts, const_specs = _weights(w, FFN_WEIGHTS, layer)
    assert len(consts) + 4 == FFN_INPUTS
    prompt_blk = lambda j: jnp.minimum(j, steps_p - 1)
    sample_blk = lambda j: jnp.maximum(j - steps_p, 0)
    cast_arrs, cast_in, cast_out, cast_shape = _cast_specs(cast, steps_p, prompt_blk)
    res = pl.pallas_call(
        functools.partial(_ffn_kernel, final=final, prompt_steps=steps_p, n_cast=len(cast_arrs)),
        grid=(steps_p + steps_s,),
        in_specs=([pl.BlockSpec((FFN_ROWS, D_MODEL), lambda j: (prompt_blk(j), 0)),
                   pl.BlockSpec((rows_s, D_MODEL), lambda j: (sample_blk(j), 0),
                                pipeline_mode=pl.Buffered(1)),
                   pl.BlockSpec((None, FFN_ROWS, PLE_DIM), lambda j: (layer, prompt_blk(j), 0)),
                   pl.BlockSpec((None, rows_s, PLE_DIM), lambda j: (layer, sample_blk(j), 0),
                                pipeline_mode=pl.Buffered(1))]
                  + const_specs + cast_in),
        out_specs=tuple([pl.BlockSpec((FFN_ROWS, D_MODEL), lambda j: (prompt_blk(j), 0)),
                         pl.BlockSpec((rows_s, D_MODEL), lambda j: (sample_blk(j), 0))] + cast_out),
        out_shape=tuple([jax.ShapeDtypeStruct(h_p.shape, _F32), jax.ShapeDtypeStruct(h_s.shape, _F32)]
                        + cast_shape),
        compiler_params=pltpu.CompilerParams(dimension_semantics=("arbitrary",),
                                             vmem_limit_bytes=FFN_VMEM_BYTES),
        name="ffn",
    )(h_p, h_s, p_p, p_s, *consts, *cast_arrs)
    return res[0], res[1], [(a[None], 0) for a in res[2:]]


def kernel(x_prompt, x_sample, state_ssm_re, state_ssm_im, p_prompt, p_sample, g_mix, w_in, lam_re, lam_im, log_dt, b_re, b_im, c_re, c_im, d_skip, w_glu, b_glu, g_v, b_v, w_s, b_s, g_out_s, g_out_g, w_out, g_ffn, w_up, w_down, g_ple, w_ple_gate, w_ple, g_final):
    batch, seq, _ = x_prompt.shape
    dec_batch, dec_seq, _ = x_sample.shape
    n_p, n_s = batch * seq, dec_batch * dec_seq
    assert seq % CHUNK == 0 and batch == SUBLANES
    assert dec_seq % S5_BLOCK == 0 and CHUNK % dec_seq == 0 and n_s % max(IN_TILE, OUT_TILE) == 0

    mixer_mats = dict(w_in=w_in, w_glu=w_glu, w_out=w_out)
    ffn_mats = dict(w_up=w_up, w_down=w_down, w_ple_gate=w_ple_gate, w_ple=w_ple)
    abr, abi, bbr, bbi = _discretise(lam_re, lam_im, log_dt, b_re, b_im)
    w_v, w_y, a_blk, cast = _s5_matrices(abr, abi, bbr, bbi, c_re, c_im,
                                         cast=[(a, 0) for a in mixer_mats.values()])

    def pack(vecs):
        return jnp.stack([jnp.pad(v, ((0, 0), (0, D_MODEL - v.shape[-1]))) for v in vecs], axis=1)
    head_of = (jnp.arange(GMLP_WIDTH)[None, :] // GMLP_HEAD_DIM == jnp.arange(GMLP_HEADS)[:, None]).astype(_F32)
    pos_of = (jnp.arange(CHUNK)[:, None] % dec_seq == jnp.arange(dec_seq)[None, :]).astype(_F32)
    w_s4 = w_s[:, :, :dec_seq, :dec_seq]
    w = dict(
        mixer_vecs=pack([g_mix, d_skip, b_glu, g_v, b_v, g_out_s, g_out_g]),
        ffn_vecs=pack([g_ffn, g_ple, jnp.broadcast_to(g_final, (DEPTH, D_MODEL))]),
        a_blk=a_blk, w_v=w_v, w_y=w_y,
        mixw_p=w_s,
        mixb_p=jnp.einsum('dht,hc->dtc', b_s, head_of, precision=_EXACT),
        mixw_s=jnp.einsum('rt,dhts,cs->dhrc', pos_of, w_s4, pos_of, precision=_EXACT),
        mixb_s=jnp.einsum('dht,rt,hc->drc', b_s[:, :, :dec_seq], pos_of, head_of, precision=_EXACT))

    zeros = jnp.zeros((1, batch, SSM_FLAT), _F32)
    s0_re = state_ssm_re.reshape(DEPTH, dec_batch, SSM_FLAT)
    s0_im = state_ssm_im.reshape(DEPTH, dec_batch, SSM_FLAT)
    pp = p_prompt.reshape(DEPTH, n_p, PLE_DIM)
    ps = p_sample.reshape(DEPTH, n_s, PLE_DIM)
    h_p = x_prompt
    h_s = x_sample.reshape(1, n_s, D_MODEL)
    re_p, im_p, re_s, im_s, v_s = [], [], [], [], []
    w.update(zip(mixer_mats, cast))
    for i in range(DEPTH):
        final = i == DEPTH - 1
        h_p, sre, sim, _, cast = _mixer(h_p, zeros, zeros, w, i, h0_layer=0, nseq=batch, t_len=CHUNK,
                                        blk_pitch=CHUNK // S5_BLOCK + SCAN_PAD_BLOCKS, keep_v=False,
                                        cast=[(a, i) for a in ffn_mats.values()])
        w.update(zip(ffn_mats, cast))
        re_p.append(sre)
        im_p.append(sim)
        h_s, sre, sim, vn, _ = _mixer(h_s, s0_re, s0_im, w, i, h0_layer=i, nseq=dec_batch, t_len=dec_seq,
                                      blk_pitch=dec_seq // S5_BLOCK, keep_v=True)
        re_s.append(sre)
        im_s.append(sim)
        v_s.append(vn)
        h_p, h_s, cast = _ffn(h_p.reshape(n_p, D_MODEL), h_s.reshape(n_s, D_MODEL), pp, ps, w, i, final=final,
                              cast=[] if final else [(a, i + 1) for a in mixer_mats.values()])
        w.update(zip(mixer_mats, cast))
        h_p = h_p.reshape(batch, seq, D_MODEL)
        h_s = h_s.reshape(1, n_s, D_MODEL)

    st_p = (DEPTH, batch, SSM_GROUPS, SSM_STATE)
    st_s = (DEPTH, dec_batch, SSM_GROUPS, SSM_STATE)
    return (h_p, h_s.reshape(dec_batch, dec_seq, D_MODEL),
            jnp.stack(re_p).reshape(st_p), jnp.stack(im_p).reshape(st_p),
            jnp.stack(re_s).reshape(st_s), jnp.stack(im_s).reshape(st_s),
            jnp.stack(v_s).reshape(DEPTH, dec_batch, dec_seq, GMLP_WIDTH))
```

```python
import functools
import math

import jax
import jax.numpy as jnp
from jax import lax
from jax.experimental import pallas as pl
from jax.experimental.pallas import tpu as pltpu

D_MODEL = 1024
DEPTH = 2
SSM_WIDTH = 512
GMLP_WIDTH = 512
SSM_GROUP = 16
SSM_GROUPS = 32
SSM_STATE = 64
SSM_FLAT = SSM_GROUPS * SSM_STATE
CHUNK = 128
GMLP_HEADS = 4
GMLP_HEAD_DIM = 128
PLE_DIM = 256
D_FF = 4096
IN_WIDTH = SSM_WIDTH + 2 * GMLP_WIDTH
EPS = 1e-6
LAM_RE_MAX = -1e-4

LANES = 128
SUBLANES = 8
S5_SLICES = SSM_WIDTH // LANES
S5_SUBS = 2 * S5_SLICES
S5_SUB_CH = SSM_WIDTH // S5_SUBS
S5_SUB_STATES = SSM_FLAT // S5_SUBS
S5_SUB_SLABS = S5_SUB_STATES // LANES
S5_BLOCK = 4
SCAN_PAD_BLOCKS = 4
IN_TILE = 128
OUT_TILE = 256
FFN_ROWS = 512
FFN_COLS = 1024
FFN_SPLIT = 2
MIXER_VMEM_BYTES = 56 * 1024 * 1024
FFN_VMEM_BYTES = 52 * 1024 * 1024

_F32 = jnp.float32
_BF16 = jnp.bfloat16
_EXACT = lax.Precision.HIGHEST


def _dot(a, b):
    return jnp.dot(a, b, preferred_element_type=_F32)


def _rmsnorm(x, g):
    r = lax.rsqrt(jnp.mean(x * x, axis=-1, keepdims=True) + EPS)
    return x * r * g


def _gelu(x):
    c = math.sqrt(2.0 / math.pi)
    return x * (0.5 * (1.0 + jnp.tanh(c * (x + 0.044715 * (x * x * x)))))


def _disc_kernel(lr_ref, li_ref, ldt_ref, br_ref, bi_ref, abr_ref, abi_ref, bbr_ref, bbi_ref):
    lr = jnp.minimum(lr_ref[...], LAM_RE_MAX)
    li = li_ref[...]
    dt = jnp.exp(ldt_ref[...])
    mag = jnp.exp(lr * dt)
    abr = mag * jnp.cos(li * dt)
    abi = mag * jnp.sin(li * dt)
    den = lr * lr + li * li
    nr = abr - 1.0
    ni = abi
    qr = (nr * lr + ni * li) / den
    qi = (ni * lr - nr * li) / den
    abr_ref[...] = abr
    abi_ref[...] = abi
    for h in range(SSM_GROUP):
        br = br_ref[h]
        bi = bi_ref[h]
        bbr_ref[h] = qr * br - qi * bi
        bbi_ref[h] = qr * bi + qi * br


def _discretise(lam_re, lam_im, log_dt, b_re, b_im):
    dg = DEPTH * SSM_GROUPS
    lr = lam_re.reshape(dg, SSM_STATE)
    li = lam_im.reshape(dg, SSM_STATE)
    ldt = log_dt.reshape(dg, 1)
    br = jnp.moveaxis(b_re, -1, 0).reshape(SSM_GROUP, dg, SSM_STATE)
    bi = jnp.moveaxis(b_im, -1, 0).reshape(SSM_GROUP, dg, SSM_STATE)
    vec = jax.ShapeDtypeStruct((dg, SSM_STATE), _F32)
    mat = jax.ShapeDtypeStruct((SSM_GROUP, dg, SSM_STATE), _F32)
    abr, abi, bbr, bbi = pl.pallas_call(
        _disc_kernel, out_shape=(vec, vec, mat, mat), name="s5_discretise")(lr, li, ldt, br, bi)
    vshp = (DEPTH, SSM_GROUPS, SSM_STATE)
    return abr.reshape(vshp), abi.reshape(vshp), bbr, bbi


def _cmul(a, b):
    return a[0] * b[0] - a[1] * b[1], a[0] * b[1] + a[1] * b[0]


PREPARE_INPUTS = 6


def _prepare_kernel(*refs):
    a_ref, bb_ref, cre_ref, cim_ref, t_ref, tt_ref = refs[:PREPARE_INPUTS]
    n_cast = (len(refs) - PREPARE_INPUTS - 2) // 2
    wv_ref, wy_ref = refs[PREPARE_INPUTS + n_cast:PREPARE_INPUTS + n_cast + 2]
    _cast_rows(refs[PREPARE_INPUTS:PREPARE_INPUTS + n_cast], refs[PREPARE_INPUTS + n_cast + 2:])
    r = S5_BLOCK
    ch = S5_SUB_CH
    n_st = S5_SUB_STATES
    g_shift = SSM_GROUP.bit_length() - 1
    p_shift = SSM_STATE.bit_length() - 1

    def group_mask(n_rows, row_shift, n_cols, col_width, col_shift):
        rg = lax.broadcasted_iota(jnp.int32, (n_rows, n_cols), 0) >> row_shift
        cg = (lax.broadcasted_iota(jnp.int32, (n_rows, n_cols), 1) & (col_width - 1)) >> col_shift
        return rg == cg

    mask_v = group_mask(ch, g_shift, n_st, n_st, p_shift)
    mask_s = group_mask(n_st, p_shift, r * ch, ch, g_shift)
    mask_u = group_mask(ch, g_shift, r * ch, ch, g_shift)

    def dot_t(x, y, precision=None):
        return lax.dot_general(x, y, (((1,), (1,)), ((), ())), precision=precision,
                               preferred_element_type=_F32)

    def expand(x):
        return jnp.where(mask_v, _dot(x.astype(_BF16), t_ref[...]), 0.0)

    for sub in range(S5_SUBS):
        rows = slice(sub * ch, (sub + 1) * ch)
        a1 = (a_ref[0, rows, :], a_ref[1, rows, :])
        bb = (bb_ref[0, rows, :], bb_ref[1, rows, :])
        cc = (cre_ref[rows, :], cim_ref[rows, :])
        apow = [None, a1]
        for _ in range(r - 1):
            apow.append(_cmul(apow[-1], a1))
        ab = [bb] + [_cmul(apow[k], bb) for k in range(1, r)]
        ca = [cc] + [_cmul(cc, apow[k]) for k in range(1, r + 1)]
        for m in range(r):
            band = jnp.concatenate([expand(ab[r - 1 - m][0]), expand(ab[r - 1 - m][1])], axis=-1)
            wv_ref[sub, m * ch:(m + 1) * ch, :] = band.astype(_BF16)
        for part in range(2):
            ca_all = jnp.concatenate([ca[i + 1][part] for i in range(r)], axis=0).astype(_BF16)
            band = jnp.where(mask_s, dot_t(tt_ref[...], ca_all), 0.0)
            wy_ref[sub, part * n_st:(part + 1) * n_st, :] = (band if part == 0 else -band).astype(_BF16)
        none = jnp.zeros((ch, SSM_STATE), _F32)
        for m in range(r):
            shifted = [jnp.concatenate([ca[i - m][part] if i >= m else none for i in range(r)], axis=0)
                       for part in range(2)]
            band = dot_t(bb[0], shifted[0], _EXACT) - dot_t(bb[1], shifted[1], _EXACT)
            wy_ref[sub, 2 * n_st + m * ch:2 * n_st + (m + 1) * ch, :] = (
                jnp.where(mask_u, band, 0.0).astype(_BF16))


def _s5_matrices(abr, abi, bbr, bbi, c_re, c_im, cast=()):
    r = S5_BLOCK
    n_rows = SSM_GROUPS * SSM_GROUP
    a_rep = jnp.repeat(jnp.stack([abr, abi], axis=1), SSM_GROUP, axis=2)
    bb = jnp.stack([bbr, bbi]).reshape(2, SSM_GROUP, DEPTH, SSM_GROUPS, SSM_STATE)
    bb = jnp.transpose(bb, (2, 0, 3, 1, 4)).reshape(DEPTH, 2, n_rows, SSM_STATE)
    cre = c_re.reshape(DEPTH, n_rows, SSM_STATE)
    cim = c_im.reshape(DEPTH, n_rows, SSM_STATE)
    t_state = jnp.tile(jnp.eye(SSM_STATE, dtype=_BF16), (1, SSM_GROUPS // S5_SUBS))

    def per_layer(a):
        nd = a.ndim - 1
        return pl.BlockSpec((None,) + a.shape[1:], lambda d: (d,) + (0,) * nd)

    v_shape = (DEPTH, S5_SUBS, r * S5_SUB_CH, 2 * S5_SUB_STATES)
    y_shape = (DEPTH, S5_SUBS, 2 * S5_SUB_STATES + r * S5_SUB_CH, r * S5_SUB_CH)
    cast_arrs, cast_in, cast_out, cast_shape = _cast_specs(cast, DEPTH, lambda d: d)
    w_v, w_y, *cast_res = pl.pallas_call(
        _prepare_kernel,
        grid=(DEPTH,),
        in_specs=[per_layer(a_rep), per_layer(bb), per_layer(cre), per_layer(cim),
                  pl.BlockSpec(t_state.shape, lambda d: (0, 0)),
                  pl.BlockSpec(t_state.shape[::-1], lambda d: (0, 0))] + cast_in,
        out_specs=tuple([pl.BlockSpec((None,) + v_shape[1:], lambda d: (d, 0, 0, 0)),
                         pl.BlockSpec((None,) + y_shape[1:], lambda d: (d, 0, 0, 0))] + cast_out),
        out_shape=tuple([jax.ShapeDtypeStruct(v_shape, _BF16), jax.ShapeDtypeStruct(y_shape, _BF16)]
                        + cast_shape),
        name="s5_prepare",
    )(a_rep, bb, cre, cim, t_state, t_state.T, *cast_arrs)
    a_pow = (abr, abi)
    for _ in range(r - 1):
        a_pow = _cmul(a_pow, (abr, abi))
    a_blk = jnp.stack(a_pow, axis=1).reshape(DEPTH, 2, SSM_FLAT)
    return w_v, w_y, a_blk, [(a[None], 0) for a in cast_res]


def _tile_rows(ref, i, rows):
    nb, tb, c = ref.shape
    if nb == 1:
        return ref[0, i * rows:(i + 1) * rows, :]
    per = rows // tb
    return ref[i * per:(i + 1) * per].reshape(rows, c)


def _store_tile_rows(ref, i, rows, val):
    nb, tb, c = ref.shape
    if nb == 1:
        ref[0, i * rows:(i + 1) * rows, :] = val
    else:
        per = rows // tb
        ref[i * per:(i + 1) * per] = val.reshape(per, tb, c)


def _load_slabs(ref, i, rows, t_len, seq_rows):
    if seq_rows == t_len:
        return jnp.concatenate([ref[s, i * rows:(i + 1) * rows, :] for s in range(S5_SLICES)], axis=-1)
    per = rows // t_len
    return jnp.concatenate(
        [jnp.concatenate([ref[s, n * seq_rows:n * seq_rows + t_len, :] for s in range(S5_SLICES)], axis=-1)
         for n in range(i * per, (i + 1) * per)], axis=0)


def _store_slabs(ref, i, rows, t_len, seq_rows, val):
    for s in range(S5_SLICES):
        cols = slice(s * LANES, (s + 1) * LANES)
        if seq_rows == t_len:
            ref[s, i * rows:(i + 1) * rows, :] = val[:, cols]
        else:
            per = rows // t_len
            for k in range(per):
                n = i * per + k
                ref[s, n * seq_rows:n * seq_rows + t_len, :] = val[k * t_len:(k + 1) * t_len, cols]


MIXER_INPUTS = 12
MIXER_VECS = ('g_mix', 'd_skip', 'b_glu', 'g_v', 'b_v', 'g_out_s', 'g_out_g')
FFN_VECS = ('g_ffn', 'g_ple', 'g_final')


def _vec_row(ref, names, name, width):
    k = names.index(name)
    return ref[k:k + 1, :width]


def _cast_rows(in_refs, out_refs):
    for src, dst in zip(in_refs, out_refs):
        dst[...] = src[...].astype(_BF16)


def _mixer_kernel(*refs, nseq, t_len, blk_pitch, keep_v, n_cast):
    (x_ref, h0r_ref, h0i_ref, vec_ref, w_in_ref, a_blk_ref, wv_ref, wy_ref,
     w_glu_ref, mixw_ref, mixb_ref, w_out_ref) = refs[:MIXER_INPUTS]
    g_mix, d_skip, b_glu, g_v, b_v, g_out_s, g_out_g = (
        _vec_row(vec_ref, MIXER_VECS, n, D_MODEL if n == 'g_mix' else SSM_WIDTH) for n in MIXER_VECS)
    n_in = MIXER_INPUTS + n_cast
    out_ref, sre_ref, sim_ref = refs[n_in:n_in + 3]
    n_out = 3 + int(keep_v)
    if keep_v:
        vn_ref = refs[n_in + 3]
    us_scr, ys_scr, yg_scr, hb2_scr = refs[n_in + n_out + n_cast:]
    rows = nseq * t_len
    n_in_tiles = rows // IN_TILE
    n_out_tiles = rows // OUT_TILE
    n_blk = t_len // S5_BLOCK
    seq_rows = S5_BLOCK * blk_pitch
    n_buf = nseq * blk_pitch

    @pl.when(pl.program_id(0) == 0)
    def _():
        sre_ref[...] = h0r_ref[...]
        sim_ref[...] = h0i_ref[...]
        us_scr[...] = jnp.zeros_like(us_scr)

    row = lax.broadcasted_iota(jnp.int32, (CHUNK, CHUNK), 0)
    col = lax.broadcasted_iota(jnp.int32, (CHUNK, CHUNK), 1)
    t_shift = t_len.bit_length() - 1
    causal = ((row >> t_shift) == (col >> t_shift)) & ((col & (t_len - 1)) <= (row & (t_len - 1)))
    mixw = [jnp.where(causal, mixw_ref[h], 0.0).astype(_BF16) for h in range(GMLP_HEADS)]

    def in_proj(i):
        x = _tile_rows(x_ref, i, IN_TILE)
        a = _rmsnorm(x, g_mix).astype(_BF16)
        return _dot(a, w_in_ref[...])

    lane_lo = lax.broadcasted_iota(jnp.int32, (n_buf, LANES), 1) < S5_SUB_CH

    def regroup_halves(blocks):
        rolled = [pltpu.roll(x, S5_SUB_CH, axis=1) for x in blocks]
        pairs = range(0, len(blocks), 2)
        lo = [jnp.where(lane_lo, blocks[k], rolled[k + 1]) for k in pairs]
        hi = [jnp.where(lane_lo, rolled[k], blocks[k + 1]) for k in pairs]
        return jnp.concatenate(lo, axis=-1), jnp.concatenate(hi, axis=-1)

    def block_inputs(sub, u_halves):
        if sub % 2 == 0:
            s = sub // 2
            u_halves = regroup_halves([us_scr[s, pl.ds(m, n_buf, stride=S5_BLOCK), :]
                                       for m in range(S5_BLOCK)])
        u_blk = u_halves[sub % 2].astype(_BF16)
        v_in = _dot(u_blk, wv_ref[sub])
        for c in range(2 * S5_SUB_SLABS):
            hb2_scr[sub % 2, c] = v_in[:, c * LANES:(c + 1) * LANES]
        return u_blk, u_halves

    z_next = in_proj(0)
    for i in range(n_in_tiles):
        z = z_next
        if i + 1 < n_in_tiles:
            z_next = in_proj(i + 1)
        _store_slabs(us_scr, i, IN_TILE, t_len, seq_rows, z[:, :SSM_WIDTH])
        if i + 1 == n_in_tiles:
            u_next, u_halves = block_inputs(0, None)
        zg = _gelu(z[:, SSM_WIDTH:])
        ug = zg[:, :GMLP_WIDTH]
        v = zg[:, GMLP_WIDTH:]
        mu = jnp.mean(v, axis=-1, keepdims=True)
        vc = v - mu
        r = lax.rsqrt(jnp.mean(vc * vc, axis=-1, keepdims=True) + EPS)
        vn = vc * r * g_v + b_v
        if keep_v:
            vn_ref[i * IN_TILE:(i + 1) * IN_TILE, :] = vn
        vn = vn.astype(_BF16)
        for b in range(IN_TILE // CHUNK):
            rb = slice(b * CHUNK, (b + 1) * CHUNK)
            mixed = [_dot(mixw[h], vn[rb, h * GMLP_HEAD_DIM:(h + 1) * GMLP_HEAD_DIM])
                     for h in range(GMLP_HEADS)]
            m = jnp.concatenate(mixed, axis=-1) + mixb_ref[...]
            yg = _rmsnorm(ug[rb, :] * m, g_out_g).astype(_BF16)
            yg_scr[i * IN_TILE + b * CHUNK:i * IN_TILE + (b + 1) * CHUNK, :] = yg

    _cast_rows(refs[MIXER_INPUTS:n_in], refs[n_in + n_out:n_in + n_out + n_cast])
    y_prev = None
    for sub in range(S5_SUBS):
        u_blk = u_next
        if sub + 1 < S5_SUBS:
            u_next, u_halves = block_inputs(sub + 1, u_halves)
        hb_scr = hb2_scr.at[sub % 2]
        s0 = sub * S5_SUB_STATES
        a_re = [jnp.broadcast_to(a_blk_ref[0:1, s0 + c * LANES:s0 + (c + 1) * LANES], (nseq, LANES))
                for c in range(S5_SUB_SLABS)]
        a_im = [jnp.broadcast_to(a_blk_ref[1:2, s0 + c * LANES:s0 + (c + 1) * LANES], (nseq, LANES))
                for c in range(S5_SUB_SLABS)]
        h_init = tuple(
            (sre_ref[:, s0 + c * LANES:s0 + (c + 1) * LANES], sim_ref[:, s0 + c * LANES:s0 + (c + 1) * LANES])
            for c in range(S5_SUB_SLABS))

        def scan_step(j, h):
            new = []
            for c in range(S5_SUB_SLABS):
                rows_j = pl.ds(j, nseq, stride=blk_pitch)
                hr, hi = h[c]
                v_re = hb_scr[c, rows_j, :]
                v_im = hb_scr[S5_SUB_SLABS + c, rows_j, :]
                hb_scr[c, rows_j, :] = hr
                hb_scr[S5_SUB_SLABS + c, rows_j, :] = hi
                new.append((a_re[c] * hr - a_im[c] * hi + v_re, a_re[c] * hi + a_im[c] * hr + v_im))
            return tuple(new)
        h_fin = h_init
        for j in range(n_blk):
            h_fin = scan_step(j, h_fin)
        for c in range(S5_SUB_SLABS):
            sre_ref[:, s0 + c * LANES:s0 + (c + 1) * LANES] = h_fin[c][0]
            sim_ref[:, s0 + c * LANES:s0 + (c + 1) * LANES] = h_fin[c][1]

        lhs = jnp.concatenate([hb_scr[c].astype(_BF16) for c in range(2 * S5_SUB_SLABS)] + [u_blk], axis=-1)
        y_sub = _dot(lhs, wy_ref[sub])
        if sub % 2 == 0:
            y_prev = y_sub
        else:
            pieces = [y[:, k * LANES:(k + 1) * LANES] for k in range(S5_BLOCK // 2) for y in (y_prev, y_sub)]
            for first, y_rows in enumerate(regroup_halves(pieces)):
                for k in range(S5_BLOCK // 2):
                    ys_scr[sub // 2, pl.ds(2 * k + first, n_buf, stride=S5_BLOCK), :] = (
                        y_rows[:, k * LANES:(k + 1) * LANES])

    def glu_in(i):
        ys = (_load_slabs(ys_scr, i, OUT_TILE, t_len, seq_rows)
              + d_skip * _load_slabs(us_scr, i, OUT_TILE, t_len, seq_rows))
        ys = _gelu(ys)
        return ys, _dot(ys.astype(_BF16), w_glu_ref[...])

    nxt = glu_in(0)
    for i in range(n_out_tiles):
        ys, gate = nxt
        if i + 1 < n_out_tiles:
            nxt = glu_in(i + 1)
        ys = ys * jax.nn.sigmoid(gate + b_glu)
        ys = _rmsnorm(ys, g_out_s).astype(_BF16)
        y = (_dot(ys, w_out_ref[:SSM_WIDTH, :])
             + _dot(yg_scr[i * OUT_TILE:(i + 1) * OUT_TILE, :], w_out_ref[SSM_WIDTH:, :]))
        _store_tile_rows(out_ref, i, OUT_TILE, _tile_rows(x_ref, i, OUT_TILE) + y)


def _layer_spec(arr, layer):
    nd = arr.ndim - 1
    return pl.BlockSpec((None,) + arr.shape[1:], lambda j: (layer,) + (0,) * nd,
                        pipeline_mode=pl.Buffered(1))


MIXER_WEIGHTS = ('mixer_vecs', 'w_in', 'a_blk', 'w_v', 'w_y', 'w_glu', 'mixw', 'mixb', 'w_out')
FFN_WEIGHTS = ('ffn_vecs', 'w_up', 'w_down', 'w_ple_gate', 'w_ple')


def _weights(w, names, layer):
    picked = [w[k] if isinstance(w[k], tuple) else (w[k], layer) for k in names]
    return [a for a, _ in picked], [_layer_spec(a, l) for a, l in picked]


def _cast_specs(cast, steps, blk_of):
    arrs, in_specs, out_specs, out_shape = [], [], [], []
    for a, layer in cast:
        _, r, c = a.shape
        rb = r // steps
        assert rb * steps == r and rb % (2 * SUBLANES) == 0
        arrs.append(a)
        in_specs.append(pl.BlockSpec((None, rb, c), lambda j, layer=layer: (layer, blk_of(j), 0)))
        out_specs.append(pl.BlockSpec((rb, c), lambda j: (blk_of(j), 0)))
        out_shape.append(jax.ShapeDtypeStruct((r, c), _BF16))
    return arrs, in_specs, out_specs, out_shape


def _mixer(x, h0r, h0i, w, layer, *, h0_layer, nseq, t_len, blk_pitch, keep_v, cast=()):
    nb, length, _ = x.shape
    t_blk = t_len if nb == nseq else nseq * t_len
    rows = nseq * t_len
    steps = length // t_blk
    slab_rows = nseq * S5_BLOCK * blk_pitch
    x_spec = pl.BlockSpec((nb, t_blk, D_MODEL), lambda j: (0, j, 0))
    st_spec = pl.BlockSpec((nseq, SSM_FLAT), lambda j: (0, 0))
    names = [k + ('_s' if keep_v else '_p') if k in ('mixw', 'mixb') else k for k in MIXER_WEIGHTS]
    consts, const_specs = _weights(w, names, layer)
    assert len(consts) + 3 == MIXER_INPUTS
    cast_arrs, cast_in, cast_out, cast_shape = _cast_specs(cast, steps, lambda j: j)
    out_specs = [x_spec, st_spec, st_spec]
    out_shape = [jax.ShapeDtypeStruct(x.shape, _F32),
                 jax.ShapeDtypeStruct((nseq, SSM_FLAT), _F32),
                 jax.ShapeDtypeStruct((nseq, SSM_FLAT), _F32)]
    scratch = [pltpu.VMEM((S5_SLICES, slab_rows, LANES), _F32),
               pltpu.VMEM((S5_SLICES, slab_rows, LANES), _F32),
               pltpu.VMEM((rows, GMLP_WIDTH), _BF16),
               pltpu.VMEM((2, 2 * S5_SUB_SLABS, nseq * blk_pitch, LANES), _F32)]
    if keep_v:
        out_specs.append(pl.BlockSpec((rows, GMLP_WIDTH), lambda j: (j, 0)))
        out_shape.append(jax.ShapeDtypeStruct((steps * rows, GMLP_WIDTH), _F32))
    res = pl.pallas_call(
        functools.partial(_mixer_kernel, nseq=nseq, t_len=t_len, blk_pitch=blk_pitch, keep_v=keep_v,
                          n_cast=len(cast_arrs)),
        grid=(steps,),
        in_specs=([x_spec, _layer_spec(h0r, h0_layer), _layer_spec(h0i, h0_layer)] + const_specs + cast_in),
        out_specs=tuple(out_specs + cast_out),
        out_shape=tuple(out_shape + cast_shape),
        scratch_shapes=scratch,
        compiler_params=pltpu.CompilerParams(dimension_semantics=("arbitrary",),
                                             vmem_limit_bytes=MIXER_VMEM_BYTES),
        name="mixer_sample" if keep_v else "mixer_prompt",
    )(x, h0r, h0i, *consts, *cast_arrs)
    n_out = 3 + int(keep_v)
    out, sre, sim = res[:3]
    return out, sre, sim, (res[3] if keep_v else None), [(a[None], 0) for a in res[n_out:]]


FFN_INPUTS = 9


def _ffn_kernel(*refs, final, prompt_steps, n_cast):
    hp_ref, hs_ref, pp_ref, ps_ref, vec_ref, w_up_ref, w_down_ref, w_gate_ref, w_ple_ref = refs[:FFN_INPUTS]
    n_in = FFN_INPUTS + n_cast
    outp_ref, outs_ref = refs[n_in:n_in + 2]
    _cast_rows(refs[FFN_INPUTS:n_in], refs[n_in + 2:])
    g_ffn, g_ple, g_fin = (_vec_row(vec_ref, FFN_VECS, n, D_MODEL) for n in FFN_VECS)

    def rows_block(h_ref, p_ref, out_ref):
        hr = h_ref.shape[0] // FFN_SPLIT
        parts = [slice(k * hr, (k + 1) * hr) for k in range(FFN_SPLIT)]
        accs = [h_ref[rs, :] for rs in parts]
        fs = [_rmsnorm(h, g_ffn).astype(_BF16) for h in accs]
        for c in range(D_FF // FFN_COLS):
            for k in range(FFN_SPLIT):
                up = _dot(fs[k], w_up_ref[:, c * FFN_COLS:(c + 1) * FFN_COLS])
                act = jnp.square(jnp.maximum(up, 0.0)).astype(_BF16)
                accs[k] = accs[k] + _dot(act, w_down_ref[c * FFN_COLS:(c + 1) * FFN_COLS, :])
        for k, rs in enumerate(parts):
            acc = accs[k]
            gate = jax.nn.sigmoid(_dot(_rmsnorm(acc, g_ple).astype(_BF16), w_gate_ref[...]))
            out = acc + gate * _dot(p_ref[rs, :].astype(_BF16), w_ple_ref[...])
            if final:
                out = _rmsnorm(out, g_fin)
            out_ref[rs, :] = out

    is_prompt = pl.program_id(0) < prompt_steps
    pl.when(is_prompt)(functools.partial(rows_block, hp_ref, pp_ref, outp_ref))
    pl.when(jnp.logical_not(is_prompt))(functools.partial(rows_block, hs_ref, ps_ref, outs_ref))


def _ffn(h_p, h_s, p_p, p_s, w, layer, *, final, cast=()):
    rows_s = min(FFN_ROWS, h_s.shape[0])
    steps_p = h_p.shape[0] // FFN_ROWS
    steps_s = h_s.shape[0] // rows_s
    consts, const_specs = _weights(w, FFN_WEIGHTS, layer)
    assert len(consts) + 4 == FFN_INPUTS
    prompt_blk = lambda j: jnp.minimum(j, steps_p - 1)
    sample_blk = lambda j: jnp.maximum(j - steps_p, 0)
    cast_arrs, cast_in, cast_out, cast_shape = _cast_specs(cast, steps_p, prompt_blk)
    res = pl.pallas_call(
        functools.partial(_ffn_kernel, final=final, prompt_steps=steps_p, n_cast=len(cast_arrs)),
        grid=(steps_p + steps_s,),
        in_specs=([pl.BlockSpec((FFN_ROWS, D_MODEL), lambda j: (prompt_blk(j), 0)),
                   pl.BlockSpec((rows_s, D_MODEL), lambda j: (sample_blk(j), 0),
                                pipeline_mode=pl.Buffered(1)),
                   pl.BlockSpec((None, FFN_ROWS, PLE_DIM), lambda j: (layer, prompt_blk(j), 0)),
                   pl.BlockSpec((None, rows_s, PLE_DIM), lambda j: (layer, sample_blk(j), 0),
                                pipeline_mode=pl.Buffered(1))]
                  + const_specs + cast_in),
        out_specs=tuple([pl.BlockSpec((FFN_ROWS, D_MODEL), lambda j: (prompt_blk(j), 0)),
                         pl.BlockSpec((rows_s, D_MODEL), lambda j: (sample_blk(j), 0))] + cast_out),
        out_shape=tuple([jax.ShapeDtypeStruct(h_p.shape, _F32), jax.ShapeDtypeStruct(h_s.shape, _F32)]
                        + cast_shape),
        compiler_params=pltpu.CompilerParams(dimension_semantics=("arbitrary",),
                                             vmem_limit_bytes=FFN_VMEM_BYTES),
        name="ffn",
    )(h_p, h_s, p_p, p_s, *consts, *cast_arrs)
    return res[0], res[1], [(a[None], 0) for a in res[2:]]


def kernel(x_prompt, x_sample, state_ssm_re, state_ssm_im, p_prompt, p_sample, g_mix, w_in, lam_re, lam_im, log_dt, b_re, b_im, c_re, c_im, d_skip, w_glu, b_glu, g_v, b_v, w_s, b_s, g_out_s, g_out_g, w_out, g_ffn, w_up, w_down, g_ple, w_ple_gate, w_ple, g_final):
    batch, seq, _ = x_prompt.shape
    dec_batch, dec_seq, _ = x_sample.shape
    n_p, n_s = batch * seq, dec_batch * dec_seq
    assert seq % CHUNK == 0 and batch == SUBLANES
    assert dec_seq % S5_BLOCK == 0 and CHUNK % dec_seq == 0 and n_s % max(IN_TILE, OUT_TILE) == 0

    mixer_mats = dict(w_in=w_in, w_glu=w_glu, w_out=w_out)
    ffn_mats = dict(w_up=w_up, w_down=w_down, w_ple_gate=w_ple_gate, w_ple=w_ple)
    abr, abi, bbr, bbi = _discretise(lam_re, lam_im, log_dt, b_re, b_im)
    w_v, w_y, a_blk, cast = _s5_matrices(abr, abi, bbr, bbi, c_re, c_im,
                                         cast=[(a, 0) for a in mixer_mats.values()])

    def pack(vecs):
        return jnp.stack([jnp.pad(v, ((0, 0), (0, D_MODEL - v.shape[-1]))) for v in vecs], axis=1)
    head_of = (jnp.arange(GMLP_WIDTH)[None, :] // GMLP_HEAD_DIM == jnp.arange(GMLP_HEADS)[:, None]).astype(_F32)
    pos_of = (jnp.arange(CHUNK)[:, None] % dec_seq == jnp.arange(dec_seq)[None, :]).astype(_F32)
    w_s4 = jnp.tril(w_s[:, :, :dec_seq, :dec_seq])
    w = dict(
        mixer_vecs=pack([g_mix, d_skip, b_glu, g_v, b_v, g_out_s, g_out_g]),
        ffn_vecs=pack([g_ffn, g_ple, jnp.broadcast_to(g_final, (DEPTH, D_MODEL))]),
        a_blk=a_blk, w_v=w_v, w_y=w_y,
        mixw_p=w_s,
        mixb_p=jnp.einsum('dht,hc->dtc', b_s, head_of, precision=_EXACT),
        mixw_s=jnp.einsum('rt,dhts,cs->dhrc', pos_of, w_s4, pos_of, precision=_EXACT),
        mixb_s=jnp.einsum('dht,rt,hc->drc', b_s[:, :, :dec_seq], pos_of, head_of, precision=_EXACT))

    zeros = jnp.zeros((1, batch, SSM_FLAT), _F32)
    s0_re = state_ssm_re.reshape(DEPTH, dec_batch, SSM_FLAT)
    s0_im = state_ssm_im.reshape(DEPTH, dec_batch, SSM_FLAT)
    pp = p_prompt.reshape(DEPTH, n_p, PLE_DIM)
    ps = p_sample.reshape(DEPTH, n_s, PLE_DIM)
    h_p = x_prompt
    h_s = x_sample.reshape(1, n_s, D_MODEL)
    re_p, im_p, re_s, im_s, v_s = [], [], [], [], []
    w.update(zip(mixer_mats, cast))
    for i in range(DEPTH):
        final = i == DEPTH - 1
        h_p, sre, sim, _, cast = _mixer(h_p, zeros, zeros, w, i, h0_layer=0, nseq=batch, t_len=CHUNK,
                                        blk_pitch=CHUNK // S5_BLOCK + SCAN_PAD_BLOCKS, keep_v=False,
                                        cast=[(a, i) for a in ffn_mats.values()])
        w.update(zip(ffn_mats, cast))
        re_p.append(sre)
        im_p.append(sim)
        h_s, sre, sim, vn, _ = _mixer(h_s, s0_re, s0_im, w, i, h0_layer=i, nseq=dec_batch, t_len=dec_seq,
                                      blk_pitch=dec_seq // S5_BLOCK, keep_v=True)
        re_s.append(sre)
        im_s.append(sim)
        v_s.append(vn)
        h_p, h_s, cast = _ffn(h_p.reshape(n_p, D_MODEL), h_s.reshape(n_s, D_MODEL), pp, ps, w, i, final=final,
                              cast=[] if final else [(a, i + 1) for a in mixer_mats.values()])
        w.update(zip(mixer_mats, cast))
        h_p = h_p.reshape(batch, seq, D_MODEL)
        h_s = h_s.reshape(1, n_s, D_MODEL)

    st_p = (DEPTH, batch, SSM_GROUPS, SSM_STATE)
    st_s = (DEPTH, dec_batch, SSM_GROUPS, SSM_STATE)
    return (h_p, h_s.reshape(dec_batch, dec_seq, D_MODEL),
            jnp.stack(re_p).reshape(st_p), jnp.stack(im_p).reshape(st_p),
            jnp.stack(re_s).reshape(st_s), jnp.stack(im_s).reshape(st_s),
            jnp.stack(v_s).reshape(DEPTH, dec_batch, dec_seq, GMLP_WIDTH))
```

```python
import functools
import math

import jax
import jax.numpy as jnp
from jax import lax
from jax.experimental import pallas as pl
from jax.experimental.pallas import tpu as pltpu

D_MODEL = 1024
DEPTH = 2
SSM_WIDTH = 512
GMLP_WIDTH = 512
SSM_GROUP = 16
SSM_GROUPS = 32
SSM_STATE = 64
SSM_FLAT = SSM_GROUPS * SSM_STATE
CHUNK = 128
GMLP_HEADS = 4
GMLP_HEAD_DIM = 128
PLE_DIM = 256
D_FF = 4096
EPS = 1e-6
LAM_RE_MAX = -1e-4

LANES = 128
SUBLANES = 8
S5_SLICES = SSM_WIDTH // LANES
S5_SUBS = 2 * S5_SLICES
S5_SUB_CH = SSM_WIDTH // S5_SUBS
S5_SUB_STATES = SSM_FLAT // S5_SUBS
S5_SUB_SLABS = S5_SUB_STATES // LANES
S5_BLOCK = 4
SCAN_PAD_BLOCKS = 4
IN_TILE = 128
OUT_TILE = 256
FFN_ROWS = 512
FFN_COLS = 1024
FFN_SPLIT = 2
MIXER_VMEM_BYTES = 56 * 1024 * 1024
FFN_VMEM_BYTES = 52 * 1024 * 1024

_F32 = jnp.float32
_BF16 = jnp.bfloat16
_EXACT = lax.Precision.HIGHEST


def _dot(a, b):
    return jnp.dot(a, b, preferred_element_type=_F32)


def _rmsnorm(x, g):
    r = lax.rsqrt(jnp.mean(x * x, axis=-1, keepdims=True) + EPS)
    return x * r * g


def _gelu(x):
    c = math.sqrt(2.0 / math.pi)
    return x * (0.5 * (1.0 + jnp.tanh(c * (x + 0.044715 * (x * x * x)))))


def _disc_kernel(lr_ref, li_ref, ldt_ref, br_ref, bi_ref, abr_ref, abi_ref, bbr_ref, bbi_ref):
    lr = jnp.minimum(lr_ref[...], LAM_RE_MAX)
    li = li_ref[...]
    dt = jnp.exp(ldt_ref[...])
    mag = jnp.exp(lr * dt)
    abr = mag * jnp.cos(li * dt)
    abi = mag * jnp.sin(li * dt)
    den = lr * lr + li * li
    nr = abr - 1.0
    ni = abi
    qr = (nr * lr + ni * li) / den
    qi = (ni * lr - nr * li) / den
    abr_ref[...] = abr
    abi_ref[...] = abi
    for h in range(SSM_GROUP):
        br = br_ref[h]
        bi = bi_ref[h]
        bbr_ref[h] = qr * br - qi * bi
        bbi_ref[h] = qr * bi + qi * br


def _discretise(lam_re, lam_im, log_dt, b_re, b_im):
    dg = DEPTH * SSM_GROUPS
    lr = lam_re.reshape(dg, SSM_STATE)
    li = lam_im.reshape(dg, SSM_STATE)
    ldt = log_dt.reshape(dg, 1)
    br = jnp.moveaxis(b_re, -1, 0).reshape(SSM_GROUP, dg, SSM_STATE)
    bi = jnp.moveaxis(b_im, -1, 0).reshape(SSM_GROUP, dg, SSM_STATE)
    vec = jax.ShapeDtypeStruct((dg, SSM_STATE), _F32)
    mat = jax.ShapeDtypeStruct((SSM_GROUP, dg, SSM_STATE), _F32)
    abr, abi, bbr, bbi = pl.pallas_call(
        _disc_kernel, out_shape=(vec, vec, mat, mat), name="s5_discretise")(lr, li, ldt, br, bi)
    vshp = (DEPTH, SSM_GROUPS, SSM_STATE)
    return abr.reshape(vshp), abi.reshape(vshp), bbr, bbi


def _cmul(a, b):
    return a[0] * b[0] - a[1] * b[1], a[0] * b[1] + a[1] * b[0]


PREPARE_INPUTS = 6


def _prepare_kernel(*refs):
    a_ref, bb_ref, cre_ref, cim_ref, t_ref, tt_ref = refs[:PREPARE_INPUTS]
    n_cast = (len(refs) - PREPARE_INPUTS - 2) // 2
    wv_ref, wy_ref = refs[PREPARE_INPUTS + n_cast:PREPARE_INPUTS + n_cast + 2]
    _cast_rows(refs[PREPARE_INPUTS:PREPARE_INPUTS + n_cast], refs[PREPARE_INPUTS + n_cast + 2:])
    r = S5_BLOCK
    ch = S5_SUB_CH
    n_st = S5_SUB_STATES
    g_shift = SSM_GROUP.bit_length() - 1
    p_shift = SSM_STATE.bit_length() - 1

    def group_mask(n_rows, row_shift, n_cols, col_width, col_shift):
        rg = lax.broadcasted_iota(jnp.int32, (n_rows, n_cols), 0) >> row_shift
        cg = (lax.broadcasted_iota(jnp.int32, (n_rows, n_cols), 1) & (col_width - 1)) >> col_shift
        return rg == cg

    mask_v = group_mask(ch, g_shift, n_st, n_st, p_shift)
    mask_s = group_mask(n_st, p_shift, r * ch, ch, g_shift)
    mask_u = group_mask(ch, g_shift, r * ch, ch, g_shift)

    def dot_t(x, y, precision=None):
        return lax.dot_general(x, y, (((1,), (1,)), ((), ())), precision=precision,
                               preferred_element_type=_F32)

    def expand(x):
        return jnp.where(mask_v, _dot(x.astype(_BF16), t_ref[...]), 0.0)

    for sub in range(S5_SUBS):
        rows = slice(sub * ch, (sub + 1) * ch)
        a1 = (a_ref[0, rows, :], a_ref[1, rows, :])
        bb = (bb_ref[0, rows, :], bb_ref[1, rows, :])
        cc = (cre_ref[rows, :], cim_ref[rows, :])
        apow = [None, a1]
        for _ in range(r - 1):
            apow.append(_cmul(apow[-1], a1))
        ab = [bb] + [_cmul(apow[k], bb) for k in range(1, r)]
        ca = [cc] + [_cmul(cc, apow[k]) for k in range(1, r + 1)]
        for m in range(r):
            band = jnp.concatenate([expand(ab[r - 1 - m][0]), expand(ab[r - 1 - m][1])], axis=-1)
            wv_ref[sub, m * ch:(m + 1) * ch, :] = band.astype(_BF16)
        for part in range(2):
            ca_all = jnp.concatenate([ca[i + 1][part] for i in range(r)], axis=0).astype(_BF16)
            band = jnp.where(mask_s, dot_t(tt_ref[...], ca_all), 0.0)
            wy_ref[sub, part * n_st:(part + 1) * n_st, :] = (band if part == 0 else -band).astype(_BF16)
        none = jnp.zeros((ch, SSM_STATE), _F32)
        for m in range(r):
            shifted = [jnp.concatenate([ca[i - m][part] if i >= m else none for i in range(r)], axis=0)
                       for part in range(2)]
            band = dot_t(bb[0], shifted[0], _EXACT) - dot_t(bb[1], shifted[1], _EXACT)
            wy_ref[sub, 2 * n_st + m * ch:2 * n_st + (m + 1) * ch, :] = (
                jnp.where(mask_u, band, 0.0).astype(_BF16))


def _s5_matrices(abr, abi, bbr, bbi, c_re, c_im, cast=()):
    r = S5_BLOCK
    n_rows = SSM_GROUPS * SSM_GROUP
    a_rep = jnp.repeat(jnp.stack([abr, abi], axis=1), SSM_GROUP, axis=2)
    bb = jnp.stack([bbr, bbi]).reshape(2, SSM_GROUP, DEPTH, SSM_GROUPS, SSM_STATE)
    bb = jnp.transpose(bb, (2, 0, 3, 1, 4)).reshape(DEPTH, 2, n_rows, SSM_STATE)
    cre = c_re.reshape(DEPTH, n_rows, SSM_STATE)
    cim = c_im.reshape(DEPTH, n_rows, SSM_STATE)
    t_state = jnp.tile(jnp.eye(SSM_STATE, dtype=_BF16), (1, SSM_GROUPS // S5_SUBS))

    def per_layer(a):
        nd = a.ndim - 1
        return pl.BlockSpec((None,) + a.shape[1:], lambda d: (d,) + (0,) * nd)

    v_shape = (DEPTH, S5_SUBS, r * S5_SUB_CH, 2 * S5_SUB_STATES)
    y_shape = (DEPTH, S5_SUBS, 2 * S5_SUB_STATES + r * S5_SUB_CH, r * S5_SUB_CH)
    cast_arrs, cast_in, cast_out, cast_shape = _cast_specs(cast, DEPTH, lambda d: d)
    w_v, w_y, *cast_res = pl.pallas_call(
        _prepare_kernel,
        grid=(DEPTH,),
        in_specs=[per_layer(a_rep), per_layer(bb), per_layer(cre), per_layer(cim),
                  pl.BlockSpec(t_state.shape, lambda d: (0, 0)),
                  pl.BlockSpec(t_state.shape[::-1], lambda d: (0, 0))] + cast_in,
        out_specs=tuple([pl.BlockSpec((None,) + v_shape[1:], lambda d: (d, 0, 0, 0)),
                         pl.BlockSpec((None,) + y_shape[1:], lambda d: (d, 0, 0, 0))] + cast_out),
        out_shape=tuple([jax.ShapeDtypeStruct(v_shape, _BF16), jax.ShapeDtypeStruct(y_shape, _BF16)]
                        + cast_shape),
        name="s5_prepare",
    )(a_rep, bb, cre, cim, t_state, t_state.T, *cast_arrs)
    a_pow = (abr, abi)
    for _ in range(r - 1):
        a_pow = _cmul(a_pow, (abr, abi))
    a_blk = jnp.stack(a_pow, axis=1).reshape(DEPTH, 2, SSM_FLAT)
    return w_v, w_y, a_blk, [(a[None], 0) for a in cast_res]


def _tile_rows(ref, i, rows):
    nb, tb, c = ref.shape
    if nb == 1:
        return ref[0, i * rows:(i + 1) * rows, :]
    per = rows // tb
    return ref[i * per:(i + 1) * per].reshape(rows, c)


def _store_tile_rows(ref, i, rows, val):
    nb, tb, c = ref.shape
    if nb == 1:
        ref[0, i * rows:(i + 1) * rows, :] = val
    else:
        per = rows // tb
        ref[i * per:(i + 1) * per] = val.reshape(per, tb, c)


def _load_slabs(ref, i, rows, t_len, seq_rows):
    if seq_rows == t_len:
        return jnp.concatenate([ref[s, i * rows:(i + 1) * rows, :] for s in range(S5_SLICES)], axis=-1)
    per = rows // t_len
    return jnp.concatenate(
        [jnp.concatenate([ref[s, n * seq_rows:n * seq_rows + t_len, :] for s in range(S5_SLICES)], axis=-1)
         for n in range(i * per, (i + 1) * per)], axis=0)


def _store_slabs(ref, i, rows, t_len, seq_rows, val):
    for s in range(S5_SLICES):
        cols = slice(s * LANES, (s + 1) * LANES)
        if seq_rows == t_len:
            ref[s, i * rows:(i + 1) * rows, :] = val[:, cols]
        else:
            per = rows // t_len
            for k in range(per):
                n = i * per + k
                ref[s, n * seq_rows:n * seq_rows + t_len, :] = val[k * t_len:(k + 1) * t_len, cols]


MIXER_INPUTS = 12
MIXER_VECS = ('g_mix', 'd_skip', 'b_glu', 'g_v', 'b_v', 'g_out_s', 'g_out_g')
FFN_VECS = ('g_ffn', 'g_ple', 'g_final')


def _vec_row(ref, names, name, width):
    k = names.index(name)
    return ref[k:k + 1, :width]


def _cast_rows(in_refs, out_refs):
    for src, dst in zip(in_refs, out_refs):
        dst[...] = src[...].astype(_BF16)


def _mixer_kernel(*refs, nseq, t_len, blk_pitch, keep_v, n_cast):
    (x_ref, h0r_ref, h0i_ref, vec_ref, w_in_ref, a_blk_ref, wv_ref, wy_ref,
     w_glu_ref, mixw_ref, mixb_ref, w_out_ref) = refs[:MIXER_INPUTS]
    g_mix, d_skip, b_glu, g_v, b_v, g_out_s, g_out_g = (
        _vec_row(vec_ref, MIXER_VECS, n, D_MODEL if n == 'g_mix' else SSM_WIDTH) for n in MIXER_VECS)
    n_in = MIXER_INPUTS + n_cast
    out_ref, sre_ref, sim_ref = refs[n_in:n_in + 3]
    n_out = 3 + int(keep_v)
    if keep_v:
        vn_ref = refs[n_in + 3]
    us_scr, ys_scr, yg_scr, hb2_scr = refs[n_in + n_out + n_cast:]
    rows = nseq * t_len
    n_in_tiles = rows // IN_TILE
    n_out_tiles = rows // OUT_TILE
    n_blk = t_len // S5_BLOCK
    seq_rows = S5_BLOCK * blk_pitch
    n_buf = nseq * blk_pitch

    @pl.when(pl.program_id(0) == 0)
    def _():
        sre_ref[...] = h0r_ref[...]
        sim_ref[...] = h0i_ref[...]
        us_scr[...] = jnp.zeros_like(us_scr)

    row = lax.broadcasted_iota(jnp.int32, (CHUNK, CHUNK), 0)
    col = lax.broadcasted_iota(jnp.int32, (CHUNK, CHUNK), 1)
    t_shift = t_len.bit_length() - 1
    causal = ((row >> t_shift) == (col >> t_shift)) & ((col & (t_len - 1)) <= (row & (t_len - 1)))
    mixw = [jnp.where(causal, mixw_ref[h], 0.0).astype(_BF16) for h in range(GMLP_HEADS)]

    def in_proj(i):
        x = _tile_rows(x_ref, i, IN_TILE)
        a = _rmsnorm(x, g_mix).astype(_BF16)
        return _dot(a, w_in_ref[...])

    lane_lo = lax.broadcasted_iota(jnp.int32, (n_buf, LANES), 1) < S5_SUB_CH

    def regroup_halves(blocks):
        rolled = [pltpu.roll(x, S5_SUB_CH, axis=1) for x in blocks]
        pairs = range(0, len(blocks), 2)
        lo = [jnp.where(lane_lo, blocks[k], rolled[k + 1]) for k in pairs]
        hi = [jnp.where(lane_lo, rolled[k], blocks[k + 1]) for k in pairs]
        return jnp.concatenate(lo, axis=-1), jnp.concatenate(hi, axis=-1)

    def block_inputs(sub, u_halves):
        if sub % 2 == 0:
            s = sub // 2
            u_halves = regroup_halves([us_scr[s, pl.ds(m, n_buf, stride=S5_BLOCK), :]
                                       for m in range(S5_BLOCK)])
        u_blk = u_halves[sub % 2].astype(_BF16)
        v_in = _dot(u_blk, wv_ref[sub])
        for c in range(2 * S5_SUB_SLABS):
            hb2_scr[sub % 2, c] = v_in[:, c * LANES:(c + 1) * LANES]
        return u_blk, u_halves

    z_next = in_proj(0)
    for i in range(n_in_tiles):
        z = z_next
        if i + 1 < n_in_tiles:
            z_next = in_proj(i + 1)
        _store_slabs(us_scr, i, IN_TILE, t_len, seq_rows, z[:, :SSM_WIDTH])
        if i + 1 == n_in_tiles:
            u_next, u_halves = block_inputs(0, None)
        zg = _gelu(z[:, SSM_WIDTH:])
        ug = zg[:, :GMLP_WIDTH]
        v = zg[:, GMLP_WIDTH:]
        mu = jnp.mean(v, axis=-1, keepdims=True)
        vc = v - mu
        r = lax.rsqrt(jnp.mean(vc * vc, axis=-1, keepdims=True) + EPS)
        vn = vc * r * g_v + b_v
        if keep_v:
            vn_ref[i * IN_TILE:(i + 1) * IN_TILE, :] = vn
        vn = vn.astype(_BF16)
        for b in range(IN_TILE // CHUNK):
            rb = slice(b * CHUNK, (b + 1) * CHUNK)
            mixed = [_dot(mixw[h], vn[rb, h * GMLP_HEAD_DIM:(h + 1) * GMLP_HEAD_DIM])
                     for h in range(GMLP_HEADS)]
            m = jnp.concatenate(mixed, axis=-1) + mixb_ref[...]
            yg = _rmsnorm(ug[rb, :] * m, g_out_g).astype(_BF16)
            yg_scr[i * IN_TILE + b * CHUNK:i * IN_TILE + (b + 1) * CHUNK, :] = yg

    _cast_rows(refs[MIXER_INPUTS:n_in], refs[n_in + n_out:n_in + n_out + n_cast])
    y_prev = None
    for sub in range(S5_SUBS):
        u_blk = u_next
        if sub + 1 < S5_SUBS:
            u_next, u_halves = block_inputs(sub + 1, u_halves)
        hb_scr = hb2_scr.at[sub % 2]
        s0 = sub * S5_SUB_STATES
        a_re = [jnp.broadcast_to(a_blk_ref[0:1, s0 + c * LANES:s0 + (c + 1) * LANES], (nseq, LANES))
                for c in range(S5_SUB_SLABS)]
        a_im = [jnp.broadcast_to(a_blk_ref[1:2, s0 + c * LANES:s0 + (c + 1) * LANES], (nseq, LANES))
                for c in range(S5_SUB_SLABS)]
        h_init = tuple(
            (sre_ref[:, s0 + c * LANES:s0 + (c + 1) * LANES], sim_ref[:, s0 + c * LANES:s0 + (c + 1) * LANES])
            for c in range(S5_SUB_SLABS))

        def scan_step(j, h):
            new = []
            for c in range(S5_SUB_SLABS):
                rows_j = pl.ds(j, nseq, stride=blk_pitch)
                hr, hi = h[c]
                v_re = hb_scr[c, rows_j, :]
                v_im = hb_scr[S5_SUB_SLABS + c, rows_j, :]
                hb_scr[c, rows_j, :] = hr
                hb_scr[S5_SUB_SLABS + c, rows_j, :] = hi
                new.append((a_re[c] * hr - a_im[c] * hi + v_re, a_re[c] * hi + a_im[c] * hr + v_im))
            return tuple(new)
        h_fin = h_init
        for j in range(n_blk):
            h_fin = scan_step(j, h_fin)
        for c in range(S5_SUB_SLABS):
            sre_ref[:, s0 + c * LANES:s0 + (c + 1) * LANES] = h_fin[c][0]
            sim_ref[:, s0 + c * LANES:s0 + (c + 1) * LANES] = h_fin[c][1]

        lhs = jnp.concatenate([hb_scr[c].astype(_BF16) for c in range(2 * S5_SUB_SLABS)] + [u_blk], axis=-1)
        y_sub = _dot(lhs, wy_ref[sub])
        if sub % 2 == 0:
            y_prev = y_sub
        else:
            pieces = [y[:, k * LANES:(k + 1) * LANES] for k in range(S5_BLOCK // 2) for y in (y_prev, y_sub)]
            for first, y_rows in enumerate(regroup_halves(pieces)):
                for k in range(S5_BLOCK // 2):
                    ys_scr[sub // 2, pl.ds(2 * k + first, n_buf, stride=S5_BLOCK), :] = (
                        y_rows[:, k * LANES:(k + 1) * LANES])

    def glu_in(i):
        ys = (_load_slabs(ys_scr, i, OUT_TILE, t_len, seq_rows)
              + d_skip * _load_slabs(us_scr, i, OUT_TILE, t_len, seq_rows))
        ys = _gelu(ys)
        return ys, _dot(ys.astype(_BF16), w_glu_ref[...])

    nxt = glu_in(0)
    for i in range(n_out_tiles):
        ys, gate = nxt
        if i + 1 < n_out_tiles:
            nxt = glu_in(i + 1)
        ys = ys * jax.nn.sigmoid(gate + b_glu)
        ys = _rmsnorm(ys, g_out_s).astype(_BF16)
        y = (_dot(ys, w_out_ref[:SSM_WIDTH, :])
             + _dot(yg_scr[i * OUT_TILE:(i + 1) * OUT_TILE, :], w_out_ref[SSM_WIDTH:, :]))
        _store_tile_rows(out_ref, i, OUT_TILE, _tile_rows(x_ref, i, OUT_TILE) + y)


def _layer_spec(arr, layer):
    nd = arr.ndim - 1
    return pl.BlockSpec((None,) + arr.shape[1:], lambda j: (layer,) + (0,) * nd,
                        pipeline_mode=pl.Buffered(1))


MIXER_WEIGHTS = ('mixer_vecs', 'w_in', 'a_blk', 'w_v', 'w_y', 'w_glu', 'mixw', 'mixb', 'w_out')
FFN_WEIGHTS = ('ffn_vecs', 'w_up', 'w_down', 'w_ple_gate', 'w_ple')


def _weights(w, names, layer):
    picked = [w[k] if isinstance(w[k], tuple) else (w[k], layer) for k in names]
    return [a for a, _ in picked], [_layer_spec(a, l) for a, l in picked]


def _cast_specs(cast, steps, blk_of):
    arrs, in_specs, out_specs, out_shape = [], [], [], []
    for a, layer in cast:
        _, r, c = a.shape
        rb = r // steps
        assert rb * steps == r and rb % (2 * SUBLANES) == 0
        arrs.append(a)
        in_specs.append(pl.BlockSpec((None, rb, c), lambda j, layer=layer: (layer, blk_of(j), 0)))
        out_specs.append(pl.BlockSpec((rb, c), lambda j: (blk_of(j), 0)))
        out_shape.append(jax.ShapeDtypeStruct((r, c), _BF16))
    return arrs, in_specs, out_specs, out_shape


def _mixer(x, h0r, h0i, w, layer, *, h0_layer, nseq, t_len, blk_pitch, keep_v, cast=()):
    nb, length, _ = x.shape
    t_blk = t_len if nb == nseq else nseq * t_len
    rows = nseq * t_len
    steps = length // t_blk
    slab_rows = nseq * S5_BLOCK * blk_pitch
    x_spec = pl.BlockSpec((nb, t_blk, D_MODEL), lambda j: (0, j, 0))
    st_spec = pl.BlockSpec((nseq, SSM_FLAT), lambda j: (0, 0))
    names = [k + ('_s' if keep_v else '_p') if k in ('mixw', 'mixb') else k for k in MIXER_WEIGHTS]
    consts, const_specs = _weights(w, names, layer)
    assert len(consts) + 3 == MIXER_INPUTS
    cast_arrs, cast_in, cast_out, cast_shape = _cast_specs(cast, steps, lambda j: j)
    out_specs = [x_spec, st_spec, st_spec]
    out_shape = [jax.ShapeDtypeStruct(x.shape, _F32),
                 jax.ShapeDtypeStruct((nseq, SSM_FLAT), _F32),
                 jax.ShapeDtypeStruct((nseq, SSM_FLAT), _F32)]
    scratch = [pltpu.VMEM((S5_SLICES, slab_rows, LANES), _F32),
               pltpu.VMEM((S5_SLICES, slab_rows, LANES), _F32),
               pltpu.VMEM((rows, GMLP_WIDTH), _BF16),
               pltpu.VMEM((2, 2 * S5_SUB_SLABS, nseq * blk_pitch, LANES), _F32)]
    if keep_v:
        out_specs.append(pl.BlockSpec((rows, GMLP_WIDTH), lambda j: (j, 0)))
        out_shape.append(jax.ShapeDtypeStruct((steps * rows, GMLP_WIDTH), _F32))
    res = pl.pallas_call(
        functools.partial(_mixer_kernel, nseq=nseq, t_len=t_len, blk_pitch=blk_pitch, keep_v=keep_v,
                          n_cast=len(cast_arrs)),
        grid=(steps,),
        in_specs=([x_spec, _layer_spec(h0r, h0_layer), _layer_spec(h0i, h0_layer)] + const_specs + cast_in),
        out_specs=tuple(out_specs + cast_out),
        out_shape=tuple(out_shape + cast_shape),
        scratch_shapes=scratch,
        compiler_params=pltpu.CompilerParams(dimension_semantics=("arbitrary",),
                                             vmem_limit_bytes=MIXER_VMEM_BYTES),
        name="mixer_sample" if keep_v else "mixer_prompt",
    )(x, h0r, h0i, *consts, *cast_arrs)
    n_out = 3 + int(keep_v)
    out, sre, sim = res[:3]
    return out, sre, sim, (res[3] if keep_v else None), [(a[None], 0) for a in res[n_out:]]


FFN_INPUTS = 9


def _ffn_kernel(*refs, final, prompt_steps, n_cast):
    hp_ref, hs_ref, pp_ref, ps_ref, vec_ref, w_up_ref, w_down_ref, w_gate_ref, w_ple_ref = refs[:FFN_INPUTS]
    n_in = FFN_INPUTS + n_cast
    outp_ref, outs_ref = refs[n_in:n_in + 2]
    _cast_rows(refs[FFN_INPUTS:n_in], refs[n_in + 2:])
    g_ffn, g_ple, g_fin = (_vec_row(vec_ref, FFN_VECS, n, D_MODEL) for n in FFN_VECS)

    def rows_block(h_ref, p_ref, out_ref):
        hr = h_ref.shape[0] // FFN_SPLIT
        parts = [slice(k * hr, (k + 1) * hr) for k in range(FFN_SPLIT)]
        accs = [h_ref[rs, :] for rs in parts]
        fs = [_rmsnorm(h, g_ffn).astype(_BF16) for h in accs]
        for c in range(D_FF // FFN_COLS):
            for k in range(FFN_SPLIT):
                up = _dot(fs[k], w_up_ref[:, c * FFN_COLS:(c + 1) * FFN_COLS])
                act = jnp.square(jnp.maximum(up, 0.0)).astype(_BF16)
                accs[k] = accs[k] + _dot(act, w_down_ref[c * FFN_COLS:(c + 1) * FFN_COLS, :])
        for k, rs in enumerate(parts):
            acc = accs[k]
            gate = jax.nn.sigmoid(_dot(_rmsnorm(acc, g_ple).astype(_BF16), w_gate_ref[...]))
            out = acc + gate * _dot(p_ref[rs, :].astype(_BF16), w_ple_ref[...])
            if final:
                out = _rmsnorm(out, g_fin)
            out_ref[rs, :] = out

    is_prompt = pl.program_id(0) < prompt_steps
    pl.when(is_prompt)(functools.partial(rows_block, hp_ref, pp_ref, outp_ref))
    pl.when(jnp.logical_not(is_prompt))(functools.partial(rows_block, hs_ref, ps_ref, outs_ref))


def _ffn(h_p, h_s, p_p, p_s, w, layer, *, final, cast=()):
    rows_s = min(FFN_ROWS, h_s.shape[0])
    steps_p = h_p.shape[0] // FFN_ROWS
    steps_s = h_s.shape[0] // rows_s
    consts, const_specs = _weights(w, FFN_WEIGHTS, layer)
    assert len(consts) + 4 == FFN_INPUTS
    prompt_blk = lambda j: jnp.minimum(j, steps_p - 1)
    sample_blk = lambda j: jnp.maximum(j - steps_p, 0)
    cast_arrs, cast_in, cast_out, cast_shape = _cast_specs(cast, steps_p, prompt_blk)
    res = pl.pallas_call(
        functools.partial(_ffn_kernel, final=final, prompt_steps=steps_p, n_cast=len(cast_arrs)),
        grid=(steps_p + steps_s,),
        in_specs=([pl.BlockSpec((FFN_ROWS, D_MODEL), lambda j: (prompt_blk(j), 0)),
                   pl.BlockSpec((rows_s, D_MODEL), lambda j: (sample_blk(j), 0),
                                pipeline_mode=pl.Buffered(1)),
                   pl.BlockSpec((None, FFN_ROWS, PLE_DIM), lambda j: (layer, prompt_blk(j), 0)),
                   pl.BlockSpec((None, rows_s, PLE_DIM), lambda j: (layer, sample_blk(j), 0),
                                pipeline_mode=pl.Buffered(1))]
                  + const_specs + cast_in),
        out_specs=tuple([pl.BlockSpec((FFN_ROWS, D_MODEL), lambda j: (prompt_blk(j), 0)),
                         pl.BlockSpec((rows_s, D_MODEL), lambda j: (sample_blk(j), 0))] + cast_out),
        out_shape=tuple([jax.ShapeDtypeStruct(h_p.shape, _F32), jax.ShapeDtypeStruct(h_s.shape, _F32)]
                        + cast_shape),
        compiler_params=pltpu.CompilerParams(dimension_semantics=("arbitrary",),
                                             vmem_limit_bytes=FFN_VMEM_BYTES),
        name="ffn",
    )(h_p, h_s, p_p, p_s, *consts, *cast_arrs)
    return res[0], res[1], [(a[None], 0) for a in res[2:]]


def kernel(x_prompt, x_sample, state_ssm_re, state_ssm_im, p_prompt, p_sample, g_mix, w_in, lam_re, lam_im, log_dt, b_re, b_im, c_re, c_im, d_skip, w_glu, b_glu, g_v, b_v, w_s, b_s, g_out_s, g_out_g, w_out, g_ffn, w_up, w_down, g_ple, w_ple_gate, w_ple, g_final):
    batch, seq, _ = x_prompt.shape
    dec_batch, dec_seq, _ = x_sample.shape
    n_p, n_s = batch * seq, dec_batch * dec_seq
    assert seq % CHUNK == 0 and batch == SUBLANES
    assert dec_seq % S5_BLOCK == 0 and CHUNK % dec_seq == 0 and n_s % max(IN_TILE, OUT_TILE) == 0

    mixer_mats = dict(w_in=w_in, w_glu=w_glu, w_out=w_out)
    ffn_mats = dict(w_up=w_up, w_down=w_down, w_ple_gate=w_ple_gate, w_ple=w_ple)
    abr, abi, bbr, bbi = _discretise(lam_re, lam_im, log_dt, b_re, b_im)
    w_v, w_y, a_blk, cast = _s5_matrices(abr, abi, bbr, bbi, c_re, c_im,
                                         cast=[(a, 0) for a in mixer_mats.values()])

    def pack(vecs):
        return jnp.stack([jnp.pad(v, ((0, 0), (0, D_MODEL - v.shape[-1]))) for v in vecs], axis=1)
    head_of = (jnp.arange(GMLP_WIDTH)[None, :] // GMLP_HEAD_DIM == jnp.arange(GMLP_HEADS)[:, None]).astype(_F32)
    pos_of = (jnp.arange(CHUNK)[:, None] % dec_seq == jnp.arange(dec_seq)[None, :]).astype(_F32)
    w_s4 = jnp.tril(w_s[:, :, :dec_seq, :dec_seq])
    w = dict(
        mixer_vecs=pack([g_mix, d_skip, b_glu, g_v, b_v, g_out_s, g_out_g]),
        ffn_vecs=pack([g_ffn, g_ple, jnp.broadcast_to(g_final, (DEPTH, D_MODEL))]),
        a_blk=a_blk, w_v=w_v, w_y=w_y,
        mixw_p=w_s,
        mixb_p=jnp.einsum('dht,hc->dtc', b_s, head_of, precision=_EXACT),
        mixw_s=jnp.einsum('rt,dhts,cs->dhrc', pos_of, w_s4, pos_of, precision=_EXACT),
        mixb_s=jnp.einsum('dht,rt,hc->drc', b_s[:, :, :dec_seq], pos_of, head_of, precision=_EXACT))

    zeros = jnp.zeros((1, batch, SSM_FLAT), _F32)
    s0_re = state_ssm_re.reshape(DEPTH, dec_batch, SSM_FLAT)
    s0_im = state_ssm_im.reshape(DEPTH, dec_batch, SSM_FLAT)
    pp = p_prompt.reshape(DEPTH, n_p, PLE_DIM)
    ps = p_sample.reshape(DEPTH, n_s, PLE_DIM)
    h_p = x_prompt
    h_s = x_sample.reshape(1, n_s, D_MODEL)
    re_p, im_p, re_s, im_s, v_s = [], [], [], [], []
    w.update(zip(mixer_mats, cast))
    for i in range(DEPTH):
        final = i == DEPTH - 1
        h_p, sre, sim, _, cast = _mixer(h_p, zeros, zeros, w, i, h0_layer=0, nseq=batch, t_len=CHUNK,
                                        blk_pitch=CHUNK // S5_BLOCK + SCAN_PAD_BLOCKS, keep_v=False,
                                        cast=[(a, i) for a in ffn_mats.values()])
        w.update(zip(ffn_mats, cast))
        re_p.append(sre)
        im_p.append(sim)
        h_s, sre, sim, vn, _ = _mixer(h_s, s0_re, s0_im, w, i, h0_layer=i, nseq=dec_batch, t_len=dec_seq,
                                      blk_pitch=dec_seq // S5_BLOCK, keep_v=True)
        re_s.append(sre)
        im_s.append(sim)
        v_s.append(vn)
        h_p, h_s, cast = _ffn(h_p.reshape(n_p, D_MODEL), h_s.reshape(n_s, D_MODEL), pp, ps, w, i, final=final,
                              cast=[] if final else [(a, i + 1) for a in mixer_mats.values()])
        w.update(zip(mixer_mats, cast))
        h_p = h_p.reshape(batch, seq, D_MODEL)
        h_s = h_s.reshape(1, n_s, D_MODEL)

    st_p = (DEPTH, batch, SSM_GROUPS, SSM_STATE)
    st_s = (DEPTH, dec_batch, SSM_GROUPS, SSM_STATE)
    return (h_p, h_s.reshape(dec_batch, dec_seq, D_MODEL),
            jnp.stack(re_p).reshape(st_p), jnp.stack(im_p).reshape(st_p),
            jnp.stack(re_s).reshape(st_s), jnp.stack(im_s).reshape(st_s),
            jnp.stack(v_s).reshape(DEPTH, dec_batch, dec_seq, GMLP_WIDTH))
```

```python
import functools
import math

import jax
import jax.numpy as jnp
from jax import lax
from jax.experimental import pallas as pl
from jax.experimental.pallas import tpu as pltpu

D_MODEL = 1024
DEPTH = 2
SSM_WIDTH = 512
GMLP_WIDTH = 512
SSM_GROUP = 16
SSM_GROUPS = 32
SSM_STATE = 64
SSM_FLAT = SSM_GROUPS * SSM_STATE
CHUNK = 128
GMLP_HEADS = 4
GMLP_HEAD_DIM = 128
PLE_DIM = 256
D_FF = 4096
EPS = 1e-6
LAM_RE_MAX = -1e-4

LANES = 128
SUBLANES = 8
S5_SLICES = SSM_WIDTH // LANES
S5_SUBS = 2 * S5_SLICES
S5_SUB_CH = SSM_WIDTH // S5_SUBS
S5_SUB_STATES = SSM_FLAT // S5_SUBS
S5_SUB_SLABS = S5_SUB_STATES // LANES
S5_BLOCK = 4
SCAN_PAD_BLOCKS = 4
IN_TILE = 128
OUT_TILE = 256
FFN_ROWS = 512
FFN_COLS = 1024
FFN_SPLIT = 2
MIXER_VMEM_BYTES = 56 * 1024 * 1024
FFN_VMEM_BYTES = 52 * 1024 * 1024

_F32 = jnp.float32
_BF16 = jnp.bfloat16
_EXACT = lax.Precision.HIGHEST


def _dot(a, b):
    return jnp.dot(a, b, preferred_element_type=_F32)


def _rmsnorm(x, g):
    r = lax.rsqrt(jnp.mean(x * x, axis=-1, keepdims=True) + EPS)
    return x * r * g


def _gelu(x):
    c = math.sqrt(2.0 / math.pi)
    return x * (0.5 * (1.0 + jnp.tanh(c * (x + 0.044715 * (x * x * x)))))


def _disc_kernel(lr_ref, li_ref, ldt_ref, br_ref, bi_ref, abr_ref, abi_ref, bbr_ref, bbi_ref):
    lr = jnp.minimum(lr_ref[...], LAM_RE_MAX)
    li = li_ref[...]
    dt = jnp.exp(ldt_ref[...])
    mag = jnp.exp(lr * dt)
    abr = mag * jnp.cos(li * dt)
    abi = mag * jnp.sin(li * dt)
    den = lr * lr + li * li
    nr = abr - 1.0
    ni = abi
    qr = (nr * lr + ni * li) / den
    qi = (ni * lr - nr * li) / den
    abr_ref[...] = abr
    abi_ref[...] = abi
    for h in range(SSM_GROUP):
        br = br_ref[h]
        bi = bi_ref[h]
        bbr_ref[h] = qr * br - qi * bi
        bbi_ref[h] = qr * bi + qi * br


def _discretise(lam_re, lam_im, log_dt, b_re, b_im):
    dg = DEPTH * SSM_GROUPS
    lr = lam_re.reshape(dg, SSM_STATE)
    li = lam_im.reshape(dg, SSM_STATE)
    ldt = log_dt.reshape(dg, 1)
    br = jnp.moveaxis(b_re, -1, 0).reshape(SSM_GROUP, dg, SSM_STATE)
    bi = jnp.moveaxis(b_im, -1, 0).reshape(SSM_GROUP, dg, SSM_STATE)
    vec = jax.ShapeDtypeStruct((dg, SSM_STATE), _F32)
    mat = jax.ShapeDtypeStruct((SSM_GROUP, dg, SSM_STATE), _F32)
    abr, abi, bbr, bbi = pl.pallas_call(
        _disc_kernel, out_shape=(vec, vec, mat, mat), name="s5_discretise")(lr, li, ldt, br, bi)
    vshp = (DEPTH, SSM_GROUPS, SSM_STATE)
    return abr.reshape(vshp), abi.reshape(vshp), bbr, bbi


def _cmul(a, b):
    return a[0] * b[0] - a[1] * b[1], a[0] * b[1] + a[1] * b[0]


PREPARE_INPUTS = 6


def _prepare_kernel(*refs):
    a_ref, bb_ref, cre_ref, cim_ref, t_ref, tt_ref = refs[:PREPARE_INPUTS]
    n_cast = (len(refs) - PREPARE_INPUTS - 2) // 2
    wv_ref, wy_ref = refs[PREPARE_INPUTS + n_cast:PREPARE_INPUTS + n_cast + 2]
    _cast_rows(refs[PREPARE_INPUTS:PREPARE_INPUTS + n_cast], refs[PREPARE_INPUTS + n_cast + 2:])
    r = S5_BLOCK
    ch = S5_SUB_CH
    n_st = S5_SUB_STATES
    g_shift = SSM_GROUP.bit_length() - 1
    p_shift = SSM_STATE.bit_length() - 1

    def group_mask(n_rows, row_shift, n_cols, col_width, col_shift):
        rg = lax.broadcasted_iota(jnp.int32, (n_rows, n_cols), 0) >> row_shift
        cg = (lax.broadcasted_iota(jnp.int32, (n_rows, n_cols), 1) & (col_width - 1)) >> col_shift
        return rg == cg

    mask_v = group_mask(ch, g_shift, n_st, n_st, p_shift)
    mask_s = group_mask(n_st, p_shift, r * ch, ch, g_shift)
    mask_u = group_mask(ch, g_shift, r * ch, ch, g_shift)

    def dot_t(x, y, precision=None):
        return lax.dot_general(x, y, (((1,), (1,)), ((), ())), precision=precision,
                               preferred_element_type=_F32)

    def expand(x):
        return jnp.where(mask_v, _dot(x.astype(_BF16), t_ref[...]), 0.0)

    for sub in range(S5_SUBS):
        rows = slice(sub * ch, (sub + 1) * ch)
        a1 = (a_ref[0, rows, :], a_ref[1, rows, :])
        bb = (bb_ref[0, rows, :], bb_ref[1, rows, :])
        cc = (cre_ref[rows, :], cim_ref[rows, :])
        apow = [None, a1]
        for _ in range(r - 1):
            apow.append(_cmul(apow[-1], a1))
        ab = [bb] + [_cmul(apow[k], bb) for k in range(1, r)]
        ca = [cc] + [_cmul(cc, apow[k]) for k in range(1, r + 1)]
        for m in range(r):
            band = jnp.concatenate([expand(ab[r - 1 - m][0]), expand(ab[r - 1 - m][1])], axis=-1)
            wv_ref[sub, m * ch:(m + 1) * ch, :] = band.astype(_BF16)
        for part in range(2):
            ca_all = jnp.concatenate([ca[i + 1][part] for i in range(r)], axis=0).astype(_BF16)
            band = jnp.where(mask_s, dot_t(tt_ref[...], ca_all), 0.0)
            wy_ref[sub, part * n_st:(part + 1) * n_st, :] = (band if part == 0 else -band).astype(_BF16)
        none = jnp.zeros((ch, SSM_STATE), _F32)
        for m in range(r):
            shifted = [jnp.concatenate([ca[i - m][part] if i >= m else none for i in range(r)], axis=0)
                       for part in range(2)]
            band = dot_t(bb[0], shifted[0], _EXACT) - dot_t(bb[1], shifted[1], _EXACT)
            wy_ref[sub, 2 * n_st + m * ch:2 * n_st + (m + 1) * ch, :] = (
                jnp.where(mask_u, band, 0.0).astype(_BF16))


def _s5_matrices(abr, abi, bbr, bbi, c_re, c_im, cast=()):
    r = S5_BLOCK
    n_rows = SSM_GROUPS * SSM_GROUP
    a_rep = jnp.repeat(jnp.stack([abr, abi], axis=1), SSM_GROUP, axis=2)
    bb = jnp.stack([bbr, bbi]).reshape(2, SSM_GROUP, DEPTH, SSM_GROUPS, SSM_STATE)
    bb = jnp.transpose(bb, (2, 0, 3, 1, 4)).reshape(DEPTH, 2, n_rows, SSM_STATE)
    cre = c_re.reshape(DEPTH, n_rows, SSM_STATE)
    cim = c_im.reshape(DEPTH, n_rows, SSM_STATE)
    t_state = jnp.tile(jnp.eye(SSM_STATE, dtype=_BF16), (1, SSM_GROUPS // S5_SUBS))

    def per_layer(a):
        nd = a.ndim - 1
        return pl.BlockSpec((None,) + a.shape[1:], lambda d: (d,) + (0,) * nd)

    v_shape = (DEPTH, S5_SUBS, r * S5_SUB_CH, 2 * S5_SUB_STATES)
    y_shape = (DEPTH, S5_SUBS, 2 * S5_SUB_STATES + r * S5_SUB_CH, r * S5_SUB_CH)
    cast_arrs, cast_in, cast_out, cast_shape = _cast_specs(cast, DEPTH, lambda d: d)
    w_v, w_y, *cast_res = pl.pallas_call(
        _prepare_kernel,
        grid=(DEPTH,),
        in_specs=[per_layer(a_rep), per_layer(bb), per_layer(cre), per_layer(cim),
                  pl.BlockSpec(t_state.shape, lambda d: (0, 0)),
                  pl.BlockSpec(t_state.shape[::-1], lambda d: (0, 0))] + cast_in,
        out_specs=tuple([pl.BlockSpec((None,) + v_shape[1:], lambda d: (d, 0, 0, 0)),
                         pl.BlockSpec((None,) + y_shape[1:], lambda d: (d, 0, 0, 0))] + cast_out),
        out_shape=tuple([jax.ShapeDtypeStruct(v_shape, _BF16), jax.ShapeDtypeStruct(y_shape, _BF16)]
                        + cast_shape),
        name="s5_prepare",
    )(a_rep, bb, cre, cim, t_state, t_state.T, *cast_arrs)
    a_pow = (abr, abi)
    for _ in range(r - 1):
        a_pow = _cmul(a_pow, (abr, abi))
    a_blk = jnp.stack(a_pow, axis=1).reshape(DEPTH, 2, SSM_FLAT)
    return w_v, w_y, a_blk, [(a[None], 0) for a in cast_res]


def _tile_rows(ref, i, rows):
    nb, tb, c = ref.shape
    if nb == 1:
        return ref[0, i * rows:(i + 1) * rows, :]
    per = rows // tb
    return ref[i * per:(i + 1) * per].reshape(rows, c)


def _store_tile_rows(ref, i, rows, val):
    nb, tb, c = ref.shape
    if nb == 1:
        ref[0, i * rows:(i + 1) * rows, :] = val
    else:
        per = rows // tb
        ref[i * per:(i + 1) * per] = val.reshape(per, tb, c)


def _load_slabs(ref, i, rows, t_len, seq_rows):
    if seq_rows == t_len:
        return jnp.concatenate([ref[s, i * rows:(i + 1) * rows, :] for s in range(S5_SLICES)], axis=-1)
    per = rows // t_len
    return jnp.concatenate(
        [jnp.concatenate([ref[s, n * seq_rows:n * seq_rows + t_len, :] for s in range(S5_SLICES)], axis=-1)
         for n in range(i * per, (i + 1) * per)], axis=0)


def _store_slabs(ref, i, rows, t_len, seq_rows, val):
    for s in range(S5_SLICES):
        cols = slice(s * LANES, (s + 1) * LANES)
        if seq_rows == t_len:
            ref[s, i * rows:(i + 1) * rows, :] = val[:, cols]
        else:
            per = rows // t_len
            for k in range(per):
                n = i * per + k
                ref[s, n * seq_rows:n * seq_rows + t_len, :] = val[k * t_len:(k + 1) * t_len, cols]


MIXER_INPUTS = 12
MIXER_VECS = ('g_mix', 'd_skip', 'b_glu', 'g_v', 'b_v', 'g_out_s', 'g_out_g')
FFN_VECS = ('g_ffn', 'g_ple', 'g_final')


def _vec_row(ref, names, name, width):
    k = names.index(name)
    return ref[k:k + 1, :width]


def _cast_rows(in_refs, out_refs):
    for src, dst in zip(in_refs, out_refs):
        dst[...] = src[...].astype(_BF16)


MIXER_LATE_WEIGHTS = ('w_v', 'w_y', 'w_glu', 'w_out')


def _mixer_kernel(*refs, nseq, t_len, blk_pitch, keep_v, n_cast, late_layers):
    (x_ref, h0r_ref, h0i_ref, vec_ref, w_in_ref, a_blk_ref, wv_hbm, wy_hbm,
     w_glu_hbm, mixw_ref, mixb_ref, w_out_hbm) = refs[:MIXER_INPUTS]
    g_mix, d_skip, b_glu, g_v, b_v, g_out_s, g_out_g = (
        _vec_row(vec_ref, MIXER_VECS, n, D_MODEL if n == 'g_mix' else SSM_WIDTH) for n in MIXER_VECS)
    n_in = MIXER_INPUTS + n_cast
    out_ref, sre_ref, sim_ref = refs[n_in:n_in + 3]
    n_out = 3 + int(keep_v)
    if keep_v:
        vn_ref = refs[n_in + 3]
    us_scr, ys_scr, yg_scr, hb2_scr, wv_ref, wy_ref, w_glu_ref, w_out_ref, late_sem = refs[n_in + n_out + n_cast:]
    late_copies = [pltpu.make_async_copy(src.at[layer], dst, late_sem.at[k])
                   for k, (src, dst, layer) in enumerate(zip((wv_hbm, wy_hbm, w_glu_hbm, w_out_hbm),
                                                             (wv_ref, wy_ref, w_glu_ref, w_out_ref), late_layers))]
    rows = nseq * t_len
    n_in_tiles = rows // IN_TILE
    n_out_tiles = rows // OUT_TILE
    n_blk = t_len // S5_BLOCK
    seq_rows = S5_BLOCK * blk_pitch
    n_buf = nseq * blk_pitch

    @pl.when(pl.program_id(0) == 0)
    def _():
        for cp in late_copies:
            cp.start()
        sre_ref[...] = h0r_ref[...]
        sim_ref[...] = h0i_ref[...]
        us_scr[...] = jnp.zeros_like(us_scr)

    row = lax.broadcasted_iota(jnp.int32, (CHUNK, CHUNK), 0)
    col = lax.broadcasted_iota(jnp.int32, (CHUNK, CHUNK), 1)
    t_shift = t_len.bit_length() - 1
    causal = ((row >> t_shift) == (col >> t_shift)) & ((col & (t_len - 1)) <= (row & (t_len - 1)))
    mixw = [jnp.where(causal, mixw_ref[h], 0.0).astype(_BF16) for h in range(GMLP_HEADS)]

    def in_proj(i):
        x = _tile_rows(x_ref, i, IN_TILE)
        a = _rmsnorm(x, g_mix).astype(_BF16)
        return _dot(a, w_in_ref[...])

    lane_lo = lax.broadcasted_iota(jnp.int32, (n_buf, LANES), 1) < S5_SUB_CH

    def regroup_halves(blocks):
        rolled = [pltpu.roll(x, S5_SUB_CH, axis=1) for x in blocks]
        pairs = range(0, len(blocks), 2)
        lo = [jnp.where(lane_lo, blocks[k], rolled[k + 1]) for k in pairs]
        hi = [jnp.where(lane_lo, rolled[k], blocks[k + 1]) for k in pairs]
        return jnp.concatenate(lo, axis=-1), jnp.concatenate(hi, axis=-1)

    def block_inputs(sub, u_halves):
        if sub % 2 == 0:
            s = sub // 2
            u_halves = regroup_halves([us_scr[s, pl.ds(m, n_buf, stride=S5_BLOCK), :]
                                       for m in range(S5_BLOCK)])
        u_blk = u_halves[sub % 2].astype(_BF16)
        v_in = _dot(u_blk, wv_ref[sub])
        for c in range(2 * S5_SUB_SLABS):
            hb2_scr[sub % 2, c] = v_in[:, c * LANES:(c + 1) * LANES]
        return u_blk, u_halves

    z_next = in_proj(0)
    for i in range(n_in_tiles):
        z = z_next
        if i + 1 < n_in_tiles:
            z_next = in_proj(i + 1)
        else:
            @pl.when(pl.program_id(0) == 0)
            def _():
                for cp in late_copies:
                    cp.wait()
        _store_slabs(us_scr, i, IN_TILE, t_len, seq_rows, z[:, :SSM_WIDTH])
        if i + 1 == n_in_tiles:
            u_next, u_halves = block_inputs(0, None)
        zg = _gelu(z[:, SSM_WIDTH:])
        ug = zg[:, :GMLP_WIDTH]
        v = zg[:, GMLP_WIDTH:]
        mu = jnp.mean(v, axis=-1, keepdims=True)
        vc = v - mu
        r = lax.rsqrt(jnp.mean(vc * vc, axis=-1, keepdims=True) + EPS)
        vn = vc * r * g_v + b_v
        if keep_v:
            vn_ref[i * IN_TILE:(i + 1) * IN_TILE, :] = vn
        vn = vn.astype(_BF16)
        for b in range(IN_TILE // CHUNK):
            rb = slice(b * CHUNK, (b + 1) * CHUNK)
            mixed = [_dot(mixw[h], vn[rb, h * GMLP_HEAD_DIM:(h + 1) * GMLP_HEAD_DIM])
                     for h in range(GMLP_HEADS)]
            m = jnp.concatenate(mixed, axis=-1) + mixb_ref[...]
            yg = _rmsnorm(ug[rb, :] * m, g_out_g).astype(_BF16)
            yg_scr[i * IN_TILE + b * CHUNK:i * IN_TILE + (b + 1) * CHUNK, :] = yg

    _cast_rows(refs[MIXER_INPUTS:n_in], refs[n_in + n_out:n_in + n_out + n_cast])
    y_prev = None
    for sub in range(S5_SUBS):
        u_blk = u_next
        if sub + 1 < S5_SUBS:
            u_next, u_halves = block_inputs(sub + 1, u_halves)
        hb_scr = hb2_scr.at[sub % 2]
        s0 = sub * S5_SUB_STATES
        a_re = [jnp.broadcast_to(a_blk_ref[0:1, s0 + c * LANES:s0 + (c + 1) * LANES], (nseq, LANES))
                for c in range(S5_SUB_SLABS)]
        a_im = [jnp.broadcast_to(a_blk_ref[1:2, s0 + c * LANES:s0 + (c + 1) * LANES], (nseq, LANES))
                for c in range(S5_SUB_SLABS)]
        h_init = tuple(
            (sre_ref[:, s0 + c * LANES:s0 + (c + 1) * LANES], sim_ref[:, s0 + c * LANES:s0 + (c + 1) * LANES])
            for c in range(S5_SUB_SLABS))

        def scan_step(j, h):
            new = []
            for c in range(S5_SUB_SLABS):
                rows_j = pl.ds(j, nseq, stride=blk_pitch)
                hr, hi = h[c]
                v_re = hb_scr[c, rows_j, :]
                v_im = hb_scr[S5_SUB_SLABS + c, rows_j, :]
                hb_scr[c, rows_j, :] = hr
                hb_scr[S5_SUB_SLABS + c, rows_j, :] = hi
                new.append((a_re[c] * hr - a_im[c] * hi + v_re, a_re[c] * hi + a_im[c] * hr + v_im))
            return tuple(new)
        h_fin = h_init
        for j in range(n_blk):
            h_fin = scan_step(j, h_fin)
        for c in range(S5_SUB_SLABS):
            sre_ref[:, s0 + c * LANES:s0 + (c + 1) * LANES] = h_fin[c][0]
            sim_ref[:, s0 + c * LANES:s0 + (c + 1) * LANES] = h_fin[c][1]

        lhs = jnp.concatenate([hb_scr[c].astype(_BF16) for c in range(2 * S5_SUB_SLABS)] + [u_blk], axis=-1)
        y_sub = _dot(lhs, wy_ref[sub])
        if sub % 2 == 0:
            y_prev = y_sub
        else:
            pieces = [y[:, k * LANES:(k + 1) * LANES] for k in range(S5_BLOCK // 2) for y in (y_prev, y_sub)]
            for first, y_rows in enumerate(regroup_halves(pieces)):
                for k in range(S5_BLOCK // 2):
                    ys_scr[sub // 2, pl.ds(2 * k + first, n_buf, stride=S5_BLOCK), :] = (
                        y_rows[:, k * LANES:(k + 1) * LANES])

    def glu_in(i):
        ys = (_load_slabs(ys_scr, i, OUT_TILE, t_len, seq_rows)
              + d_skip * _load_slabs(us_scr, i, OUT_TILE, t_len, seq_rows))
        ys = _gelu(ys)
        return ys, _dot(ys.astype(_BF16), w_glu_ref[...])

    nxt = glu_in(0)
    for i in range(n_out_tiles):
        ys, gate = nxt
        if i + 1 < n_out_tiles:
            nxt = glu_in(i + 1)
        ys = ys * jax.nn.sigmoid(gate + b_glu)
        ys = _rmsnorm(ys, g_out_s).astype(_BF16)
        y = (_dot(ys, w_out_ref[:SSM_WIDTH, :])
             + _dot(yg_scr[i * OUT_TILE:(i + 1) * OUT_TILE, :], w_out_ref[SSM_WIDTH:, :]))
        _store_tile_rows(out_ref, i, OUT_TILE, _tile_rows(x_ref, i, OUT_TILE) + y)


def _layer_spec(arr, layer):
    nd = arr.ndim - 1
    return pl.BlockSpec((None,) + arr.shape[1:], lambda j: (layer,) + (0,) * nd,
                        pipeline_mode=pl.Buffered(1))


MIXER_WEIGHTS = ('mixer_vecs', 'w_in', 'a_blk', 'w_v', 'w_y', 'w_glu', 'mixw', 'mixb', 'w_out')
FFN_WEIGHTS = ('ffn_vecs', 'w_up', 'w_down', 'w_ple_gate', 'w_ple')


def _weights(w, names, layer):
    picked = [w[k] if isinstance(w[k], tuple) else (w[k], layer) for k in names]
    return [a for a, _ in picked], [_layer_spec(a, l) for a, l in picked]


def _cast_specs(cast, steps, blk_of):
    arrs, in_specs, out_specs, out_shape = [], [], [], []
    for a, layer in cast:
        _, r, c = a.shape
        rb = r // steps
        assert rb * steps == r and rb % (2 * SUBLANES) == 0
        arrs.append(a)
        in_specs.append(pl.BlockSpec((None, rb, c), lambda j, layer=layer: (layer, blk_of(j), 0)))
        out_specs.append(pl.BlockSpec((rb, c), lambda j: (blk_of(j), 0)))
        out_shape.append(jax.ShapeDtypeStruct((r, c), _BF16))
    return arrs, in_specs, out_specs, out_shape


def _mixer(x, h0r, h0i, w, layer, *, h0_layer, nseq, t_len, blk_pitch, keep_v, cast=()):
    nb, length, _ = x.shape
    t_blk = t_len if nb == nseq else nseq * t_len
    rows = nseq * t_len
    steps = length // t_blk
    slab_rows = nseq * S5_BLOCK * blk_pitch
    x_spec = pl.BlockSpec((nb, t_blk, D_MODEL), lambda j: (0, j, 0))
    st_spec = pl.BlockSpec((nseq, SSM_FLAT), lambda j: (0, 0))
    names = [k + ('_s' if keep_v else '_p') if k in ('mixw', 'mixb') else k for k in MIXER_WEIGHTS]
    consts, const_specs = _weights(w, names, layer)
    assert len(consts) + 3 == MIXER_INPUTS
    late = [names.index(k) for k in MIXER_LATE_WEIGHTS]
    late_layers = tuple(0 if isinstance(w[names[k]], tuple) else layer for k in late)
    for k in late:
        const_specs[k] = pl.BlockSpec(memory_space=pl.ANY)
    cast_arrs, cast_in, cast_out, cast_shape = _cast_specs(cast, steps, lambda j: j)
    out_specs = [x_spec, st_spec, st_spec]
    out_shape = [jax.ShapeDtypeStruct(x.shape, _F32),
                 jax.ShapeDtypeStruct((nseq, SSM_FLAT), _F32),
                 jax.ShapeDtypeStruct((nseq, SSM_FLAT), _F32)]
    scratch = [pltpu.VMEM((S5_SLICES, slab_rows, LANES), _F32),
               pltpu.VMEM((S5_SLICES, slab_rows, LANES), _F32),
               pltpu.VMEM((rows, GMLP_WIDTH), _BF16),
               pltpu.VMEM((2, 2 * S5_SUB_SLABS, nseq * blk_pitch, LANES), _F32)]
    scratch += [pltpu.VMEM(consts[k].shape[1:], consts[k].dtype) for k in late]
    scratch.append(pltpu.SemaphoreType.DMA((len(late),)))
    if keep_v:
        out_specs.append(pl.BlockSpec((rows, GMLP_WIDTH), lambda j: (j, 0)))
        out_shape.append(jax.ShapeDtypeStruct((steps * rows, GMLP_WIDTH), _F32))
    res = pl.pallas_call(
        functools.partial(_mixer_kernel, nseq=nseq, t_len=t_len, blk_pitch=blk_pitch, keep_v=keep_v,
                          n_cast=len(cast_arrs), late_layers=late_layers),
        grid=(steps,),
        in_specs=([x_spec, _layer_spec(h0r, h0_layer), _layer_spec(h0i, h0_layer)] + const_specs + cast_in),
        out_specs=tuple(out_specs + cast_out),
        out_shape=tuple(out_shape + cast_shape),
        scratch_shapes=scratch,
        compiler_params=pltpu.CompilerParams(dimension_semantics=("arbitrary",),
                                             vmem_limit_bytes=MIXER_VMEM_BYTES),
        name="mixer_sample" if keep_v else "mixer_prompt",
    )(x, h0r, h0i, *consts, *cast_arrs)
    n_out = 3 + int(keep_v)
    out, sre, sim = res[:3]
    return out, sre, sim, (res[3] if keep_v else None), [(a[None], 0) for a in res[n_out:]]


FFN_INPUTS = 9


def _ffn_kernel(*refs, final, prompt_steps, n_cast):
    hp_ref, hs_ref, pp_ref, ps_ref, vec_ref, w_up_ref, w_down_ref, w_gate_ref, w_ple_ref = refs[:FFN_INPUTS]
    n_in = FFN_INPUTS + n_cast
    outp_ref, outs_ref = refs[n_in:n_in + 2]
    _cast_rows(refs[FFN_INPUTS:n_in], refs[n_in + 2:])
    g_ffn, g_ple, g_fin = (_vec_row(vec_ref, FFN_VECS, n, D_MODEL) for n in FFN_VECS)

    def rows_block(h_ref, p_ref, out_ref):
        hr = h_ref.shape[0] // FFN_SPLIT
        parts = [slice(k * hr, (k + 1) * hr) for k in range(FFN_SPLIT)]
        accs = [h_ref[rs, :] for rs in parts]
        fs = [_rmsnorm(h, g_ffn).astype(_BF16) for h in accs]
        for c in range(D_FF // FFN_COLS):
            for k in range(FFN_SPLIT):
                up = _dot(fs[k], w_up_ref[:, c * FFN_COLS:(c + 1) * FFN_COLS])
                act = jnp.square(jnp.maximum(up, 0.0)).astype(_BF16)
                accs[k] = accs[k] + _dot(act, w_down_ref[c * FFN_COLS:(c + 1) * FFN_COLS, :])
        for k, rs in enumerate(parts):
            acc = accs[k]
            gate = jax.nn.sigmoid(_dot(_rmsnorm(acc, g_ple).astype(_BF16), w_gate_ref[...]))
            out = acc + gate * _dot(p_ref[rs, :].astype(_BF16), w_ple_ref[...])
            if final:
                out = _rmsnorm(out, g_fin)
            out_ref[rs, :] = out

    is_prompt = pl.program_id(0) < prompt_steps
    pl.when(is_prompt)(functools.partial(rows_block, hp_ref, pp_ref, outp_ref))
    pl.when(jnp.logical_not(is_prompt))(functools.partial(rows_block, hs_ref, ps_ref, outs_ref))


def _ffn(h_p, h_s, p_p, p_s, w, layer, *, final, cast=()):
    rows_s = min(FFN_ROWS, h_s.shape[0])
    steps_p = h_p.shape[0] // FFN_ROWS
    steps_s = h_s.shape[0] // rows_s
    consts, const_specs = _weights(w, FFN_WEIGHTS, layer)
    assert len(consts) + 4 == FFN_INPUTS
    prompt_blk = lambda j: jnp.minimum(j, steps_p - 1)
    sample_blk = lambda j: jnp.maximum(j - steps_p, 0)
    cast_arrs, cast_in, cast_out, cast_shape = _cast_specs(cast, steps_p, prompt_blk)
    res = pl.pallas_call(
        functools.partial(_ffn_kernel, final=final, prompt_steps=steps_p, n_cast=len(cast_arrs)),
        grid=(steps_p + steps_s,),
        in_specs=([pl.BlockSpec((FFN_ROWS, D_MODEL), lambda j: (prompt_blk(j), 0)),
                   pl.BlockSpec((rows_s, D_MODEL), lambda j: (sample_blk(j), 0),
                                pipeline_mode=pl.Buffered(1)),
                   pl.BlockSpec((None, FFN_ROWS, PLE_DIM), lambda j: (layer, prompt_blk(j), 0)),
                   pl.BlockSpec((None, rows_s, PLE_DIM), lambda j: (layer, sample_blk(j), 0),
                                pipeline_mode=pl.Buffered(1))]
                  + const_specs + cast_in),
        out_specs=tuple([pl.BlockSpec((FFN_ROWS, D_MODEL), lambda j: (prompt_blk(j), 0)),
                         pl.BlockSpec((rows_s, D_MODEL), lambda j: (sample_blk(j), 0))] + cast_out),
        out_shape=tuple([jax.ShapeDtypeStruct(h_p.shape, _F32), jax.ShapeDtypeStruct(h_s.shape, _F32)]
                        + cast_shape),
        compiler_params=pltpu.CompilerParams(dimension_semantics=("arbitrary",),
                                             vmem_limit_bytes=FFN_VMEM_BYTES),
        name="ffn",
    )(h_p, h_s, p_p, p_s, *consts, *cast_arrs)
    return res[0], res[1], [(a[None], 0) for a in res[2:]]


def kernel(x_prompt, x_sample, state_ssm_re, state_ssm_im, p_prompt, p_sample, g_mix, w_in, lam_re, lam_im, log_dt, b_re, b_im, c_re, c_im, d_skip, w_glu, b_glu, g_v, b_v, w_s, b_s, g_out_s, g_out_g, w_out, g_ffn, w_up, w_down, g_ple, w_ple_gate, w_ple, g_final):
    batch, seq, _ = x_prompt.shape
    dec_batch, dec_seq, _ = x_sample.shape
    n_p, n_s = batch * seq, dec_batch * dec_seq
    assert seq % CHUNK == 0 and batch == SUBLANES
    assert dec_seq % S5_BLOCK == 0 and CHUNK % dec_seq == 0 and n_s % max(IN_TILE, OUT_TILE) == 0

    mixer_mats = dict(w_in=w_in, w_glu=w_glu, w_out=w_out)
    ffn_mats = dict(w_up=w_up, w_down=w_down, w_ple_gate=w_ple_gate, w_ple=w_ple)
    abr, abi, bbr, bbi = _discretise(lam_re, lam_im, log_dt, b_re, b_im)
    w_v, w_y, a_blk, cast = _s5_matrices(abr, abi, bbr, bbi, c_re, c_im,
                                         cast=[(a, 0) for a in mixer_mats.values()])

    def pack(vecs):
        return jnp.stack([jnp.pad(v, ((0, 0), (0, D_MODEL - v.shape[-1]))) for v in vecs], axis=1)
    head_of = (jnp.arange(GMLP_WIDTH)[None, :] // GMLP_HEAD_DIM == jnp.arange(GMLP_HEADS)[:, None]).astype(_F32)
    pos_of = (jnp.arange(CHUNK)[:, None] % dec_seq == jnp.arange(dec_seq)[None, :]).astype(_F32)
    w_s4 = jnp.tril(w_s[:, :, :dec_seq, :dec_seq])
    w = dict(
        mixer_vecs=pack([g_mix, d_skip, b_glu, g_v, b_v, g_out_s, g_out_g]),
        ffn_vecs=pack([g_ffn, g_ple, jnp.broadcast_to(g_final, (DEPTH, D_MODEL))]),
        a_blk=a_blk, w_v=w_v, w_y=w_y,
        mixw_p=w_s,
        mixb_p=jnp.einsum('dht,hc->dtc', b_s, head_of, precision=_EXACT),
        mixw_s=jnp.einsum('rt,dhts,cs->dhrc', pos_of, w_s4, pos_of, precision=_EXACT),
        mixb_s=jnp.einsum('dht,rt,hc->drc', b_s[:, :, :dec_seq], pos_of, head_of, precision=_EXACT))

    zeros = jnp.zeros((1, batch, SSM_FLAT), _F32)
    s0_re = state_ssm_re.reshape(DEPTH, dec_batch, SSM_FLAT)
    s0_im = state_ssm_im.reshape(DEPTH, dec_batch, SSM_FLAT)
    pp = p_prompt.reshape(DEPTH, n_p, PLE_DIM)
    ps = p_sample.reshape(DEPTH, n_s, PLE_DIM)
    h_p = x_prompt
    h_s = x_sample.reshape(1, n_s, D_MODEL)
    re_p, im_p, re_s, im_s, v_s = [], [], [], [], []
    w.update(zip(mixer_mats, cast))
    for i in range(DEPTH):
        final = i == DEPTH - 1
        h_p, sre, sim, _, cast = _mixer(h_p, zeros, zeros, w, i, h0_layer=0, nseq=batch, t_len=CHUNK,
                                        blk_pitch=CHUNK // S5_BLOCK + SCAN_PAD_BLOCKS, keep_v=False,
                                        cast=[(a, i) for a in ffn_mats.values()])
        w.update(zip(ffn_mats, cast))
        re_p.append(sre)
        im_p.append(sim)
        h_s, sre, sim, vn, _ = _mixer(h_s, s0_re, s0_im, w, i, h0_layer=i, nseq=dec_batch, t_len=dec_seq,
                                      blk_pitch=dec_seq // S5_BLOCK, keep_v=True)
        re_s.append(sre)
        im_s.append(sim)
        v_s.append(vn)
        h_p, h_s, cast = _ffn(h_p.reshape(n_p, D_MODEL), h_s.reshape(n_s, D_MODEL), pp, ps, w, i, final=final,
                              cast=[] if final else [(a, i + 1) for a in mixer_mats.values()])
        w.update(zip(mixer_mats, cast))
        h_p = h_p.reshape(batch, seq, D_MODEL)
        h_s = h_s.reshape(1, n_s, D_MODEL)

    st_p = (DEPTH, batch, SSM_GROUPS, SSM_STATE)
    st_s = (DEPTH, dec_batch, SSM_GROUPS, SSM_STATE)
    return (h_p, h_s.reshape(dec_batch, dec_seq, D_MODEL),
            jnp.stack(re_p).reshape(st_p), jnp.stack(im_p).reshape(st_p),
            jnp.stack(re_s).reshape(st_s), jnp.stack(im_s).reshape(st_s),
            jnp.stack(v_s).reshape(DEPTH, dec_batch, dec_seq, GMLP_WIDTH))
```

```python
import functools
import math

import jax
import jax.numpy as jnp
from jax import lax
from jax.experimental import pallas as pl
from jax.experimental.pallas import tpu as pltpu

D_MODEL = 1024
DEPTH = 2
SSM_WIDTH = 512
GMLP_WIDTH = 512
SSM_GROUP = 16
SSM_GROUPS = 32
SSM_STATE = 64
SSM_FLAT = SSM_GROUPS * SSM_STATE
CHUNK = 128
GMLP_HEADS = 4
GMLP_HEAD_DIM = 128
PLE_DIM = 256
D_FF = 4096
EPS = 1e-6
LAM_RE_MAX = -1e-4

LANES = 128
SUBLANES = 8
S5_SLICES = SSM_WIDTH // LANES
S5_SUBS = 2 * S5_SLICES
S5_SUB_CH = SSM_WIDTH // S5_SUBS
S5_SUB_STATES = SSM_FLAT // S5_SUBS
S5_SUB_SLABS = S5_SUB_STATES // LANES
S5_BLOCK = 4
SCAN_PAD_BLOCKS = 4
IN_TILE = 128
OUT_TILE = 256
FFN_ROWS = 512
FFN_COLS = 1024
FFN_SPLIT = 2
MIXER_VMEM_BYTES = 56 * 1024 * 1024
FFN_VMEM_BYTES = 52 * 1024 * 1024

_F32 = jnp.float32
_BF16 = jnp.bfloat16
_EXACT = lax.Precision.HIGHEST


def _dot(a, b):
    return jnp.dot(a, b, preferred_element_type=_F32)


def _rmsnorm(x, g):
    r = lax.rsqrt(jnp.mean(x * x, axis=-1, keepdims=True) + EPS)
    return x * r * g


def _gelu(x):
    c = math.sqrt(2.0 / math.pi)
    return x * (0.5 * (1.0 + jnp.tanh(c * (x + 0.044715 * (x * x * x)))))


def _disc_kernel(lr_ref, li_ref, ldt_ref, br_ref, bi_ref, abr_ref, abi_ref, bbr_ref, bbi_ref):
    lr = jnp.minimum(lr_ref[...], LAM_RE_MAX)
    li = li_ref[...]
    dt = jnp.exp(ldt_ref[...])
    mag = jnp.exp(lr * dt)
    abr = mag * jnp.cos(li * dt)
    abi = mag * jnp.sin(li * dt)
    den = lr * lr + li * li
    nr = abr - 1.0
    ni = abi
    qr = (nr * lr + ni * li) / den
    qi = (ni * lr - nr * li) / den
    abr_ref[...] = abr
    abi_ref[...] = abi
    for h in range(SSM_GROUP):
        br = br_ref[h]
        bi = bi_ref[h]
        bbr_ref[h] = qr * br - qi * bi
        bbi_ref[h] = qr * bi + qi * br


def _discretise(lam_re, lam_im, log_dt, b_re, b_im):
    dg = DEPTH * SSM_GROUPS
    lr = lam_re.reshape(dg, SSM_STATE)
    li = lam_im.reshape(dg, SSM_STATE)
    ldt = log_dt.reshape(dg, 1)
    br = jnp.moveaxis(b_re, -1, 0).reshape(SSM_GROUP, dg, SSM_STATE)
    bi = jnp.moveaxis(b_im, -1, 0).reshape(SSM_GROUP, dg, SSM_STATE)
    vec = jax.ShapeDtypeStruct((dg, SSM_STATE), _F32)
    mat = jax.ShapeDtypeStruct((SSM_GROUP, dg, SSM_STATE), _F32)
    abr, abi, bbr, bbi = pl.pallas_call(
        _disc_kernel, out_shape=(vec, vec, mat, mat), name="s5_discretise")(lr, li, ldt, br, bi)
    vshp = (DEPTH, SSM_GROUPS, SSM_STATE)
    return abr.reshape(vshp), abi.reshape(vshp), bbr, bbi


def _cmul(a, b):
    return a[0] * b[0] - a[1] * b[1], a[0] * b[1] + a[1] * b[0]


PREPARE_INPUTS = 6


def _prepare_kernel(*refs):
    a_ref, bb_ref, cre_ref, cim_ref, t_ref, tt_ref = refs[:PREPARE_INPUTS]
    n_cast = (len(refs) - PREPARE_INPUTS - 2) // 2
    wv_ref, wy_ref = refs[PREPARE_INPUTS + n_cast:PREPARE_INPUTS + n_cast + 2]
    _cast_rows(refs[PREPARE_INPUTS:PREPARE_INPUTS + n_cast], refs[PREPARE_INPUTS + n_cast + 2:])
    r = S5_BLOCK
    ch = S5_SUB_CH
    n_st = S5_SUB_STATES
    g_shift = SSM_GROUP.bit_length() - 1
    p_shift = SSM_STATE.bit_length() - 1

    def group_mask(n_rows, row_shift, n_cols, col_width, col_shift):
        rg = lax.broadcasted_iota(jnp.int32, (n_rows, n_cols), 0) >> row_shift
        cg = (lax.broadcasted_iota(jnp.int32, (n_rows, n_cols), 1) & (col_width - 1)) >> col_shift
        return rg == cg

    mask_v = group_mask(ch, g_shift, n_st, n_st, p_shift)
    mask_s = group_mask(n_st, p_shift, r * ch, ch, g_shift)
    mask_u = group_mask(ch, g_shift, r * ch, ch, g_shift)

    def dot_t(x, y, precision=None):
        return lax.dot_general(x, y, (((1,), (1,)), ((), ())), precision=precision,
                               preferred_element_type=_F32)

    def expand(x):
        return jnp.where(mask_v, _dot(x.astype(_BF16), t_ref[...]), 0.0)

    for sub in range(S5_SUBS):
        rows = slice(sub * ch, (sub + 1) * ch)
        a1 = (a_ref[0, rows, :], a_ref[1, rows, :])
        bb = (bb_ref[0, rows, :], bb_ref[1, rows, :])
        cc = (cre_ref[rows, :], cim_ref[rows, :])
        apow = [None, a1]
        for _ in range(r - 1):
            apow.append(_cmul(apow[-1], a1))
        ab = [bb] + [_cmul(apow[k], bb) for k in range(1, r)]
        ca = [cc] + [_cmul(cc, apow[k]) for k in range(1, r + 1)]
        for m in range(r):
            band = jnp.concatenate([expand(ab[r - 1 - m][0]), expand(ab[r - 1 - m][1])], axis=-1)
            wv_ref[sub, m * ch:(m + 1) * ch, :] = band.astype(_BF16)
        for part in range(2):
            ca_all = jnp.concatenate([ca[i + 1][part] for i in range(r)], axis=0).astype(_BF16)
            band = jnp.where(mask_s, dot_t(tt_ref[...], ca_all), 0.0)
            wy_ref[sub, part * n_st:(part + 1) * n_st, :] = (band if part == 0 else -band).astype(_BF16)
        none = jnp.zeros((ch, SSM_STATE), _F32)
        for m in range(r):
            shifted = [jnp.concatenate([ca[i - m][part] if i >= m else none for i in range(r)], axis=0)
                       for part in range(2)]
            band = dot_t(bb[0], shifted[0], _EXACT) - dot_t(bb[1], shifted[1], _EXACT)
            wy_ref[sub, 2 * n_st + m * ch:2 * n_st + (m + 1) * ch, :] = (
                jnp.where(mask_u, band, 0.0).astype(_BF16))


def _s5_matrices(abr, abi, bbr, bbi, c_re, c_im, cast=()):
    r = S5_BLOCK
    n_rows = SSM_GROUPS * SSM_GROUP
    a_rep = jnp.repeat(jnp.stack([abr, abi], axis=1), SSM_GROUP, axis=2)
    bb = jnp.stack([bbr, bbi]).reshape(2, SSM_GROUP, DEPTH, SSM_GROUPS, SSM_STATE)
    bb = jnp.transpose(bb, (2, 0, 3, 1, 4)).reshape(DEPTH, 2, n_rows, SSM_STATE)
    cre = c_re.reshape(DEPTH, n_rows, SSM_STATE)
    cim = c_im.reshape(DEPTH, n_rows, SSM_STATE)
    t_state = jnp.tile(jnp.eye(SSM_STATE, dtype=_BF16), (1, SSM_GROUPS // S5_SUBS))

    def per_layer(a):
        nd = a.ndim - 1
        return pl.BlockSpec((None,) + a.shape[1:], lambda d: (d,) + (0,) * nd)

    v_shape = (DEPTH, S5_SUBS, r * S5_SUB_CH, 2 * S5_SUB_STATES)
    y_shape = (DEPTH, S5_SUBS, 2 * S5_SUB_STATES + r * S5_SUB_CH, r * S5_SUB_CH)
    cast_arrs, cast_in, cast_out, cast_shape = _cast_specs(cast, DEPTH, lambda d: d)
    w_v, w_y, *cast_res = pl.pallas_call(
        _prepare_kernel,
        grid=(DEPTH,),
        in_specs=[per_layer(a_rep), per_layer(bb), per_layer(cre), per_layer(cim),
                  pl.BlockSpec(t_state.shape, lambda d: (0, 0)),
                  pl.BlockSpec(t_state.shape[::-1], lambda d: (0, 0))] + cast_in,
        out_specs=tuple([pl.BlockSpec((None,) + v_shape[1:], lambda d: (d, 0, 0, 0)),
                         pl.BlockSpec((None,) + y_shape[1:], lambda d: (d, 0, 0, 0))] + cast_out),
        out_shape=tuple([jax.ShapeDtypeStruct(v_shape, _BF16), jax.ShapeDtypeStruct(y_shape, _BF16)]
                        + cast_shape),
        name="s5_prepare",
    )(a_rep, bb, cre, cim, t_state, t_state.T, *cast_arrs)
    a_pow = (abr, abi)
    for _ in range(r - 1):
        a_pow = _cmul(a_pow, (abr, abi))
    a_blk = jnp.stack(a_pow, axis=1).reshape(DEPTH, 2, SSM_FLAT)
    return w_v, w_y, a_blk, [(a[None], 0) for a in cast_res]


def _tile_rows(ref, i, rows):
    nb, tb, c = ref.shape
    if nb == 1:
        return ref[0, i * rows:(i + 1) * rows, :]
    per = rows // tb
    return ref[i * per:(i + 1) * per].reshape(rows, c)


def _store_tile_rows(ref, i, rows, val):
    nb, tb, c = ref.shape
    if nb == 1:
        ref[0, i * rows:(i + 1) * rows, :] = val
    else:
        per = rows // tb
        ref[i * per:(i + 1) * per] = val.reshape(per, tb, c)


def _load_slabs(ref, i, rows, t_len, seq_rows):
    if seq_rows == t_len:
        return jnp.concatenate([ref[s, i * rows:(i + 1) * rows, :] for s in range(S5_SLICES)], axis=-1)
    per = rows // t_len
    return jnp.concatenate(
        [jnp.concatenate([ref[s, n * seq_rows:n * seq_rows + t_len, :] for s in range(S5_SLICES)], axis=-1)
         for n in range(i * per, (i + 1) * per)], axis=0)


def _store_slabs(ref, i, rows, t_len, seq_rows, val):
    for s in range(S5_SLICES):
        cols = slice(s * LANES, (s + 1) * LANES)
        if seq_rows == t_len:
            ref[s, i * rows:(i + 1) * rows, :] = val[:, cols]
        else:
            per = rows // t_len
            for k in range(per):
                n = i * per + k
                ref[s, n * seq_rows:n * seq_rows + t_len, :] = val[k * t_len:(k + 1) * t_len, cols]


MIXER_INPUTS = 12
MIXER_VECS = ('g_mix', 'd_skip', 'b_glu', 'g_v', 'b_v', 'g_out_s', 'g_out_g')
FFN_VECS = ('g_ffn', 'g_ple', 'g_final')


def _vec_row(ref, names, name, width):
    k = names.index(name)
    return ref[k:k + 1, :width]


def _cast_rows(in_refs, out_refs):
    for src, dst in zip(in_refs, out_refs):
        dst[...] = src[...].astype(_BF16)


MIXER_LATE_WEIGHTS = ('w_v', 'w_y', 'w_glu', 'w_out')


def _mixer_kernel(*refs, nseq, t_len, blk_pitch, keep_v, n_cast, late_layers):
    (x_ref, h0r_ref, h0i_ref, vec_ref, w_in_ref, a_blk_ref, wv_hbm, wy_hbm,
     w_glu_hbm, mixw_ref, mixb_ref, w_out_hbm) = refs[:MIXER_INPUTS]
    g_mix, d_skip, b_glu, g_v, b_v, g_out_s, g_out_g = (
        _vec_row(vec_ref, MIXER_VECS, n, D_MODEL if n == 'g_mix' else SSM_WIDTH) for n in MIXER_VECS)
    n_in = MIXER_INPUTS + n_cast
    out_ref, sre_ref, sim_ref = refs[n_in:n_in + 3]
    n_out = 3 + int(keep_v)
    if keep_v:
        vn_ref = refs[n_in + 3]
    us_scr, ys_scr, yg_scr, hb2_scr, wv_ref, wy_ref, w_glu_ref, w_out_ref, late_sem = refs[n_in + n_out + n_cast:]
    late_copies = [pltpu.make_async_copy(src.at[layer], dst, late_sem.at[k])
                   for k, (src, dst, layer) in enumerate(zip((wv_hbm, wy_hbm, w_glu_hbm, w_out_hbm),
                                                             (wv_ref, wy_ref, w_glu_ref, w_out_ref), late_layers))]
    rows = nseq * t_len
    n_in_tiles = rows // IN_TILE
    n_out_tiles = rows // OUT_TILE
    n_blk = t_len // S5_BLOCK
    seq_rows = S5_BLOCK * blk_pitch
    n_buf = nseq * blk_pitch

    @pl.when(pl.program_id(0) == 0)
    def _():
        for k, cp in enumerate(late_copies):
            cp.start(priority=k % 2)
        sre_ref[...] = h0r_ref[...]
        sim_ref[...] = h0i_ref[...]
        us_scr[...] = jnp.zeros_like(us_scr)

    row = lax.broadcasted_iota(jnp.int32, (CHUNK, CHUNK), 0)
    col = lax.broadcasted_iota(jnp.int32, (CHUNK, CHUNK), 1)
    t_shift = t_len.bit_length() - 1
    causal = ((row >> t_shift) == (col >> t_shift)) & ((col & (t_len - 1)) <= (row & (t_len - 1)))
    mixw = [jnp.where(causal, mixw_ref[h], 0.0).astype(_BF16) for h in range(GMLP_HEADS)]

    def in_proj(i):
        x = _tile_rows(x_ref, i, IN_TILE)
        a = _rmsnorm(x, g_mix).astype(_BF16)
        return _dot(a, w_in_ref[...])

    lane_lo = lax.broadcasted_iota(jnp.int32, (n_buf, LANES), 1) < S5_SUB_CH

    def regroup_halves(blocks):
        rolled = [pltpu.roll(x, S5_SUB_CH, axis=1) for x in blocks]
        pairs = range(0, len(blocks), 2)
        lo = [jnp.where(lane_lo, blocks[k], rolled[k + 1]) for k in pairs]
        hi = [jnp.where(lane_lo, rolled[k], blocks[k + 1]) for k in pairs]
        return jnp.concatenate(lo, axis=-1), jnp.concatenate(hi, axis=-1)

    def block_inputs(sub, u_halves):
        if sub % 2 == 0:
            s = sub // 2
            u_halves = regroup_halves([us_scr[s, pl.ds(m, n_buf, stride=S5_BLOCK), :]
                                       for m in range(S5_BLOCK)])
        u_blk = u_halves[sub % 2].astype(_BF16)
        v_in = _dot(u_blk, wv_ref[sub])
        for c in range(2 * S5_SUB_SLABS):
            hb2_scr[sub % 2, c] = v_in[:, c * LANES:(c + 1) * LANES]
        return u_blk, u_halves

    z_next = in_proj(0)
    for i in range(n_in_tiles):
        z = z_next
        if i + 1 < n_in_tiles:
            z_next = in_proj(i + 1)
        else:
            @pl.when(pl.program_id(0) == 0)
            def _():
                for cp in late_copies:
                    cp.wait()
        _store_slabs(us_scr, i, IN_TILE, t_len, seq_rows, z[:, :SSM_WIDTH])
        if i + 1 == n_in_tiles:
            u_next, u_halves = block_inputs(0, None)
        zg = _gelu(z[:, SSM_WIDTH:])
        ug = zg[:, :GMLP_WIDTH]
        v = zg[:, GMLP_WIDTH:]
        mu = jnp.mean(v, axis=-1, keepdims=True)
        vc = v - mu
        r = lax.rsqrt(jnp.mean(vc * vc, axis=-1, keepdims=True) + EPS)
        vn = vc * r * g_v + b_v
        if keep_v:
            vn_ref[i * IN_TILE:(i + 1) * IN_TILE, :] = vn
        vn = vn.astype(_BF16)
        for b in range(IN_TILE // CHUNK):
            rb = slice(b * CHUNK, (b + 1) * CHUNK)
            mixed = [_dot(mixw[h], vn[rb, h * GMLP_HEAD_DIM:(h + 1) * GMLP_HEAD_DIM])
                     for h in range(GMLP_HEADS)]
            m = jnp.concatenate(mixed, axis=-1) + mixb_ref[...]
            yg = _rmsnorm(ug[rb, :] * m, g_out_g).astype(_BF16)
            yg_scr[i * IN_TILE + b * CHUNK:i * IN_TILE + (b + 1) * CHUNK, :] = yg

    _cast_rows(refs[MIXER_INPUTS:n_in], refs[n_in + n_out:n_in + n_out + n_cast])
    y_prev = None
    for sub in range(S5_SUBS):
        u_blk = u_next
        if sub + 1 < S5_SUBS:
            u_next, u_halves = block_inputs(sub + 1, u_halves)
        hb_scr = hb2_scr.at[sub % 2]
        s0 = sub * S5_SUB_STATES
        a_re = [jnp.broadcast_to(a_blk_ref[0:1, s0 + c * LANES:s0 + (c + 1) * LANES], (nseq, LANES))
                for c in range(S5_SUB_SLABS)]
        a_im = [jnp.broadcast_to(a_blk_ref[1:2, s0 + c * LANES:s0 + (c + 1) * LANES], (nseq, LANES))
                for c in range(S5_SUB_SLABS)]
        h_init = tuple(
            (sre_ref[:, s0 + c * LANES:s0 + (c + 1) * LANES], sim_ref[:, s0 + c * LANES:s0 + (c + 1) * LANES])
            for c in range(S5_SUB_SLABS))

        def scan_step(j, h):
            new = []
            for c in range(S5_SUB_SLABS):
                rows_j = pl.ds(j, nseq, stride=blk_pitch)
                hr, hi = h[c]
                v_re = hb_scr[c, rows_j, :]
                v_im = hb_scr[S5_SUB_SLABS + c, rows_j, :]
                hb_scr[c, rows_j, :] = hr
                hb_scr[S5_SUB_SLABS + c, rows_j, :] = hi
                new.append((a_re[c] * hr - a_im[c] * hi + v_re, a_re[c] * hi + a_im[c] * hr + v_im))
            return tuple(new)
        h_fin = h_init
        for j in range(n_blk):
            h_fin = scan_step(j, h_fin)
        for c in range(S5_SUB_SLABS):
            sre_ref[:, s0 + c * LANES:s0 + (c + 1) * LANES] = h_fin[c][0]
            sim_ref[:, s0 + c * LANES:s0 + (c + 1) * LANES] = h_fin[c][1]

        lhs = jnp.concatenate([hb_scr[c].astype(_BF16) for c in range(2 * S5_SUB_SLABS)] + [u_blk], axis=-1)
        y_sub = _dot(lhs, wy_ref[sub])
        if sub % 2 == 0:
            y_prev = y_sub
        else:
            pieces = [y[:, k * LANES:(k + 1) * LANES] for k in range(S5_BLOCK // 2) for y in (y_prev, y_sub)]
            for first, y_rows in enumerate(regroup_halves(pieces)):
                for k in range(S5_BLOCK // 2):
                    ys_scr[sub // 2, pl.ds(2 * k + first, n_buf, stride=S5_BLOCK), :] = (
                        y_rows[:, k * LANES:(k + 1) * LANES])

    def glu_in(i):
        ys = (_load_slabs(ys_scr, i, OUT_TILE, t_len, seq_rows)
              + d_skip * _load_slabs(us_scr, i, OUT_TILE, t_len, seq_rows))
        ys = _gelu(ys)
        return ys, _dot(ys.astype(_BF16), w_glu_ref[...])

    nxt = glu_in(0)
    for i in range(n_out_tiles):
        ys, gate = nxt
        if i + 1 < n_out_tiles:
            nxt = glu_in(i + 1)
        ys = ys * jax.nn.sigmoid(gate + b_glu)
        ys = _rmsnorm(ys, g_out_s).astype(_BF16)
        y = (_dot(ys, w_out_ref[:SSM_WIDTH, :])
             + _dot(yg_scr[i * OUT_TILE:(i + 1) * OUT_TILE, :], w_out_ref[SSM_WIDTH:, :]))
        _store_tile_rows(out_ref, i, OUT_TILE, _tile_rows(x_ref, i, OUT_TILE) + y)


def _layer_spec(arr, layer):
    nd = arr.ndim - 1
    return pl.BlockSpec((None,) + arr.shape[1:], lambda j: (layer,) + (0,) * nd,
                        pipeline_mode=pl.Buffered(1))


MIXER_WEIGHTS = ('mixer_vecs', 'w_in', 'a_blk', 'w_v', 'w_y', 'w_glu', 'mixw', 'mixb', 'w_out')
FFN_WEIGHTS = ('ffn_vecs', 'w_up', 'w_down', 'w_ple_gate', 'w_ple')


def _weights(w, names, layer):
    picked = [w[k] if isinstance(w[k], tuple) else (w[k], layer) for k in names]
    return [a for a, _ in picked], [_layer_spec(a, l) for a, l in picked]


def _cast_specs(cast, steps, blk_of):
    arrs, in_specs, out_specs, out_shape = [], [], [], []
    for a, layer in cast:
        _, r, c = a.shape
        rb = r // steps
        assert rb * steps == r and rb % (2 * SUBLANES) == 0
        arrs.append(a)
        in_specs.append(pl.BlockSpec((None, rb, c), lambda j, layer=layer: (layer, blk_of(j), 0)))
        out_specs.append(pl.BlockSpec((rb, c), lambda j: (blk_of(j), 0)))
        out_shape.append(jax.ShapeDtypeStruct((r, c), _BF16))
    return arrs, in_specs, out_specs, out_shape


def _mixer(x, h0r, h0i, w, layer, *, h0_layer, nseq, t_len, blk_pitch, keep_v, cast=()):
    nb, length, _ = x.shape
    t_blk = t_len if nb == nseq else nseq * t_len
    rows = nseq * t_len
    steps = length // t_blk
    slab_rows = nseq * S5_BLOCK * blk_pitch
    x_spec = pl.BlockSpec((nb, t_blk, D_MODEL), lambda j: (0, j, 0))
    st_spec = pl.BlockSpec((nseq, SSM_FLAT), lambda j: (0, 0))
    names = [k + ('_s' if keep_v else '_p') if k in ('mixw', 'mixb') else k for k in MIXER_WEIGHTS]
    consts, const_specs = _weights(w, names, layer)
    assert len(consts) + 3 == MIXER_INPUTS
    late = [names.index(k) for k in MIXER_LATE_WEIGHTS]
    late_layers = tuple(0 if isinstance(w[names[k]], tuple) else layer for k in late)
    for k in late:
        const_specs[k] = pl.BlockSpec(memory_space=pl.ANY)
    cast_arrs, cast_in, cast_out, cast_shape = _cast_specs(cast, steps, lambda j: j)
    out_specs = [x_spec, st_spec, st_spec]
    out_shape = [jax.ShapeDtypeStruct(x.shape, _F32),
                 jax.ShapeDtypeStruct((nseq, SSM_FLAT), _F32),
                 jax.ShapeDtypeStruct((nseq, SSM_FLAT), _F32)]
    scratch = [pltpu.VMEM((S5_SLICES, slab_rows, LANES), _F32),
               pltpu.VMEM((S5_SLICES, slab_rows, LANES), _F32),
               pltpu.VMEM((rows, GMLP_WIDTH), _BF16),
               pltpu.VMEM((2, 2 * S5_SUB_SLABS, nseq * blk_pitch, LANES), _F32)]
    scratch += [pltpu.VMEM(consts[k].shape[1:], consts[k].dtype) for k in late]
    scratch.append(pltpu.SemaphoreType.DMA((len(late),)))
    if keep_v:
        out_specs.append(pl.BlockSpec((rows, GMLP_WIDTH), lambda j: (j, 0)))
        out_shape.append(jax.ShapeDtypeStruct((steps * rows, GMLP_WIDTH), _F32))
    res = pl.pallas_call(
        functools.partial(_mixer_kernel, nseq=nseq, t_len=t_len, blk_pitch=blk_pitch, keep_v=keep_v,
                          n_cast=len(cast_arrs), late_layers=late_layers),
        grid=(steps,),
        in_specs=([x_spec, _layer_spec(h0r, h0_layer), _layer_spec(h0i, h0_layer)] + const_specs + cast_in),
        out_specs=tuple(out_specs + cast_out),
        out_shape=tuple(out_shape + cast_shape),
        scratch_shapes=scratch,
        compiler_params=pltpu.CompilerParams(dimension_semantics=("arbitrary",),
                                             vmem_limit_bytes=MIXER_VMEM_BYTES),
        name="mixer_sample" if keep_v else "mixer_prompt",
    )(x, h0r, h0i, *consts, *cast_arrs)
    n_out = 3 + int(keep_v)
    out, sre, sim = res[:3]
    return out, sre, sim, (res[3] if keep_v else None), [(a[None], 0) for a in res[n_out:]]


FFN_INPUTS = 9


def _ffn_kernel(*refs, final, prompt_steps, n_cast):
    hp_ref, hs_ref, pp_ref, ps_ref, vec_ref, w_up_ref, w_down_ref, w_gate_ref, w_ple_ref = refs[:FFN_INPUTS]
    n_in = FFN_INPUTS + n_cast
    outp_ref, outs_ref = refs[n_in:n_in + 2]
    _cast_rows(refs[FFN_INPUTS:n_in], refs[n_in + 2:])
    g_ffn, g_ple, g_fin = (_vec_row(vec_ref, FFN_VECS, n, D_MODEL) for n in FFN_VECS)

    def rows_block(h_ref, p_ref, out_ref):
        hr = h_ref.shape[0] // FFN_SPLIT
        parts = [slice(k * hr, (k + 1) * hr) for k in range(FFN_SPLIT)]
        accs = [h_ref[rs, :] for rs in parts]
        fs = [_rmsnorm(h, g_ffn).astype(_BF16) for h in accs]
        for c in range(D_FF // FFN_COLS):
            for k in range(FFN_SPLIT):
                up = _dot(fs[k], w_up_ref[:, c * FFN_COLS:(c + 1) * FFN_COLS])
                act = jnp.square(jnp.maximum(up, 0.0)).astype(_BF16)
                accs[k] = accs[k] + _dot(act, w_down_ref[c * FFN_COLS:(c + 1) * FFN_COLS, :])
        for k, rs in enumerate(parts):
            acc = accs[k]
            gate = jax.nn.sigmoid(_dot(_rmsnorm(acc, g_ple).astype(_BF16), w_gate_ref[...]))
            out = acc + gate * _dot(p_ref[rs, :].astype(_BF16), w_ple_ref[...])
            if final:
                out = _rmsnorm(out, g_fin)
            out_ref[rs, :] = out

    is_prompt = pl.program_id(0) < prompt_steps
    pl.when(is_prompt)(functools.partial(rows_block, hp_ref, pp_ref, outp_ref))
    pl.when(jnp.logical_not(is_prompt))(functools.partial(rows_block, hs_ref, ps_ref, outs_ref))


def _ffn(h_p, h_s, p_p, p_s, w, layer, *, final, cast=()):
    rows_s = min(FFN_ROWS, h_s.shape[0])
    steps_p = h_p.shape[0] // FFN_ROWS
    steps_s = h_s.shape[0] // rows_s
    consts, const_specs = _weights(w, FFN_WEIGHTS, layer)
    assert len(consts) + 4 == FFN_INPUTS
    prompt_blk = lambda j: jnp.minimum(j, steps_p - 1)
    sample_blk = lambda j: jnp.maximum(j - steps_p, 0)
    cast_arrs, cast_in, cast_out, cast_shape = _cast_specs(cast, steps_p, prompt_blk)
    res = pl.pallas_call(
        functools.partial(_ffn_kernel, final=final, prompt_steps=steps_p, n_cast=len(cast_arrs)),
        grid=(steps_p + steps_s,),
        in_specs=([pl.BlockSpec((FFN_ROWS, D_MODEL), lambda j: (prompt_blk(j), 0)),
                   pl.BlockSpec((rows_s, D_MODEL), lambda j: (sample_blk(j), 0),
                                pipeline_mode=pl.Buffered(1)),
                   pl.BlockSpec((None, FFN_ROWS, PLE_DIM), lambda j: (layer, prompt_blk(j), 0)),
                   pl.BlockSpec((None, rows_s, PLE_DIM), lambda j: (layer, sample_blk(j), 0),
                                pipeline_mode=pl.Buffered(1))]
                  + const_specs + cast_in),
        out_specs=tuple([pl.BlockSpec((FFN_ROWS, D_MODEL), lambda j: (prompt_blk(j), 0)),
                         pl.BlockSpec((rows_s, D_MODEL), lambda j: (sample_blk(j), 0))] + cast_out),
        out_shape=tuple([jax.ShapeDtypeStruct(h_p.shape, _F32), jax.ShapeDtypeStruct(h_s.shape, _F32)]
                        + cast_shape),
        compiler_params=pltpu.CompilerParams(dimension_semantics=("arbitrary",),
                                             vmem_limit_bytes=FFN_VMEM_BYTES),
        name="ffn",
    )(h_p, h_s, p_p, p_s, *consts, *cast_arrs)
    return res[0], res[1], [(a[None], 0) for a in res[2:]]


def kernel(x_prompt, x_sample, state_ssm_re, state_ssm_im, p_prompt, p_sample, g_mix, w_in, lam_re, lam_im, log_dt, b_re, b_im, c_re, c_im, d_skip, w_glu, b_glu, g_v, b_v, w_s, b_s, g_out_s, g_out_g, w_out, g_ffn, w_up, w_down, g_ple, w_ple_gate, w_ple, g_final):
    batch, seq, _ = x_prompt.shape
    dec_batch, dec_seq, _ = x_sample.shape
    n_p, n_s = batch * seq, dec_batch * dec_seq
    assert seq % CHUNK == 0 and batch == SUBLANES
    assert dec_seq % S5_BLOCK == 0 and CHUNK % dec_seq == 0 and n_s % max(IN_TILE, OUT_TILE) == 0

    mixer_mats = dict(w_in=w_in, w_glu=w_glu, w_out=w_out)
    ffn_mats = dict(w_up=w_up, w_down=w_down, w_ple_gate=w_ple_gate, w_ple=w_ple)
    abr, abi, bbr, bbi = _discretise(lam_re, lam_im, log_dt, b_re, b_im)
    w_v, w_y, a_blk, cast = _s5_matrices(abr, abi, bbr, bbi, c_re, c_im,
                                         cast=[(a, 0) for a in mixer_mats.values()])

    def pack(vecs):
        return jnp.stack([jnp.pad(v, ((0, 0), (0, D_MODEL - v.shape[-1]))) for v in vecs], axis=1)
    head_of = (jnp.arange(GMLP_WIDTH)[None, :] // GMLP_HEAD_DIM == jnp.arange(GMLP_HEADS)[:, None]).astype(_F32)
    pos_of = (jnp.arange(CHUNK)[:, None] % dec_seq == jnp.arange(dec_seq)[None, :]).astype(_F32)
    w_s4 = jnp.tril(w_s[:, :, :dec_seq, :dec_seq])
    w = dict(
        mixer_vecs=pack([g_mix, d_skip, b_glu, g_v, b_v, g_out_s, g_out_g]),
        ffn_vecs=pack([g_ffn, g_ple, jnp.broadcast_to(g_final, (DEPTH, D_MODEL))]),
        a_blk=a_blk, w_v=w_v, w_y=w_y,
        mixw_p=w_s,
        mixb_p=jnp.einsum('dht,hc->dtc', b_s, head_of, precision=_EXACT),
        mixw_s=jnp.einsum('rt,dhts,cs->dhrc', pos_of, w_s4, pos_of, precision=_EXACT),
        mixb_s=jnp.einsum('dht,rt,hc->drc', b_s[:, :, :dec_seq], pos_of, head_of, precision=_EXACT))

    zeros = jnp.zeros((1, batch, SSM_FLAT), _F32)
    s0_re = state_ssm_re.reshape(DEPTH, dec_batch, SSM_FLAT)
    s0_im = state_ssm_im.reshape(DEPTH, dec_batch, SSM_FLAT)
    pp = p_prompt.reshape(DEPTH, n_p, PLE_DIM)
    ps = p_sample.reshape(DEPTH, n_s, PLE_DIM)
    h_p = x_prompt
    h_s = x_sample.reshape(1, n_s, D_MODEL)
    re_p, im_p, re_s, im_s, v_s = [], [], [], [], []
    w.update(zip(mixer_mats, cast))
    for i in range(DEPTH):
        final = i == DEPTH - 1
        h_p, sre, sim, _, cast = _mixer(h_p, zeros, zeros, w, i, h0_layer=0, nseq=batch, t_len=CHUNK,
                                        blk_pitch=CHUNK // S5_BLOCK + SCAN_PAD_BLOCKS, keep_v=False,
                                        cast=[(a, i) for a in ffn_mats.values()])
        w.update(zip(ffn_mats, cast))
        re_p.append(sre)
        im_p.append(sim)
        h_s, sre, sim, vn, _ = _mixer(h_s, s0_re, s0_im, w, i, h0_layer=i, nseq=dec_batch, t_len=dec_seq,
                                      blk_pitch=dec_seq // S5_BLOCK, keep_v=True)
        re_s.append(sre)
        im_s.append(sim)
        v_s.append(vn)
        h_p, h_s, cast = _ffn(h_p.reshape(n_p, D_MODEL), h_s.reshape(n_s, D_MODEL), pp, ps, w, i, final=final,
                              cast=[] if final else [(a, i + 1) for a in mixer_mats.values()])
        w.update(zip(mixer_mats, cast))
        h_p = h_p.reshape(batch, seq, D_MODEL)
        h_s = h_s.reshape(1, n_s, D_MODEL)

    st_p = (DEPTH, batch, SSM_GROUPS, SSM_STATE)
    st_s = (DEPTH, dec_batch, SSM_GROUPS, SSM_STATE)
    return (h_p, h_s.reshape(dec_batch, dec_seq, D_MODEL),
            jnp.stack(re_p).reshape(st_p), jnp.stack(im_p).reshape(st_p),
            jnp.stack(re_s).reshape(st_s), jnp.stack(im_s).reshape(st_s),
            jnp.stack(v_s).reshape(DEPTH, dec_batch, dec_seq, GMLP_WIDTH))
```
